```python
import math
import jax, jax.numpy as jnp
from jax import lax
import numpy as np

D_MODEL = 1024
BATCH = 1
SEQ = 16384
DEPTH = 4
DEC_BATCH = 8
DEC_SEQ = 4096
PAST_LEN = 128

GRID_W = 64
D_SSM = D_MODEL // 2
SSM_GROUP = 16
N_SSM_GROUPS = D_SSM // SSM_GROUP
SSM_STATE = 64
N_DIRS = 2
N_HEADS = 8
N_KV_HEADS = 2
HEAD_DIM = 64
D_ATTN = N_HEADS * HEAD_DIM
D_KV = N_KV_HEADS * HEAD_DIM
Q_BLOCK = 128
ROPE_THETA = 10000.0
N_EXPERT_GROUPS = 4
EXPERTS_PER_GROUP = 8
N_EXPERTS = N_EXPERT_GROUPS * EXPERTS_PER_GROUP
FINE_TOP_K = 2
D_EXPERT = 256
TOKEN_BLOCK = 128
D_IN = D_SSM + D_ATTN + 2 * D_KV + 2 * D_MODEL
N_MOD = 6
EPS = 1e-6

kernel_name = "hybrid_s5_axialgqa_hmoe_encoder"


def rms_norm(x, g):
    xf = x.astype(jnp.float32)
    y = xf * lax.rsqrt(jnp.mean(xf * xf, axis=-1, keepdims=True) + EPS)
    return (y * g.astype(jnp.float32)).astype(x.dtype)


def axial_rope_tables(seq_len):
    rows = seq_len // GRID_W
    row = jnp.repeat(jnp.arange(rows, dtype=jnp.float32), GRID_W)
    col = jnp.tile(jnp.arange(GRID_W, dtype=jnp.float32), rows)
    half = HEAD_DIM // 2
    inv_freq = 1.0 / (ROPE_THETA ** (jnp.arange(0, half, 2, dtype=jnp.float32) / half))
    ang_r = row[:, None] * inv_freq[None, :]
    ang_c = col[:, None] * inv_freq[None, :]
    ang = jnp.concatenate([ang_r, ang_r, ang_c, ang_c], axis=-1)
    return jnp.cos(ang), jnp.sin(ang)


def apply_axial_rope(x, cos, sin):
    x1, x2, x3, x4 = jnp.split(x, 4, axis=-1)
    rot = jnp.concatenate([-x2, x1, -x4, x3], axis=-1)
    return x * cos[None, :, None, :] + rot * sin[None, :, None, :]


def _ssm_combine(left, right):
    a1, b1 = left
    a2, b2 = right
    return a1 * a2, a2 * b1 + b2


def ssm_branch(u, a_re, a_im, log_dt, b_re, b_im, c_re, c_im, d_skip, w_glu, b_glu):
    bsz, seq_len, _ = u.shape
    f32 = jnp.float32
    uf = u.astype(f32).reshape(bsz, seq_len, N_SSM_GROUPS, SSM_GROUP)
    y = d_skip.astype(f32).reshape(N_SSM_GROUPS, SSM_GROUP) * uf
    for dr in range(N_DIRS):
        lam = lax.complex(a_re[dr].astype(f32), a_im[dr].astype(f32))
        dt = jnp.exp(log_dt[dr].astype(f32))[:, None]
        lam_bar = jnp.exp(lam * dt)
        coef = (lam_bar - 1.0) / lam
        bu = lax.complex(jnp.einsum('blgh,gph->blgp', uf, b_re[dr].astype(f32)),
                         jnp.einsum('blgh,gph->blgp', uf, b_im[dr].astype(f32)))
        bu = coef * bu
        a = jnp.broadcast_to(lam_bar, bu.shape)
        _, st = lax.associative_scan(_ssm_combine, (a, bu), axis=1, reverse=(dr == 1))
        y = y + (jnp.einsum('blgp,ghp->blgh', st.real, c_re[dr].astype(f32))
                 - jnp.einsum('blgp,ghp->blgh', st.imag, c_im[dr].astype(f32)))
    y = jax.nn.gelu(y.reshape(bsz, seq_len, D_SSM))
    y = y * jax.nn.sigmoid(y @ w_glu.astype(f32) + b_glu.astype(f32))
    return y.astype(u.dtype)


def _head_rms(x, g):
    return x * lax.rsqrt(jnp.mean(x * x, axis=-1, keepdims=True) + EPS) * g.astype(jnp.float32)


def attention_branch(q, k, v, g_q, g_k, cos, sin):
    bsz, seq_len, _ = q.shape
    f32 = jnp.float32
    qh = _head_rms(q.astype(f32).reshape(bsz, seq_len, N_HEADS, HEAD_DIM), g_q)
    kh = _head_rms(k.astype(f32).reshape(bsz, seq_len, N_KV_HEADS, HEAD_DIM), g_k)
    vh = v.astype(f32).reshape(bsz, seq_len, N_KV_HEADS, HEAD_DIM)
    qh = apply_axial_rope(qh, cos, sin) * (HEAD_DIM ** -0.5)
    kh = apply_axial_rope(kh, cos, sin)
    rep = N_HEADS // N_KV_HEADS
    nblk = seq_len // Q_BLOCK
    qb = qh.reshape(bsz, nblk, Q_BLOCK, N_KV_HEADS, rep, HEAD_DIM).transpose(1, 0, 2, 3, 4, 5)

    def block(qblk):
        s = jnp.einsum('bqhrd,bkhd->bhrqk', qblk, kh)
        p = jax.nn.softmax(s, axis=-1)
        return jnp.einsum('bhrqk,bkhd->bqhrd', p, vh)

    o = lax.map(block, qb)
    o = o.transpose(1, 0, 2, 3, 4, 5).reshape(bsz, seq_len, D_ATTN)
    return o.astype(q.dtype)


def hierarchical_moe(h, w_coarse, b_coarse, w_fine, b_fine, w_gate, w_up, w_down):
    bsz, seq_len, d = h.shape
    f32 = jnp.float32
    t = h.reshape(-1, d)
    tf = t.astype(f32)
    p_coarse = jax.nn.softmax(tf @ w_coarse.astype(f32) + b_coarse.astype(f32), axis=-1)
    pc_top, g_idx = lax.top_k(p_coarse, 1)
    fine = (tf @ w_fine.astype(f32) + b_fine.astype(f32)).reshape(-1, N_EXPERT_GROUPS, EXPERTS_PER_GROUP)
    fine_sel = jnp.take_along_axis(fine, g_idx[:, :, None], axis=1)[:, 0]
    f_val, f_idx = lax.top_k(fine_sel, FINE_TOP_K)
    weights = pc_top * jax.nn.softmax(f_val, axis=-1)
    eid = g_idx * EXPERTS_PER_GROUP + f_idx
    gate = jnp.sum(jax.nn.one_hot(eid, N_EXPERTS, dtype=f32) * weights[..., None], axis=1)
    nblk = t.shape[0] // TOKEN_BLOCK

    def block(args):
        xb, gb = args
        hg = jnp.einsum('td,edf->tef', xb, w_gate)
        hu = jnp.einsum('td,edf->tef', xb, w_up)
        act = jax.nn.silu(hg) * hu * gb[..., None].astype(xb.dtype)
        return jnp.einsum('tef,efd->td', act, w_down)

    out = lax.map(block, (t.reshape(nblk, TOKEN_BLOCK, d), gate.reshape(nblk, TOKEN_BLOCK, N_EXPERTS)))
    return out.reshape(bsz, seq_len, d).astype(h.dtype)


def _trunk(x, c, w_ada, b_ada, g_norm1, g_norm2, w_in, ssm_a_re, ssm_a_im, ssm_log_dt,
           ssm_b_re, ssm_b_im, ssm_c_re, ssm_c_im, ssm_d, w_glu, b_glu, g_q, g_k,
           w_branch_ssm, w_branch_attn, w_out, w_coarse, b_coarse, w_fine, b_fine,
           w_expert_gate, w_expert_up, w_expert_down):
    seq_len = x.shape[1]
    cos, sin = axial_rope_tables(seq_len)
    splits = [D_SSM, D_SSM + D_ATTN, D_SSM + D_ATTN + D_KV, D_SSM + D_ATTN + 2 * D_KV,
              D_SSM + D_ATTN + 2 * D_KV + D_MODEL]
    c_act = jax.nn.silu(c)
    for i in range(DEPTH):
        mod = c_act @ w_ada[i] + b_ada[i]
        shift1, scale1, gate1, shift2, scale2, gate2 = [m[:, None, :] for m in jnp.split(mod, N_MOD, axis=-1)]
        h = rms_norm(x, g_norm1[i]) * (1.0 + scale1) + shift1
        z = h @ w_in[i]
        u, q, k, v, ga, gb = jnp.split(z, splits, axis=-1)
        y_s = ssm_branch(u, ssm_a_re[i], ssm_a_im[i], ssm_log_dt[i], ssm_b_re[i], ssm_b_im[i],
                         ssm_c_re[i], ssm_c_im[i], ssm_d[i], w_glu[i], b_glu[i])
        y_a = attention_branch(q, k, v, g_q[i], g_k[i], cos, sin)
        merged = (jax.nn.sigmoid(ga) * (y_s @ w_branch_ssm[i])
                  + jax.nn.sigmoid(gb) * (y_a @ w_branch_attn[i]))
        x = x + gate1 * (merged @ w_out[i])
        h2 = rms_norm(x, g_norm2[i]) * (1.0 + scale2) + shift2
        x = x + gate2 * hierarchical_moe(h2, w_coarse[i], b_coarse[i], w_fine[i], b_fine[i],
                                         w_expert_gate[i], w_expert_up[i], w_expert_down[i])
    return x


def setup_inputs(seed: int = 0) -> dict:
    key = jax.random.key(seed)
    ks = jax.random.split(key, 32)
    f32 = jnp.float32

    def nrm(k, shape, scale):
        return jax.random.normal(k, shape, f32) * scale

    n_idx = jnp.arange(SSM_STATE, dtype=f32)
    a_im_base = jnp.broadcast_to(math.pi * n_idx, (DEPTH, N_DIRS, N_SSM_GROUPS, SSM_STATE))
    return {
        "x_prompt": nrm(ks[0], (BATCH, SEQ, D_MODEL), 1.0),
        "x_sample": nrm(ks[1], (DEC_BATCH, DEC_SEQ, D_MODEL), 1.0),
        "c_prompt": nrm(ks[2], (BATCH, D_MODEL), 1.0),
        "c_sample": nrm(ks[3], (DEC_BATCH, D_MODEL), 1.0),
        "w_ada": nrm(ks[4], (DEPTH, D_MODEL, N_MOD * D_MODEL), 0.5 * D_MODEL ** -0.5),
        "b_ada": nrm(ks[5], (DEPTH, N_MOD * D_MODEL), 0.02),
        "g_norm1": 1.0 + nrm(ks[6], (DEPTH, D_MODEL), 0.02),
        "g_norm2": 1.0 + nrm(ks[7], (DEPTH, D_MODEL), 0.02),
        "w_in": nrm(ks[8], (DEPTH, D_MODEL, D_IN), D_MODEL ** -0.5),
        "ssm_a_re": -0.5 + nrm(ks[9], (DEPTH, N_DIRS, N_SSM_GROUPS, SSM_STATE), 0.01),
        "ssm_a_im": a_im_base + nrm(ks[10], (DEPTH, N_DIRS, N_SSM_GROUPS, SSM_STATE), 0.01),
        "ssm_log_dt": jax.random.uniform(ks[11], (DEPTH, N_DIRS, N_SSM_GROUPS), f32,
                                         math.log(1e-3), math.log(1e-1)),
        "ssm_b_re": nrm(ks[12], (DEPTH, N_DIRS, N_SSM_GROUPS, SSM_STATE, SSM_GROUP), (2 * SSM_GROUP) ** -0.5),
        "ssm_b_im": nrm(ks[13], (DEPTH, N_DIRS, N_SSM_GROUPS, SSM_STATE, SSM_GROUP), (2 * SSM_GROUP) ** -0.5),
        "ssm_c_re": nrm(ks[14], (DEPTH, N_DIRS, N_SSM_GROUPS, SSM_GROUP, SSM_STATE), (2 * SSM_STATE) ** -0.5),
        "ssm_c_im": nrm(ks[15], (DEPTH, N_DIRS, N_SSM_GROUPS, SSM_GROUP, SSM_STATE), (2 * SSM_STATE) ** -0.5),
        "ssm_d": nrm(ks[16], (DEPTH, D_SSM), 1.0),
        "w_glu": nrm(ks[17], (DEPTH, D_SSM, D_SSM), D_SSM ** -0.5),
        "b_glu": nrm(ks[18], (DEPTH, D_SSM), 0.02),
        "g_q": 1.0 + nrm(ks[19], (DEPTH, HEAD_DIM), 0.02),
        "g_k": 1.0 + nrm(ks[20], (DEPTH, HEAD_DIM), 0.02),
        "w_branch_ssm": nrm(ks[21], (DEPTH, D_SSM, D_MODEL), D_SSM ** -0.5),
        "w_branch_attn": nrm(ks[22], (DEPTH, D_ATTN, D_MODEL), D_ATTN ** -0.5),
        "w_out": nrm(ks[23], (DEPTH, D_MODEL, D_MODEL), D_MODEL ** -0.5),
        "w_coarse": nrm(ks[24], (DEPTH, D_MODEL, N_EXPERT_GROUPS), D_MODEL ** -0.5),
        "b_coarse": nrm(ks[25], (DEPTH, N_EXPERT_GROUPS), 0.01),
        "w_fine": nrm(ks[26], (DEPTH, D_MODEL, N_EXPERTS), D_MODEL ** -0.5),
        "b_fine": nrm(ks[27], (DEPTH, N_EXPERTS), 0.01),
        "w_expert_gate": nrm(ks[28], (DEPTH, N_EXPERTS, D_MODEL, D_EXPERT), D_MODEL ** -0.5),
        "w_expert_up": nrm(ks[29], (DEPTH, N_EXPERTS, D_MODEL, D_EXPERT), D_MODEL ** -0.5),
        "w_expert_down": nrm(ks[30], (DEPTH, N_EXPERTS, D_EXPERT, D_MODEL), D_EXPERT ** -0.5),
    }


def reference(x_prompt, x_sample, c_prompt, c_sample, w_ada, b_ada, g_norm1, g_norm2, w_in,
              ssm_a_re, ssm_a_im, ssm_log_dt, ssm_b_re, ssm_b_im, ssm_c_re, ssm_c_im, ssm_d,
              w_glu, b_glu, g_q, g_k, w_branch_ssm, w_branch_attn, w_out, w_coarse, b_coarse,
              w_fine, b_fine, w_expert_gate, w_expert_up, w_expert_down):
    y_prompt = _trunk(x_prompt, c_prompt, w_ada, b_ada, g_norm1, g_norm2, w_in, ssm_a_re, ssm_a_im,
                      ssm_log_dt, ssm_b_re, ssm_b_im, ssm_c_re, ssm_c_im, ssm_d, w_glu, b_glu, g_q, g_k,
                      w_branch_ssm, w_branch_attn, w_out, w_coarse, b_coarse, w_fine, b_fine,
                      w_expert_gate, w_expert_up, w_expert_down)
    y_sample = _trunk(x_sample, c_sample, w_ada, b_ada, g_norm1, g_norm2, w_in, ssm_a_re, ssm_a_im,
                      ssm_log_dt, ssm_b_re, ssm_b_im, ssm_c_re, ssm_c_im, ssm_d, w_glu, b_glu, g_q, g_k,
                      w_branch_ssm, w_branch_attn, w_out, w_coarse, b_coarse, w_fine, b_fine,
                      w_expert_gate, w_expert_up, w_expert_down)
    return (y_prompt, y_sample)
```

```python
import functools
import math

import jax
import jax.numpy as jnp
import numpy as np
from jax import lax
from jax.experimental import pallas as pl
from jax.experimental.pallas import tpu as pltpu

F32 = jnp.float32
BF16 = jnp.bfloat16

EPS = 1e-6
GRID_W = 64
ROPE_THETA = 10000.0
HEAD_DIM = 64
N_KV_HEADS = 2
FINE_TOP_K = 2
LOG2E = 1.4426950408889634
LANES = 128
SUBLANES = 8
SSM_CHUNK = 16

TOKEN_BLOCK = 512
Q_TILE = 256
KV_TILE = 512
EXPERT_TILE = 256
VMEM_LIMIT = 56 * 1024 * 1024


def _sigmoid(x):
    return 1.0 / (1.0 + jnp.exp(-x))


def _cparams(sem):
    return pltpu.CompilerParams(dimension_semantics=sem, vmem_limit_bytes=VMEM_LIMIT)


def _ada_kernel(c_ref, w_ref, b_ref, o_ref):
    c = c_ref[...]
    ca = c * _sigmoid(c)
    o_ref[0, 0] = jnp.dot(ca, w_ref[0], preferred_element_type=F32,
                          precision=lax.Precision.HIGHEST) + b_ref[0, 0]


def _ada_modulation(c, w_ada, b_ada):
    depth, d, _ = w_ada.shape
    nb = c.shape[0]
    n_mod = w_ada.shape[2] // d
    out = pl.pallas_call(
        _ada_kernel,
        grid=(depth, n_mod),
        in_specs=[
            pl.BlockSpec((nb, d), lambda l, j: (0, 0)),
            pl.BlockSpec((1, d, d), lambda l, j: (l, 0, j)),
            pl.BlockSpec((1, 1, 1, d), lambda l, j: (l, j, 0, 0)),
        ],
        out_specs=pl.BlockSpec((1, 1, nb, d), lambda l, j: (l, j, 0, 0)),
        out_shape=jax.ShapeDtypeStruct((depth, n_mod, nb, d), F32),
        compiler_params=_cparams(("arbitrary", "arbitrary")),
        name="ada_modulation",
    )(c, w_ada, b_ada.reshape(depth, n_mod, 1, d))
    return out.transpose(0, 2, 1, 3)


def _rope(x, cos, sin_signed, low_half):
    up = pltpu.roll(x, LANES - 16, axis=1)
    down = pltpu.roll(x, 16, axis=1)
    return x * cos + jnp.where(low_half, up, down) * sin_signed


def _mix_in_kernel(seq_ref, pos_ref, x_ref, mod_ref, g1_ref, w_ref, cos_ref, sin_ref,
                   gq_ref, gk_ref, hm_ref, u_ref, q_ref, k_ref, v_ref, gate_ref,
                   *, d_ssm, d_attn, d_kv):
    del seq_ref, pos_ref
    x = x_ref[...]
    mod = mod_ref[0]
    xn = x * lax.rsqrt(jnp.mean(x * x, axis=-1, keepdims=True) + EPS) * g1_ref[...]
    h = (xn * (1.0 + mod[1:2]) + mod[0:1]).astype(BF16)

    def proj(lo, hi):
        return jnp.dot(h, w_ref[:, lo:hi], preferred_element_type=F32)

    u_ref[...] = proj(0, d_ssm).astype(BF16)

    cos = cos_ref[...]
    sin_signed = sin_ref[...]
    lane = lax.broadcasted_iota(jnp.int32, cos.shape, 1)
    low_half = (lane % 32) < 16
    hm = hm_ref[...]

    def norm_rope(z, gain, scale):
        cols = []
        for j in range(z.shape[1] // LANES):
            zj = z[:, j * LANES:(j + 1) * LANES]
            msq = jnp.dot((zj * zj).astype(BF16), hm, preferred_element_type=F32)
            zn = zj * lax.rsqrt(msq + EPS) * gain
            cols.append(_rope(zn, cos, sin_signed, low_half) * scale)
        return jnp.concatenate(cols, axis=1) if len(cols) > 1 else cols[0]

    o = d_ssm
    q_ref[...] = norm_rope(proj(o, o + d_attn), gq_ref[...],
                           (HEAD_DIM ** -0.5) * LOG2E).astype(BF16)
    o += d_attn
    k_ref[...] = norm_rope(proj(o, o + d_kv), gk_ref[...], 1.0).astype(BF16)
    o += d_kv
    v_ref[...] = proj(o, o + d_kv).astype(BF16)
    o += d_kv
    n_gate = gate_ref.shape[1]
    half = n_gate // 2
    gate_ref[:, :half] = _sigmoid(proj(o, o + half)).astype(BF16)
    gate_ref[:, half:] = _sigmoid(proj(o + half, o + n_gate)).astype(BF16)


def _mix_in(x, mod, g1, w_in, cos_t, sin_t, gq_t, gk_t, hm, blk_seq, blk_pos, dims):
    t, d = x.shape
    d_ssm, d_attn, d_kv = dims
    n_gate = w_in.shape[1] - d_ssm - d_attn - 2 * d_kv
    tb = TOKEN_BLOCK
    row = lambda i, s, p: (i, 0)
    const = lambda i, s, p: (0, 0)
    grid_spec = pltpu.PrefetchScalarGridSpec(
        num_scalar_prefetch=2,
        grid=(t // tb,),
        in_specs=[
            pl.BlockSpec((tb, d), row),
            pl.BlockSpec((1,) + mod.shape[1:], lambda i, s, p: (s[i], 0, 0)),
            pl.BlockSpec((1, d), const),
            pl.BlockSpec(w_in.shape, const),
            pl.BlockSpec((tb, LANES), lambda i, s, p: (p[i], 0)),
            pl.BlockSpec((tb, LANES), lambda i, s, p: (p[i], 0)),
            pl.BlockSpec((1, LANES), const),
            pl.BlockSpec((1, LANES), const),
            pl.BlockSpec((LANES, LANES), const),
        ],
        out_specs=[
            pl.BlockSpec((tb, d_ssm), row),
            pl.BlockSpec((tb, d_attn), row),
            pl.BlockSpec((tb, d_kv), row),
            pl.BlockSpec((tb, d_kv), row),
            pl.BlockSpec((tb, n_gate), row),
        ],
    )
    return pl.pallas_call(
        functools.partial(_mix_in_kernel, d_ssm=d_ssm, d_attn=d_attn, d_kv=d_kv),
        grid_spec=grid_spec,
        out_shape=[
            jax.ShapeDtypeStruct((t, d_ssm), BF16),
            jax.ShapeDtypeStruct((t, d_attn), BF16),
            jax.ShapeDtypeStruct((t, d_kv), BF16),
            jax.ShapeDtypeStruct((t, d_kv), BF16),
            jax.ShapeDtypeStruct((t, n_gate), BF16),
        ],
        compiler_params=_cparams(("arbitrary",)),
        name="mix_in",
    )(blk_seq, blk_pos, x, mod, g1, w_in, cos_t, sin_t, gq_t, gk_t, hm)


def _ssm_weights(a_re, a_im, log_dt, b_re, b_im, c_re, c_im, d_skip):
    n_dirs, g, p = a_re.shape
    hch = b_re.shape[-1]
    c = SSM_CHUNK
    lam = lax.complex(a_re.astype(F32), a_im.astype(F32))
    dt = jnp.exp(log_dt.astype(F32))[..., None]
    lam_dt = lam * dt
    coef = (jnp.exp(lam_dt) - 1.0) / lam
    steps = jnp.arange(c + 1, dtype=F32)
    pw = jnp.exp(lam_dt[:, None] * steps[None, :, None, None])
    bc = lax.complex(b_re.astype(F32), b_im.astype(F32)) * coef[..., None]
    cc = lax.complex(c_re.astype(F32), c_im.astype(F32))

    kern = jnp.einsum('rghp,rdgp,rgpk->rdghk', cc, pw[:, :c], bc).real
    i_idx = jnp.arange(c)[:, None]
    j_idx = jnp.arange(c)[None, :]
    lag = j_idx - i_idx
    kf = jnp.where((lag >= 0)[..., None, None, None],
                   kern[0][jnp.clip(lag, 0, c - 1)], 0.0)
    kb = jnp.where((lag <= 0)[..., None, None, None],
                   kern[1][jnp.clip(-lag, 0, c - 1)], 0.0)
    eye_t = (lag == 0).astype(F32)[..., None, None, None]
    dmat = jnp.eye(hch, dtype=F32)[None] * d_skip.astype(F32).reshape(g, hch)[:, :, None]
    w_intra = kf + kb + eye_t * dmat[None, None]
    w_intra = w_intra.transpose(2, 0, 4, 1, 3).reshape(g, c * hch, c * hch)

    win_f = pw[0, c - 1 - jnp.arange(c)][:, :, None, :] * bc[0].transpose(0, 2, 1)[None]
    win_b = pw[1, jnp.arange(c)][:, :, None, :] * bc[1].transpose(0, 2, 1)[None]
    def rows(w):
        return w.transpose(1, 0, 2, 3).reshape(g, c * hch, p)
    win = [rows(win_f.real), rows(win_f.imag), rows(win_b.real), rows(win_b.imag)]

    cout_f = cc[0][None] * pw[0, 1 + jnp.arange(c)][:, :, None, :]
    cout_b = cc[1][None] * pw[1, c - jnp.arange(c)][:, :, None, :]
    def cols(w):
        return w.transpose(1, 3, 0, 2).reshape(g, p, c * hch)
    wout = [cols(cout_f.real), cols(-cout_f.imag), cols(cout_b.real), cols(-cout_b.imag)]

    gp = g // 2
    ch = c * hch
    z_in = jnp.zeros((gp, ch, p), F32)
    w_in_pair = jnp.concatenate([
        jnp.concatenate([jnp.concatenate([w[0::2], z_in], -1) for w in win], -1),
        jnp.concatenate([jnp.concatenate([z_in, w[1::2]], -1) for w in win], -1)], 1)
    z_out = jnp.zeros((gp, p, ch), F32)
    w_out_pair = jnp.concatenate([
        jnp.concatenate([jnp.concatenate([w[0::2], z_out], -1),
                         jnp.concatenate([z_out, w[1::2]], -1)], 1) for w in wout], 1)
    z_sq = jnp.zeros((gp, ch, ch), F32)
    w_intra_pair = jnp.concatenate([jnp.concatenate([w_intra[0::2], z_sq], -1),
                                    jnp.concatenate([z_sq, w_intra[1::2]], -1)], 1)
    a_chunk = pw[:, c]
    planes = [a_chunk[0].real, a_chunk[0].imag, a_chunk[1].real, a_chunk[1].imag]
    a_pair = jnp.stack([jnp.concatenate([pl_[0::2], pl_[1::2]], -1) for pl_ in planes], 1)
    a_pair = jnp.concatenate([a_pair, jnp.zeros_like(a_pair)], 1)
    return w_in_pair.astype(BF16), w_intra_pair.astype(BF16), w_out_pair.astype(BF16), a_pair


def _ssm_kernel(u_ref, win_ref, wintra_ref, wout_ref, a_ref, y_ref, x_sc, s_sc, *, seq_groups):
    u = u_ref[0]
    x_sc[...] = jnp.dot(u, win_ref[0], preferred_element_type=F32)
    a = a_ref[0]
    w = LANES
    sub = SUBLANES
    tile = (sub, w)
    a_fr, a_fi, a_br, a_bi = (jnp.broadcast_to(a[r:r + 1], tile) for r in range(4))
    row_id = lax.broadcasted_iota(jnp.int32, tile, 0)
    for n_chunks, starts in seq_groups:
        def body(cb, carry, starts=starts, n_chunks=n_chunks):
            out = []
            for start, (sfr, sfi, sbr, sbi) in zip(starts, carry):
                rowf = pl.multiple_of(start + cb * sub, sub)
                rowb = pl.multiple_of(start + n_chunks - sub - cb * sub, sub)
                xfr = x_sc[pl.ds(rowf, sub), 0:w]
                xfi = x_sc[pl.ds(rowf, sub), w:2 * w]
                xbr = x_sc[pl.ds(rowb, sub), 2 * w:3 * w]
                xbi = x_sc[pl.ds(rowb, sub), 3 * w:4 * w]
                tfr = tfi = tbr = tbi = jnp.zeros(tile, F32)
                for r in range(sub):
                    rb = sub - 1 - r
                    tfr = jnp.where(row_id == r, sfr, tfr)
                    tfi = jnp.where(row_id == r, sfi, tfi)
                    tbr = jnp.where(row_id == rb, sbr, tbr)
                    tbi = jnp.where(row_id == rb, sbi, tbi)
                    ufr = jnp.broadcast_to(xfr[r:r + 1], tile)
                    ufi = jnp.broadcast_to(xfi[r:r + 1], tile)
                    ubr = jnp.broadcast_to(xbr[rb:rb + 1], tile)
                    ubi = jnp.broadcast_to(xbi[rb:rb + 1], tile)
                    sfr, sfi = a_fr * sfr - a_fi * sfi + ufr, a_fr * sfi + a_fi * sfr + ufi
                    sbr, sbi = a_br * sbr - a_bi * sbi + ubr, a_br * sbi + a_bi * sbr + ubi
                s_sc[pl.ds(rowf, sub), 0:w] = tfr
                s_sc[pl.ds(rowf, sub), w:2 * w] = tfi
                s_sc[pl.ds(rowb, sub), 2 * w:3 * w] = tbr
                s_sc[pl.ds(rowb, sub), 3 * w:4 * w] = tbi
                out.append((sfr, sfi, sbr, sbi))
            return tuple(out)
        zero = jnp.zeros(tile, F32)
        lax.fori_loop(0, n_chunks // sub, body, tuple((zero,) * 4 for _ in starts))
    y = jnp.dot(u, wintra_ref[0], preferred_element_type=F32)
    y += jnp.dot(s_sc[...].astype(BF16), wout_ref[0], preferred_element_type=F32)
    y_ref[0] = y.astype(y_ref.dtype)


def _ssm_branch(u_pairs, w_in, w_intra, w_out, a_pair, seq_groups):
    gp, nc, width = u_pairs.shape
    blk = lambda i: (i, 0, 0)
    return pl.pallas_call(
        functools.partial(_ssm_kernel, seq_groups=seq_groups),
        grid=(gp,),
        in_specs=[
            pl.BlockSpec((1, nc, width), blk),
            pl.BlockSpec((1,) + w_in.shape[1:], blk),
            pl.BlockSpec((1,) + w_intra.shape[1:], blk),
            pl.BlockSpec((1,) + w_out.shape[1:], blk),
            pl.BlockSpec((1,) + a_pair.shape[1:], blk),
        ],
        out_specs=pl.BlockSpec((1, nc, width), blk),
        out_shape=jax.ShapeDtypeStruct((gp, nc, width), BF16),
        scratch_shapes=[pltpu.VMEM((nc, w_in.shape[2]), F32), pltpu.VMEM((nc, w_out.shape[1]), F32)],
        compiler_params=_cparams(("arbitrary",)),
        name="ssm_branch",
    )(u_pairs, w_in, w_intra, w_out, a_pair)


def _attn_kernel(q_ref, k_ref, vt_ref, o_ref, m_sc, l_sc, acc_sc, *, n_kv_tiles):
    tq = q_ref.shape[1]
    n_slices = q_ref.shape[2] // LANES
    m_sc[...] = jnp.full(m_sc.shape, -jnp.inf, F32)
    l_sc[...] = jnp.zeros(l_sc.shape, F32)
    acc_sc[...] = jnp.zeros(acc_sc.shape, F32)

    def kv_step(kb, carry):
        start = pl.multiple_of(kb * KV_TILE, KV_TILE)
        k = k_ref[0, pl.ds(start, KV_TILE), :]
        lane = lax.broadcasted_iota(jnp.int32, k.shape, 1)
        k_sel = (jnp.where(lane < HEAD_DIM, k, jnp.zeros_like(k)),
                 jnp.where(lane >= HEAD_DIM, k, jnp.zeros_like(k)))
        for j in range(n_slices):
            qj = q_ref[0, :, j * LANES:(j + 1) * LANES]
            for g in range(N_KV_HEADS):
                h = j * N_KV_HEADS + g
                s_t = lax.dot_general(k_sel[g], qj, (((1,), (1,)), ((), ())),
                                      preferred_element_type=F32)
                m_old = m_sc[h:h + 1, :]
                m_new = jnp.maximum(m_old, jnp.max(s_t, axis=0, keepdims=True))
                p = jnp.exp2(s_t - m_new)
                alpha = jnp.exp2(m_old - m_new)
                l_sc[h:h + 1, :] = alpha * l_sc[h:h + 1, :] + jnp.sum(p, axis=0, keepdims=True)
                vt = vt_ref[0, kb, g * HEAD_DIM:(g + 1) * HEAD_DIM, :]
                pv = jnp.dot(vt, p.astype(BF16), preferred_element_type=F32)
                rows = slice(h * HEAD_DIM, (h + 1) * HEAD_DIM)
                acc_sc[rows, :] = alpha * acc_sc[rows, :] + pv
                m_sc[h:h + 1, :] = m_new
        return carry

    lax.fori_loop(0, n_kv_tiles, kv_step, 0)
    for h in range(n_slices * N_KV_HEADS):
        rows = slice(h * HEAD_DIM, (h + 1) * HEAD_DIM)
        o_ref[0, rows, :] = (acc_sc[rows, :] / l_sc[h:h + 1, :]).astype(o_ref.dtype)
    del tq


def _attention(q, k, vt):
    b, l, d_attn = q.shape
    d_kv = k.shape[2]
    tq = min(Q_TILE, l)
    n_heads = d_attn // HEAD_DIM
    return pl.pallas_call(
        functools.partial(_attn_kernel, n_kv_tiles=l // KV_TILE),
        grid=(b, l // tq),
        in_specs=[
            pl.BlockSpec((1, tq, d_attn), lambda i, j: (i, j, 0)),
            pl.BlockSpec((1, l, d_kv), lambda i, j: (i, 0, 0)),
            pl.BlockSpec((1,) + vt.shape[1:], lambda i, j: (i, 0, 0, 0)),
        ],
        out_specs=pl.BlockSpec((1, d_attn, tq), lambda i, j: (i, 0, j)),
        out_shape=jax.ShapeDtypeStruct((b, d_attn, l), BF16),
        scratch_shapes=[pltpu.VMEM((n_heads, tq), F32), pltpu.VMEM((n_heads, tq), F32),
                        pltpu.VMEM((d_attn, tq), F32)],
        compiler_params=_cparams(("arbitrary", "arbitrary")),
        name="attention",
    )(q, k, vt)


def _mix_out_kernel(seq_ref, x_ref, y_ref, o_ref, gate_ref, mod_ref, wglu_ref, bglu_ref,
                    wbs_ref, wba_ref, wout_ref, g2_ref, wr_ref, br_ref,
                    x1_ref, h2_ref, route_ref, *, n_groups, per_group):
    del seq_ref
    mod = mod_ref[0]
    y = y_ref[...].astype(F32)
    y = 0.5 * y * (1.0 + jnp.tanh(math.sqrt(2.0 / math.pi) * (y + 0.044715 * (y * y * y))))
    glu = jnp.dot(y.astype(BF16), wglu_ref[...], preferred_element_type=F32) + bglu_ref[...]
    ys = (y * _sigmoid(glu)).astype(BF16)
    d = x_ref.shape[1]
    gate = gate_ref[...].astype(F32)
    merged = gate[:, :d] * jnp.dot(ys, wbs_ref[...], preferred_element_type=F32)
    merged += gate[:, d:] * jnp.dot(o_ref[...], wba_ref[...], preferred_element_type=F32)
    x1 = x_ref[...] + mod[2:3] * jnp.dot(merged.astype(BF16), wout_ref[...],
                                         preferred_element_type=F32)
    x1_ref[...] = x1
    xn = x1 * lax.rsqrt(jnp.mean(x1 * x1, axis=-1, keepdims=True) + EPS) * g2_ref[...]
    h2 = (xn * (1.0 + mod[4:5]) + mod[3:4]).astype(BF16)
    h2_ref[...] = h2

    n_exp = n_groups * per_group
    logits = jnp.dot(h2, wr_ref[...], preferred_element_type=F32) + br_ref[...]
    lane = lax.broadcasted_iota(jnp.int32, logits.shape, 1).astype(F32)
    neg = jnp.float32(-jnp.inf)
    big = jnp.float32(1 << 20)
    lc = jnp.where(lane >= n_exp, jnp.where(lane < n_exp + n_groups, logits, neg), neg)
    c_max = jnp.max(lc, axis=-1, keepdims=True)
    g_idx = jnp.min(jnp.where(lc == c_max, lane - n_exp, big), axis=-1, keepdims=True)
    pc_top = 1.0 / jnp.sum(jnp.exp(lc - c_max), axis=-1, keepdims=True)
    lf = jnp.where(lane >= g_idx * per_group,
                   jnp.where(lane < (g_idx + 1.0) * per_group, logits, neg), neg)
    f1 = jnp.max(lf, axis=-1, keepdims=True)
    e1 = jnp.min(jnp.where(lf == f1, lane, big), axis=-1, keepdims=True)
    lf2 = jnp.where(lane == e1, neg, lf)
    f2 = jnp.max(lf2, axis=-1, keepdims=True)
    e2 = jnp.min(jnp.where(lf2 == f2, lane, big), axis=-1, keepdims=True)
    r = jnp.exp(f2 - f1)
    w1 = pc_top / (1.0 + r)
    w2 = pc_top * r / (1.0 + r)
    route = jnp.where(lane == 0, e1,
                      jnp.where(lane == 1, e2,
                                jnp.where(lane == 2, w1, jnp.where(lane == 3, w2, 0.0))))
    route_ref[...] = route


def _mix_out(x, y, o, gates, mod, w_glu, b_glu, w_bs, w_ba, w_out, g2, w_router, b_router,
             blk_seq, n_groups, per_group):
    t, d = x.shape
    tb = TOKEN_BLOCK
    row = lambda i, s: (i, 0)
    const = lambda i, s: (0, 0)
    grid_spec = pltpu.PrefetchScalarGridSpec(
        num_scalar_prefetch=1,
        grid=(t // tb,),
        in_specs=[
            pl.BlockSpec((tb, d), row),
            pl.BlockSpec((tb, y.shape[1]), row),
            pl.BlockSpec((tb, o.shape[1]), row),
            pl.BlockSpec((tb, gates.shape[1]), row),
            pl.BlockSpec((1,) + mod.shape[1:], lambda i, s: (s[i], 0, 0)),
            pl.BlockSpec(w_glu.shape, const),
            pl.BlockSpec(b_glu.shape, const),
            pl.BlockSpec(w_bs.shape, const),
            pl.BlockSpec(w_ba.shape, const),
            pl.BlockSpec(w_out.shape, const),
            pl.BlockSpec(g2.shape, const),
            pl.BlockSpec(w_router.shape, const),
            pl.BlockSpec(b_router.shape, const),
        ],
        out_specs=[
            pl.BlockSpec((tb, d), row),
            pl.BlockSpec((tb, d), row),
            pl.BlockSpec((tb, LANES), row),
        ],
    )
    return pl.pallas_call(
        functools.partial(_mix_out_kernel, n_groups=n_groups, per_group=per_group),
        grid_spec=grid_spec,
        out_shape=[
            jax.ShapeDtypeStruct((t, d), F32),
            jax.ShapeDtypeStruct((t, d), BF16),
            jax.ShapeDtypeStruct((t, LANES), F32),
        ],
        compiler_params=_cparams(("arbitrary",)),
        name="mix_out",
    )(blk_seq, x, y, o, gates, mod, w_glu, b_glu, w_bs, w_ba, w_out, g2, w_router, b_router)


def _expert_kernel(te_ref, nt_ref, xs_ref, wg_ref, wu_ref, wd_ref, ys_ref):
    del te_ref
    live = pl.program_id(0) < nt_ref[0]

    @pl.when(live)
    def _():
        x = xs_ref[...]
        hg = jnp.dot(x, wg_ref[0], preferred_element_type=F32)
        hu = jnp.dot(x, wu_ref[0], preferred_element_type=F32)
        act = (hg * _sigmoid(hg) * hu).astype(BF16)
        ys_ref[...] = jnp.dot(act, wd_ref[0], preferred_element_type=F32).astype(ys_ref.dtype)

    @pl.when(jnp.logical_not(live))
    def _():
        ys_ref[...] = jnp.zeros(ys_ref.shape, ys_ref.dtype)


def _experts(xs, tile_expert, n_live, w_gate, w_up, w_down):
    p, d = xs.shape
    tm = EXPERT_TILE
    f = w_gate.shape[2]
    grid_spec = pltpu.PrefetchScalarGridSpec(
        num_scalar_prefetch=2,
        grid=(p // tm,),
        in_specs=[
            pl.BlockSpec((tm, d), lambda i, te, nt: (i, 0)),
            pl.BlockSpec((1, d, f), lambda i, te, nt: (te[i], 0, 0)),
            pl.BlockSpec((1, d, f), lambda i, te, nt: (te[i], 0, 0)),
            pl.BlockSpec((1, f, d), lambda i, te, nt: (te[i], 0, 0)),
        ],
        out_specs=pl.BlockSpec((tm, d), lambda i, te, nt: (i, 0)),
    )
    return pl.pallas_call(
        _expert_kernel,
        grid_spec=grid_spec,
        out_shape=jax.ShapeDtypeStruct((p, d), BF16),
        compiler_params=_cparams(("arbitrary",)),
        name="experts",
    )(tile_expert, n_live, xs, w_gate, w_up, w_down)


def _combine_kernel(seq_ref, x_ref, y1_ref, y2_ref, route_ref, mod_ref, o_ref):
    del seq_ref
    route = route_ref[...]
    w1 = route[:, 2:3]
    w2 = route[:, 3:4]
    moe = w1 * y1_ref[...].astype(F32) + w2 * y2_ref[...].astype(F32)
    o_ref[...] = x_ref[...] + mod_ref[0][5:6] * moe


def _combine(x1, y1, y2, route, mod, blk_seq):
    t, d = x1.shape
    tb = TOKEN_BLOCK
    row = lambda i, s: (i, 0)
    grid_spec = pltpu.PrefetchScalarGridSpec(
        num_scalar_prefetch=1,
        grid=(t // tb,),
        in_specs=[
            pl.BlockSpec((tb, d), row),
            pl.BlockSpec((tb, d), row),
            pl.BlockSpec((tb, d), row),
            pl.BlockSpec((tb, LANES), row),
            pl.BlockSpec((1,) + mod.shape[1:], lambda i, s: (s[i], 0, 0)),
        ],
        out_specs=pl.BlockSpec((tb, d), row),
    )
    return pl.pallas_call(
        _combine_kernel,
        grid_spec=grid_spec,
        out_shape=jax.ShapeDtypeStruct((t, d), F32),
        compiler_params=_cparams(("arbitrary",)),
        name="moe_combine",
    )(blk_seq, x1, y1, y2, route, mod)


def _dispatch_plan(route, n_experts):
    t = route.shape[0]
    tm = EXPERT_TILE
    eid = route[:, :FINE_TOP_K].astype(jnp.int32).reshape(-1)
    onehot = (eid[:, None] == jnp.arange(n_experts, dtype=jnp.int32)[None, :]).astype(jnp.int32)
    csum = jnp.cumsum(onehot, axis=0)
    rank = jnp.take_along_axis(csum, eid[:, None], axis=1)[:, 0] - 1
    counts = csum[-1]
    padded = ((counts + tm - 1) // tm) * tm
    pad_end = jnp.cumsum(padded)
    pad_off = pad_end - padded
    dest = pad_off[eid] + rank
    n_rows = FINE_TOP_K * t + n_experts * tm
    n_tiles = n_rows // tm
    src_tok = jnp.zeros((n_rows,), jnp.int32).at[dest].set(
        jnp.arange(FINE_TOP_K * t, dtype=jnp.int32) // FINE_TOP_K)
    tile_start = jnp.arange(n_tiles, dtype=jnp.int32) * tm
    tile_expert = jnp.minimum(jnp.searchsorted(pad_end, tile_start, side='right'),
                              n_experts - 1).astype(jnp.int32)
    n_live = (pad_end[-1] // tm).astype(jnp.int32).reshape(1)
    last_live = jnp.maximum(n_live[0] - 1, 0)
    tile_expert = jnp.where(tile_start // tm < n_live[0], tile_expert, tile_expert[last_live])
    return dest.reshape(t, FINE_TOP_K), src_tok, tile_expert, n_live


def _rope_tables(max_len):
    rows = max_len // GRID_W
    row = jnp.repeat(jnp.arange(rows, dtype=F32), GRID_W)
    col = jnp.tile(jnp.arange(GRID_W, dtype=F32), rows)
    half = HEAD_DIM // 2
    inv_freq = 1.0 / (ROPE_THETA ** (jnp.arange(0, half, 2, dtype=F32) / half))
    ang_r = row[:, None] * inv_freq[None, :]
    ang_c = col[:, None] * inv_freq[None, :]
    ang = jnp.concatenate([ang_r, ang_r, ang_c, ang_c], axis=-1)
    sign = jnp.where((jnp.arange(HEAD_DIM) % (HEAD_DIM // 2)) < HEAD_DIM // 4, -1.0, 1.0)
    reps = LANES // HEAD_DIM
    return jnp.tile(jnp.cos(ang), (1, reps)), jnp.tile(jnp.sin(ang) * sign[None, :], (1, reps))


def kernel(x_prompt, x_sample, c_prompt, c_sample, w_ada, b_ada, g_norm1, g_norm2, w_in, ssm_a_re, ssm_a_im, ssm_log_dt, ssm_b_re, ssm_b_im, ssm_c_re, ssm_c_im, ssm_d, w_glu, b_glu, g_q, g_k, w_branch_ssm, w_branch_attn, w_out, w_coarse, b_coarse, w_fine, b_fine, w_expert_gate, w_expert_up, w_expert_down):
    b1, l1, d = x_prompt.shape
    b2, l2, _ = x_sample.shape
    depth = w_in.shape[0]
    d_ssm = w_glu.shape[-1]
    d_attn = w_branch_attn.shape[1]
    d_kv = N_KV_HEADS * HEAD_DIM
    n_heads = d_attn // HEAD_DIM
    n_groups = w_coarse.shape[-1]
    n_experts = w_fine.shape[-1]
    per_group = n_experts // n_groups
    n_ssm_groups = ssm_b_re.shape[2]
    ssm_h = ssm_b_re.shape[-1]
    tb = TOKEN_BLOCK
    assert l1 % tb == 0 and l2 % tb == 0 and l1 % KV_TILE == 0 and l2 % KV_TILE == 0
    assert l1 % (SSM_CHUNK * SUBLANES) == 0 and l2 % (SSM_CHUNK * SUBLANES) == 0
    assert d_attn == N_KV_HEADS * (n_heads // N_KV_HEADS) * HEAD_DIM and 2 * HEAD_DIM == LANES
    assert n_ssm_groups % 2 == 0 and ssm_h * SSM_CHUNK * 2 == 4 * LANES and n_experts + n_groups <= LANES

    seq_lens = [l1] * b1 + [l2] * b2
    t = sum(seq_lens)
    x = jnp.concatenate([x_prompt.reshape(b1 * l1, d), x_sample.reshape(b2 * l2, d)], axis=0)
    c = jnp.concatenate([c_prompt, c_sample], axis=0)
    blk_seq = np.concatenate([np.full(n // tb, s, np.int32) for s, n in enumerate(seq_lens)])
    blk_pos = np.concatenate([np.arange(n // tb, dtype=np.int32) for n in seq_lens])
    blk_seq, blk_pos = jnp.asarray(blk_seq), jnp.asarray(blk_pos)
    cos_t, sin_t = _rope_tables(max(l1, l2))

    nc = t // SSM_CHUNK
    starts = np.cumsum([0] + [n // SSM_CHUNK for n in seq_lens])[:-1]
    seq_groups = []
    for n in sorted(set(seq_lens), reverse=True):
        seq_groups.append((n // SSM_CHUNK, tuple(int(s) for s, m in zip(starts, seq_lens) if m == n)))
    seq_groups = tuple(seq_groups)

    rep = n_heads // N_KV_HEADS
    slot_head = np.array([g * rep + j for j in range(rep) for g in range(N_KV_HEADS)])
    col_perm = (slot_head[:, None] * HEAD_DIM + np.arange(HEAD_DIM)[None, :]).reshape(-1)
    in_perm = np.concatenate([np.arange(d_ssm), d_ssm + col_perm, np.arange(d_ssm + d_attn, w_in.shape[-1])])

    mod_all = _ada_modulation(c, w_ada, b_ada)
    gq_t = jnp.tile(g_q.astype(F32), (1, LANES // HEAD_DIM))
    gk_t = jnp.tile(g_k.astype(F32), (1, LANES // HEAD_DIM))
    head_mean = jnp.asarray(np.kron(np.eye(LANES // HEAD_DIM), np.full((HEAD_DIM, HEAD_DIM), 1.0 / HEAD_DIM)), BF16)
    w_router = jnp.concatenate([w_fine, w_coarse,
                                jnp.zeros((depth, d, LANES - n_experts - n_groups), F32)], -1).astype(BF16)
    b_router = jnp.concatenate([b_fine, b_coarse,
                                jnp.zeros((depth, LANES - n_experts - n_groups), F32)], -1)

    for i in range(depth):
        mod = mod_all[i]
        u, q, k, v, gates = _mix_in(
            x, mod, g_norm1[i][None], w_in[i][:, in_perm].astype(BF16), cos_t, sin_t,
            gq_t[i][None], gk_t[i][None], head_mean, blk_seq, blk_pos, (d_ssm, d_attn, d_kv))

        gp = n_ssm_groups // 2
        u_pairs = u.reshape(nc, SSM_CHUNK, gp, 2, ssm_h).transpose(2, 0, 3, 1, 4).reshape(gp, nc, -1)
        ssm_w = _ssm_weights(ssm_a_re[i], ssm_a_im[i], ssm_log_dt[i], ssm_b_re[i], ssm_b_im[i],
                             ssm_c_re[i], ssm_c_im[i], ssm_d[i])
        y_pairs = _ssm_branch(u_pairs, *ssm_w, seq_groups)
        y = y_pairs.reshape(gp, nc, 2, SSM_CHUNK, ssm_h).transpose(1, 3, 0, 2, 4).reshape(t, d_ssm)

        outs = []
        off = 0
        for bsz, ln in ((b1, l1), (b2, l2)):
            n = bsz * ln
            qs = q[off:off + n].reshape(bsz, ln, d_attn)
            ks = k[off:off + n].reshape(bsz, ln, d_kv)
            vts = v[off:off + n].reshape(bsz, ln // KV_TILE, KV_TILE, d_kv).transpose(0, 1, 3, 2)
            o_t = _attention(qs, ks, vts)
            outs.append(o_t.transpose(0, 2, 1).reshape(n, d_attn))
            off += n
        o = jnp.concatenate(outs, axis=0)

        w_ba = w_branch_attn[i][col_perm].astype(BF16)
        x1, h2, route = _mix_out(
            x, y, o, gates, mod, w_glu[i].astype(BF16), b_glu[i][None], w_branch_ssm[i].astype(BF16),
            w_ba, w_out[i].astype(BF16), g_norm2[i][None], w_router[i], b_router[i][None],
            blk_seq, n_groups, per_group)

        dest, src_tok, tile_expert, n_live = _dispatch_plan(route, n_experts)
        xs = jnp.take(h2, src_tok, axis=0)
        ys = _experts(xs, tile_expert, n_live, w_expert_gate[i].astype(BF16),
                      w_expert_up[i].astype(BF16), w_expert_down[i].astype(BF16))
        y1 = jnp.take(ys, dest[:, 0], axis=0)
        y2 = jnp.take(ys, dest[:, 1], axis=0)
        x = _combine(x1, y1, y2, route, mod, blk_seq)

    y_prompt = x[:b1 * l1].reshape(b1, l1, d)
    y_sample = x[b1 * l1:].reshape(b2, l2, d)
    return (y_prompt, y_sample)
```

```python
import functools
import math

import jax
import jax.numpy as jnp
import numpy as np
from jax import lax
from jax.experimental import pallas as pl
from jax.experimental.pallas import tpu as pltpu

F32 = jnp.float32
BF16 = jnp.bfloat16

EPS = 1e-6
GRID_W = 64
ROPE_THETA = 10000.0
HEAD_DIM = 64
N_KV_HEADS = 2
FINE_TOP_K = 2
LOG2E = 1.4426950408889634
LANES = 128
SUBLANES = 8
SSM_CHUNK = 16

TOKEN_BLOCK = 512
Q_TILE = 256
KV_TILE = 512
EXPERT_TILE = 256
VMEM_LIMIT = 56 * 1024 * 1024


def _sigmoid(x):
    return 1.0 / (1.0 + jnp.exp(-x))


def _cparams(sem):
    return pltpu.CompilerParams(dimension_semantics=sem, vmem_limit_bytes=VMEM_LIMIT)


def _ada_kernel(c_ref, w_ref, b_ref, o_ref):
    c = c_ref[...]
    ca = c * _sigmoid(c)
    o_ref[0, 0] = jnp.dot(ca, w_ref[0], preferred_element_type=F32,
                          precision=lax.Precision.HIGHEST) + b_ref[0, 0]


def _ada_modulation(c, w_ada, b_ada):
    depth, d, _ = w_ada.shape
    nb = c.shape[0]
    n_mod = w_ada.shape[2] // d
    out = pl.pallas_call(
        _ada_kernel,
        grid=(depth, n_mod),
        in_specs=[
            pl.BlockSpec((nb, d), lambda l, j: (0, 0)),
            pl.BlockSpec((1, d, d), lambda l, j: (l, 0, j)),
            pl.BlockSpec((1, 1, 1, d), lambda l, j: (l, j, 0, 0)),
        ],
        out_specs=pl.BlockSpec((1, 1, nb, d), lambda l, j: (l, j, 0, 0)),
        out_shape=jax.ShapeDtypeStruct((depth, n_mod, nb, d), F32),
        compiler_params=_cparams(("arbitrary", "arbitrary")),
        name="ada_modulation",
    )(c, w_ada, b_ada.reshape(depth, n_mod, 1, d))
    return out.transpose(0, 2, 1, 3)


def _rope(x, cos, sin_signed, low_half):
    up = pltpu.roll(x, LANES - 16, axis=1)
    down = pltpu.roll(x, 16, axis=1)
    return x * cos + jnp.where(low_half, up, down) * sin_signed


def _mix_in_kernel(seq_ref, pos_ref, x_ref, mod_ref, g1_ref, w_ref, cos_ref, sin_ref,
                   gq_ref, gk_ref, hm_ref, u_ref, q_ref, k_ref, v_ref, gate_ref,
                   *, d_ssm, d_attn, d_kv):
    del seq_ref, pos_ref
    x = x_ref[...]
    mod = mod_ref[0]
    xn = x * lax.rsqrt(jnp.mean(x * x, axis=-1, keepdims=True) + EPS) * g1_ref[...]
    h = (xn * (1.0 + mod[1:2]) + mod[0:1]).astype(BF16)

    def proj(lo, hi):
        return jnp.dot(h, w_ref[:, lo:hi], preferred_element_type=F32)

    u_ref[...] = proj(0, d_ssm).astype(BF16)

    cos = cos_ref[...]
    sin_signed = sin_ref[...]
    lane = lax.broadcasted_iota(jnp.int32, cos.shape, 1)
    low_half = (lane % 32) < 16
    hm = hm_ref[...]

    def norm_rope(z, gain, scale):
        cols = []
        for j in range(z.shape[1] // LANES):
            zj = z[:, j * LANES:(j + 1) * LANES]
            msq = jnp.dot((zj * zj).astype(BF16), hm, preferred_element_type=F32)
            zn = zj * lax.rsqrt(msq + EPS) * gain
            cols.append(_rope(zn, cos, sin_signed, low_half) * scale)
        return jnp.concatenate(cols, axis=1) if len(cols) > 1 else cols[0]

    o = d_ssm
    q_ref[...] = norm_rope(proj(o, o + d_attn), gq_ref[...],
                           (HEAD_DIM ** -0.5) * LOG2E).astype(BF16)
    o += d_attn
    k_ref[...] = norm_rope(proj(o, o + d_kv), gk_ref[...], 1.0).astype(BF16)
    o += d_kv
    v_ref[...] = proj(o, o + d_kv).astype(BF16)
    o += d_kv
    n_gate = gate_ref.shape[1]
    half = n_gate // 2
    gate_ref[:, :half] = _sigmoid(proj(o, o + half)).astype(BF16)
    gate_ref[:, half:] = _sigmoid(proj(o + half, o + n_gate)).astype(BF16)


def _mix_in(x, mod, g1, w_in, cos_t, sin_t, gq_t, gk_t, hm, blk_seq, blk_pos, dims):
    t, d = x.shape
    d_ssm, d_attn, d_kv = dims
    n_gate = w_in.shape[1] - d_ssm - d_attn - 2 * d_kv
    tb = TOKEN_BLOCK
    row = lambda i, s, p: (i, 0)
    const = lambda i, s, p: (0, 0)
    grid_spec = pltpu.PrefetchScalarGridSpec(
        num_scalar_prefetch=2,
        grid=(t // tb,),
        in_specs=[
            pl.BlockSpec((tb, d), row),
            pl.BlockSpec((1,) + mod.shape[1:], lambda i, s, p: (s[i], 0, 0)),
            pl.BlockSpec((1, d), const),
            pl.BlockSpec(w_in.shape, const),
            pl.BlockSpec((tb, LANES), lambda i, s, p: (p[i], 0)),
            pl.BlockSpec((tb, LANES), lambda i, s, p: (p[i], 0)),
            pl.BlockSpec((1, LANES), const),
            pl.BlockSpec((1, LANES), const),
            pl.BlockSpec((LANES, LANES), const),
        ],
        out_specs=[
            pl.BlockSpec((tb, d_ssm), row),
            pl.BlockSpec((tb, d_attn), row),
            pl.BlockSpec((tb, d_kv), row),
            pl.BlockSpec((tb, d_kv), row),
            pl.BlockSpec((tb, n_gate), row),
        ],
    )
    return pl.pallas_call(
        functools.partial(_mix_in_kernel, d_ssm=d_ssm, d_attn=d_attn, d_kv=d_kv),
        grid_spec=grid_spec,
        out_shape=[
            jax.ShapeDtypeStruct((t, d_ssm), BF16),
            jax.ShapeDtypeStruct((t, d_attn), BF16),
            jax.ShapeDtypeStruct((t, d_kv), BF16),
            jax.ShapeDtypeStruct((t, d_kv), BF16),
            jax.ShapeDtypeStruct((t, n_gate), BF16),
        ],
        compiler_params=_cparams(("arbitrary",)),
        name="mix_in",
    )(blk_seq, blk_pos, x, mod, g1, w_in, cos_t, sin_t, gq_t, gk_t, hm)


def _ssm_weights(a_re, a_im, log_dt, b_re, b_im, c_re, c_im, d_skip):
    n_dirs, g, p = a_re.shape
    hch = b_re.shape[-1]
    c = SSM_CHUNK
    lam = lax.complex(a_re.astype(F32), a_im.astype(F32))
    dt = jnp.exp(log_dt.astype(F32))[..., None]
    lam_dt = lam * dt
    coef = (jnp.exp(lam_dt) - 1.0) / lam
    steps = jnp.arange(c + 1, dtype=F32)
    pw = jnp.exp(lam_dt[:, None] * steps[None, :, None, None])
    bc = lax.complex(b_re.astype(F32), b_im.astype(F32)) * coef[..., None]
    cc = lax.complex(c_re.astype(F32), c_im.astype(F32))

    kern = jnp.einsum('rghp,rdgp,rgpk->rdghk', cc, pw[:, :c], bc).real
    i_idx = jnp.arange(c)[:, None]
    j_idx = jnp.arange(c)[None, :]
    lag = j_idx - i_idx
    kf = jnp.where((lag >= 0)[..., None, None, None],
                   kern[0][jnp.clip(lag, 0, c - 1)], 0.0)
    kb = jnp.where((lag <= 0)[..., None, None, None],
                   kern[1][jnp.clip(-lag, 0, c - 1)], 0.0)
    eye_t = (lag == 0).astype(F32)[..., None, None, None]
    dmat = jnp.eye(hch, dtype=F32)[None] * d_skip.astype(F32).reshape(g, hch)[:, :, None]
    w_intra = kf + kb + eye_t * dmat[None, None]
    w_intra = w_intra.transpose(2, 0, 4, 1, 3).reshape(g, c * hch, c * hch)

    win_f = pw[0, c - 1 - jnp.arange(c)][:, :, None, :] * bc[0].transpose(0, 2, 1)[None]
    win_b = pw[1, jnp.arange(c)][:, :, None, :] * bc[1].transpose(0, 2, 1)[None]
    def rows(w):
        return w.transpose(1, 0, 2, 3).reshape(g, c * hch, p)
    win = [rows(win_f.real), rows(win_f.imag), rows(win_b.real), rows(win_b.imag)]

    cout_f = cc[0][None] * pw[0, 1 + jnp.arange(c)][:, :, None, :]
    cout_b = cc[1][None] * pw[1, c - jnp.arange(c)][:, :, None, :]
    def cols(w):
        return w.transpose(1, 3, 0, 2).reshape(g, p, c * hch)
    wout = [cols(cout_f.real), cols(-cout_f.imag), cols(cout_b.real), cols(-cout_b.imag)]

    gp = g // 2
    ch = c * hch
    z_in = jnp.zeros((gp, ch, p), F32)
    w_in_pair = jnp.concatenate([
        jnp.concatenate([jnp.concatenate([w[0::2], z_in], -1) for w in win], -1),
        jnp.concatenate([jnp.concatenate([z_in, w[1::2]], -1) for w in win], -1)], 1)
    z_out = jnp.zeros((gp, p, ch), F32)
    w_out_pair = jnp.concatenate([
        jnp.concatenate([jnp.concatenate([w[0::2], z_out], -1),
                         jnp.concatenate([z_out, w[1::2]], -1)], 1) for w in wout], 1)
    z_sq = jnp.zeros((gp, ch, ch), F32)
    w_intra_pair = jnp.concatenate([jnp.concatenate([w_intra[0::2], z_sq], -1),
                                    jnp.concatenate([z_sq, w_intra[1::2]], -1)], 1)
    a_chunk = pw[:, c]
    planes = [a_chunk[0].real, a_chunk[0].imag, a_chunk[1].real, a_chunk[1].imag]
    a_pair = jnp.stack([jnp.concatenate([pl_[0::2], pl_[1::2]], -1) for pl_ in planes], 1)
    a_pair = jnp.concatenate([a_pair, jnp.zeros_like(a_pair)], 1)
    return w_in_pair.astype(BF16), w_intra_pair.astype(BF16), w_out_pair.astype(BF16), a_pair


def _ssm_kernel(u_ref, win_ref, wintra_ref, wout_ref, a_ref, y_ref, x_sc, s_sc, *, seq_groups):
    u = u_ref[0]
    x_sc[...] = jnp.dot(u, win_ref[0], preferred_element_type=F32)
    a = a_ref[0]
    w = LANES
    sub = SUBLANES
    tile = (sub, w)
    a_fr, a_fi, a_br, a_bi = (jnp.broadcast_to(a[r:r + 1], tile) for r in range(4))
    row_id = lax.broadcasted_iota(jnp.int32, tile, 0)
    for n_chunks, starts in seq_groups:
        def body(cb, carry, starts=starts, n_chunks=n_chunks):
            out = []
            for start, (sfr, sfi, sbr, sbi) in zip(starts, carry):
                rowf = pl.multiple_of(start + cb * sub, sub)
                rowb = pl.multiple_of(start + n_chunks - sub - cb * sub, sub)
                xfr = x_sc[pl.ds(rowf, sub), 0:w]
                xfi = x_sc[pl.ds(rowf, sub), w:2 * w]
                xbr = x_sc[pl.ds(rowb, sub), 2 * w:3 * w]
                xbi = x_sc[pl.ds(rowb, sub), 3 * w:4 * w]
                tfr = tfi = tbr = tbi = jnp.zeros(tile, F32)
                for r in range(sub):
                    rb = sub - 1 - r
                    tfr = jnp.where(row_id == r, sfr, tfr)
                    tfi = jnp.where(row_id == r, sfi, tfi)
                    tbr = jnp.where(row_id == rb, sbr, tbr)
                    tbi = jnp.where(row_id == rb, sbi, tbi)
                    ufr = jnp.broadcast_to(xfr[r:r + 1], tile)
                    ufi = jnp.broadcast_to(xfi[r:r + 1], tile)
                    ubr = jnp.broadcast_to(xbr[rb:rb + 1], tile)
                    ubi = jnp.broadcast_to(xbi[rb:rb + 1], tile)
                    sfr, sfi = a_fr * sfr - a_fi * sfi + ufr, a_fr * sfi + a_fi * sfr + ufi
                    sbr, sbi = a_br * sbr - a_bi * sbi + ubr, a_br * sbi + a_bi * sbr + ubi
                s_sc[pl.ds(rowf, sub), 0:w] = tfr
                s_sc[pl.ds(rowf, sub), w:2 * w] = tfi
                s_sc[pl.ds(rowb, sub), 2 * w:3 * w] = tbr
                s_sc[pl.ds(rowb, sub), 3 * w:4 * w] = tbi
                out.append((sfr, sfi, sbr, sbi))
            return tuple(out)
        zero = jnp.zeros(tile, F32)
        lax.fori_loop(0, n_chunks // sub, body, tuple((zero,) * 4 for _ in starts))
    y = jnp.dot(u, wintra_ref[0], preferred_element_type=F32)
    y += jnp.dot(s_sc[...].astype(BF16), wout_ref[0], preferred_element_type=F32)
    y_ref[0] = y.astype(y_ref.dtype)


def _ssm_branch(u_pairs, w_in, w_intra, w_out, a_pair, seq_groups):
    gp, nc, width = u_pairs.shape
    blk = lambda i: (i, 0, 0)
    return pl.pallas_call(
        functools.partial(_ssm_kernel, seq_groups=seq_groups),
        grid=(gp,),
        in_specs=[
            pl.BlockSpec((1, nc, width), blk),
            pl.BlockSpec((1,) + w_in.shape[1:], blk),
            pl.BlockSpec((1,) + w_intra.shape[1:], blk),
            pl.BlockSpec((1,) + w_out.shape[1:], blk),
            pl.BlockSpec((1,) + a_pair.shape[1:], blk),
        ],
        out_specs=pl.BlockSpec((1, nc, width), blk),
        out_shape=jax.ShapeDtypeStruct((gp, nc, width), BF16),
        scratch_shapes=[pltpu.VMEM((nc, w_in.shape[2]), F32), pltpu.VMEM((nc, w_out.shape[1]), F32)],
        compiler_params=_cparams(("arbitrary",)),
        name="ssm_branch",
    )(u_pairs, w_in, w_intra, w_out, a_pair)


def _attn_kernel(q_ref, kt_ref, v_ref, o_ref, m_sc, l_sc, acc_sc, *, n_kv_tiles):
    tq = q_ref.shape[1]
    n_slices = q_ref.shape[2] // LANES
    q = jnp.concatenate([q_ref[0, :, j * LANES:(j + 1) * LANES] for j in range(n_slices)], axis=0)
    m_sc[...] = jnp.full(m_sc.shape, -jnp.inf, F32)
    l_sc[...] = jnp.zeros(l_sc.shape, F32)
    acc_sc[...] = jnp.zeros(acc_sc.shape, F32)
    n_rep = KV_TILE // LANES
    lane = lax.broadcasted_iota(jnp.int32, acc_sc.shape, 1)
    lane_head = lane // HEAD_DIM
    key_head = lax.broadcasted_iota(jnp.int32, kt_ref.shape[2:], 0) // HEAD_DIM

    def kv_step(kb, carry):
        start = pl.multiple_of(kb * KV_TILE, KV_TILE)
        kt = kt_ref[0, kb]
        v = v_ref[0, pl.ds(start, KV_TILE), :]
        acc = acc_sc[...]
        for g in range(N_KV_HEADS):
            ktg = jnp.where(key_head == g, kt, jnp.zeros_like(kt))
            s = jnp.dot(q, ktg, preferred_element_type=F32)
            m_old = m_sc[g]
            m_new = jnp.maximum(m_old, jnp.max(s, axis=-1, keepdims=True))
            alpha = jnp.exp2(m_old - m_new)
            p = jnp.exp2(s - jnp.concatenate([m_new] * n_rep, axis=1))
            l_part = p[:, 0:LANES]
            for c in range(1, n_rep):
                l_part = l_part + p[:, c * LANES:(c + 1) * LANES]
            l_sc[g] = alpha * l_sc[g] + l_part
            m_sc[g] = m_new
            pv = jnp.dot(p.astype(BF16), v, preferred_element_type=F32)
            acc = jnp.where(lane_head == g, alpha * acc + pv, acc)
        acc_sc[...] = acc
        return carry

    lax.fori_loop(0, n_kv_tiles, kv_step, 0)
    inv = [1.0 / jnp.sum(l_sc[g], axis=-1, keepdims=True) for g in range(N_KV_HEADS)]
    out = acc_sc[...] * jnp.where(lane_head == 0, inv[0], inv[1])
    for j in range(n_slices):
        o_ref[0, :, j * LANES:(j + 1) * LANES] = out[j * tq:(j + 1) * tq].astype(o_ref.dtype)


def _attention(q, kt, v):
    b, l, d_attn = q.shape
    d_kv = v.shape[2]
    tq = min(Q_TILE, l)
    rows = (d_attn // LANES) * tq
    return pl.pallas_call(
        functools.partial(_attn_kernel, n_kv_tiles=l // KV_TILE),
        grid=(b, l // tq),
        in_specs=[
            pl.BlockSpec((1, tq, d_attn), lambda i, j: (i, j, 0)),
            pl.BlockSpec((1,) + kt.shape[1:], lambda i, j: (i, 0, 0, 0)),
            pl.BlockSpec((1, l, d_kv), lambda i, j: (i, 0, 0)),
        ],
        out_specs=pl.BlockSpec((1, tq, d_attn), lambda i, j: (i, j, 0)),
        out_shape=jax.ShapeDtypeStruct((b, l, d_attn), BF16),
        scratch_shapes=[pltpu.VMEM((N_KV_HEADS, rows, LANES), F32),
                        pltpu.VMEM((N_KV_HEADS, rows, LANES), F32),
                        pltpu.VMEM((rows, LANES), F32)],
        compiler_params=_cparams(("arbitrary", "arbitrary")),
        name="attention",
    )(q, kt, v)


def _mix_out_kernel(seq_ref, x_ref, y_ref, o_ref, gate_ref, mod_ref, wglu_ref, bglu_ref,
                    wbs_ref, wba_ref, wout_ref, g2_ref, wr_ref, br_ref,
                    x1_ref, h2_ref, route_ref, *, n_groups, per_group):
    del seq_ref
    mod = mod_ref[0]
    y = y_ref[...].astype(F32)
    y = 0.5 * y * (1.0 + jnp.tanh(math.sqrt(2.0 / math.pi) * (y + 0.044715 * (y * y * y))))
    glu = jnp.dot(y.astype(BF16), wglu_ref[...], preferred_element_type=F32) + bglu_ref[...]
    ys = (y * _sigmoid(glu)).astype(BF16)
    d = x_ref.shape[1]
    gate = gate_ref[...].astype(F32)
    merged = gate[:, :d] * jnp.dot(ys, wbs_ref[...], preferred_element_type=F32)
    merged += gate[:, d:] * jnp.dot(o_ref[...], wba_ref[...], preferred_element_type=F32)
    x1 = x_ref[...] + mod[2:3] * jnp.dot(merged.astype(BF16), wout_ref[...],
                                         preferred_element_type=F32)
    x1_ref[...] = x1
    xn = x1 * lax.rsqrt(jnp.mean(x1 * x1, axis=-1, keepdims=True) + EPS) * g2_ref[...]
    h2 = (xn * (1.0 + mod[4:5]) + mod[3:4]).astype(BF16)
    h2_ref[...] = h2

    n_exp = n_groups * per_group
    logits = jnp.dot(h2, wr_ref[...], preferred_element_type=F32) + br_ref[...]
    lane = lax.broadcasted_iota(jnp.int32, logits.shape, 1).astype(F32)
    neg = jnp.float32(-jnp.inf)
    big = jnp.float32(1 << 20)
    lc = jnp.where(lane >= n_exp, jnp.where(lane < n_exp + n_groups, logits, neg), neg)
    c_max = jnp.max(lc, axis=-1, keepdims=True)
    g_idx = jnp.min(jnp.where(lc == c_max, lane - n_exp, big), axis=-1, keepdims=True)
    pc_top = 1.0 / jnp.sum(jnp.exp(lc - c_max), axis=-1, keepdims=True)
    lf = jnp.where(lane >= g_idx * per_group,
                   jnp.where(lane < (g_idx + 1.0) * per_group, logits, neg), neg)
    f1 = jnp.max(lf, axis=-1, keepdims=True)
    e1 = jnp.min(jnp.where(lf == f1, lane, big), axis=-1, keepdims=True)
    lf2 = jnp.where(lane == e1, neg, lf)
    f2 = jnp.max(lf2, axis=-1, keepdims=True)
    e2 = jnp.min(jnp.where(lf2 == f2, lane, big), axis=-1, keepdims=True)
    r = jnp.exp(f2 - f1)
    w1 = pc_top / (1.0 + r)
    w2 = pc_top * r / (1.0 + r)
    route = jnp.where(lane == 0, e1,
                      jnp.where(lane == 1, e2,
                                jnp.where(lane == 2, w1, jnp.where(lane == 3, w2, 0.0))))
    route_ref[...] = route


def _mix_out(x, y, o, gates, mod, w_glu, b_glu, w_bs, w_ba, w_out, g2, w_router, b_router,
             blk_seq, n_groups, per_group):
    t, d = x.shape
    tb = TOKEN_BLOCK
    row = lambda i, s: (i, 0)
    const = lambda i, s: (0, 0)
    grid_spec = pltpu.PrefetchScalarGridSpec(
        num_scalar_prefetch=1,
        grid=(t // tb,),
        in_specs=[
            pl.BlockSpec((tb, d), row),
            pl.BlockSpec((tb, y.shape[1]), row),
            pl.BlockSpec((tb, o.shape[1]), row),
            pl.BlockSpec((tb, gates.shape[1]), row),
            pl.BlockSpec((1,) + mod.shape[1:], lambda i, s: (s[i], 0, 0)),
            pl.BlockSpec(w_glu.shape, const),
            pl.BlockSpec(b_glu.shape, const),
            pl.BlockSpec(w_bs.shape, const),
            pl.BlockSpec(w_ba.shape, const),
            pl.BlockSpec(w_out.shape, const),
            pl.BlockSpec(g2.shape, const),
            pl.BlockSpec(w_router.shape, const),
            pl.BlockSpec(b_router.shape, const),
        ],
        out_specs=[
            pl.BlockSpec((tb, d), row),
            pl.BlockSpec((tb, d), row),
            pl.BlockSpec((tb, LANES), row),
        ],
    )
    return pl.pallas_call(
        functools.partial(_mix_out_kernel, n_groups=n_groups, per_group=per_group),
        grid_spec=grid_spec,
        out_shape=[
            jax.ShapeDtypeStruct((t, d), F32),
            jax.ShapeDtypeStruct((t, d), BF16),
            jax.ShapeDtypeStruct((t, LANES), F32),
        ],
        compiler_params=_cparams(("arbitrary",)),
        name="mix_out",
    )(blk_seq, x, y, o, gates, mod, w_glu, b_glu, w_bs, w_ba, w_out, g2, w_router, b_router)


def _expert_kernel(te_ref, nt_ref, xs_ref, wg_ref, wu_ref, wd_ref, ys_ref):
    del te_ref
    live = pl.program_id(0) < nt_ref[0]

    @pl.when(live)
    def _():
        x = xs_ref[...]
        hg = jnp.dot(x, wg_ref[0], preferred_element_type=F32)
        hu = jnp.dot(x, wu_ref[0], preferred_element_type=F32)
        act = (hg * _sigmoid(hg) * hu).astype(BF16)
        ys_ref[...] = jnp.dot(act, wd_ref[0], preferred_element_type=F32).astype(ys_ref.dtype)

    @pl.when(jnp.logical_not(live))
    def _():
        ys_ref[...] = jnp.zeros(ys_ref.shape, ys_ref.dtype)


def _experts(xs, tile_expert, n_live, w_gate, w_up, w_down):
    p, d = xs.shape
    tm = EXPERT_TILE
    f = w_gate.shape[2]
    grid_spec = pltpu.PrefetchScalarGridSpec(
        num_scalar_prefetch=2,
        grid=(p // tm,),
        in_specs=[
            pl.BlockSpec((tm, d), lambda i, te, nt: (i, 0)),
            pl.BlockSpec((1, d, f), lambda i, te, nt: (te[i], 0, 0)),
            pl.BlockSpec((1, d, f), lambda i, te, nt: (te[i], 0, 0)),
            pl.BlockSpec((1, f, d), lambda i, te, nt: (te[i], 0, 0)),
        ],
        out_specs=pl.BlockSpec((tm, d), lambda i, te, nt: (i, 0)),
    )
    return pl.pallas_call(
        _expert_kernel,
        grid_spec=grid_spec,
        out_shape=jax.ShapeDtypeStruct((p, d), BF16),
        compiler_params=_cparams(("arbitrary",)),
        name="experts",
    )(tile_expert, n_live, xs, w_gate, w_up, w_down)


def _combine_kernel(seq_ref, x_ref, y1_ref, y2_ref, route_ref, mod_ref, o_ref):
    del seq_ref
    route = route_ref[...]
    w1 = route[:, 2:3]
    w2 = route[:, 3:4]
    moe = w1 * y1_ref[...].astype(F32) + w2 * y2_ref[...].astype(F32)
    o_ref[...] = x_ref[...] + mod_ref[0][5:6] * moe


def _combine(x1, y1, y2, route, mod, blk_seq):
    t, d = x1.shape
    tb = TOKEN_BLOCK
    row = lambda i, s: (i, 0)
    grid_spec = pltpu.PrefetchScalarGridSpec(
        num_scalar_prefetch=1,
        grid=(t // tb,),
        in_specs=[
            pl.BlockSpec((tb, d), row),
            pl.BlockSpec((tb, d), row),
            pl.BlockSpec((tb, d), row),
            pl.BlockSpec((tb, LANES), row),
            pl.BlockSpec((1,) + mod.shape[1:], lambda i, s: (s[i], 0, 0)),
        ],
        out_specs=pl.BlockSpec((tb, d), row),
    )
    return pl.pallas_call(
        _combine_kernel,
        grid_spec=grid_spec,
        out_shape=jax.ShapeDtypeStruct((t, d), F32),
        compiler_params=_cparams(("arbitrary",)),
        name="moe_combine",
    )(blk_seq, x1, y1, y2, route, mod)


def _dispatch_plan(route, n_experts):
    t = route.shape[0]
    tm = EXPERT_TILE
    eid = route[:, :FINE_TOP_K].astype(jnp.int32).reshape(-1)
    onehot = (eid[:, None] == jnp.arange(n_experts, dtype=jnp.int32)[None, :]).astype(jnp.int32)
    csum = jnp.cumsum(onehot, axis=0)
    rank = jnp.take_along_axis(csum, eid[:, None], axis=1)[:, 0] - 1
    counts = csum[-1]
    padded = ((counts + tm - 1) // tm) * tm
    pad_end = jnp.cumsum(padded)
    pad_off = pad_end - padded
    dest = pad_off[eid] + rank
    n_rows = FINE_TOP_K * t + n_experts * tm
    n_tiles = n_rows // tm
    src_tok = jnp.zeros((n_rows,), jnp.int32).at[dest].set(
        jnp.arange(FINE_TOP_K * t, dtype=jnp.int32) // FINE_TOP_K)
    tile_start = jnp.arange(n_tiles, dtype=jnp.int32) * tm
    tile_expert = jnp.minimum(jnp.searchsorted(pad_end, tile_start, side='right'),
                              n_experts - 1).astype(jnp.int32)
    n_live = (pad_end[-1] // tm).astype(jnp.int32).reshape(1)
    last_live = jnp.maximum(n_live[0] - 1, 0)
    tile_expert = jnp.where(tile_start // tm < n_live[0], tile_expert, tile_expert[last_live])
    return dest.reshape(t, FINE_TOP_K), src_tok, tile_expert, n_live


def _rope_tables(max_len):
    rows = max_len // GRID_W
    row = jnp.repeat(jnp.arange(rows, dtype=F32), GRID_W)
    col = jnp.tile(jnp.arange(GRID_W, dtype=F32), rows)
    half = HEAD_DIM // 2
    inv_freq = 1.0 / (ROPE_THETA ** (jnp.arange(0, half, 2, dtype=F32) / half))
    ang_r = row[:, None] * inv_freq[None, :]
    ang_c = col[:, None] * inv_freq[None, :]
    ang = jnp.concatenate([ang_r, ang_r, ang_c, ang_c], axis=-1)
    sign = jnp.where((jnp.arange(HEAD_DIM) % (HEAD_DIM // 2)) < HEAD_DIM // 4, -1.0, 1.0)
    reps = LANES // HEAD_DIM
    return jnp.tile(jnp.cos(ang), (1, reps)), jnp.tile(jnp.sin(ang) * sign[None, :], (1, reps))


def kernel(x_prompt, x_sample, c_prompt, c_sample, w_ada, b_ada, g_norm1, g_norm2, w_in, ssm_a_re, ssm_a_im, ssm_log_dt, ssm_b_re, ssm_b_im, ssm_c_re, ssm_c_im, ssm_d, w_glu, b_glu, g_q, g_k, w_branch_ssm, w_branch_attn, w_out, w_coarse, b_coarse, w_fine, b_fine, w_expert_gate, w_expert_up, w_expert_down):
    b1, l1, d = x_prompt.shape
    b2, l2, _ = x_sample.shape
    depth = w_in.shape[0]
    d_ssm = w_glu.shape[-1]
    d_attn = w_branch_attn.shape[1]
    d_kv = N_KV_HEADS * HEAD_DIM
    n_heads = d_attn // HEAD_DIM
    n_groups = w_coarse.shape[-1]
    n_experts = w_fine.shape[-1]
    per_group = n_experts // n_groups
    n_ssm_groups = ssm_b_re.shape[2]
    ssm_h = ssm_b_re.shape[-1]
    tb = TOKEN_BLOCK
    assert l1 % tb == 0 and l2 % tb == 0 and l1 % KV_TILE == 0 and l2 % KV_TILE == 0
    assert l1 % (SSM_CHUNK * SUBLANES) == 0 and l2 % (SSM_CHUNK * SUBLANES) == 0
    assert d_attn == N_KV_HEADS * (n_heads // N_KV_HEADS) * HEAD_DIM and 2 * HEAD_DIM == LANES
    assert n_ssm_groups % 2 == 0 and ssm_h * SSM_CHUNK * 2 == 4 * LANES and n_experts + n_groups <= LANES

    seq_lens = [l1] * b1 + [l2] * b2
    t = sum(seq_lens)
    x = jnp.concatenate([x_prompt.reshape(b1 * l1, d), x_sample.reshape(b2 * l2, d)], axis=0)
    c = jnp.concatenate([c_prompt, c_sample], axis=0)
    blk_seq = np.concatenate([np.full(n // tb, s, np.int32) for s, n in enumerate(seq_lens)])
    blk_pos = np.concatenate([np.arange(n // tb, dtype=np.int32) for n in seq_lens])
    blk_seq, blk_pos = jnp.asarray(blk_seq), jnp.asarray(blk_pos)
    cos_t, sin_t = _rope_tables(max(l1, l2))

    nc = t // SSM_CHUNK
    starts = np.cumsum([0] + [n // SSM_CHUNK for n in seq_lens])[:-1]
    seq_groups = []
    for n in sorted(set(seq_lens), reverse=True):
        seq_groups.append((n // SSM_CHUNK, tuple(int(s) for s, m in zip(starts, seq_lens) if m == n)))
    seq_groups = tuple(seq_groups)

    rep = n_heads // N_KV_HEADS
    slot_head = np.array([g * rep + j for j in range(rep) for g in range(N_KV_HEADS)])
    col_perm = (slot_head[:, None] * HEAD_DIM + np.arange(HEAD_DIM)[None, :]).reshape(-1)
    in_perm = np.concatenate([np.arange(d_ssm), d_ssm + col_perm, np.arange(d_ssm + d_attn, w_in.shape[-1])])

    mod_all = _ada_modulation(c, w_ada, b_ada)
    gq_t = jnp.tile(g_q.astype(F32), (1, LANES // HEAD_DIM))
    gk_t = jnp.tile(g_k.astype(F32), (1, LANES // HEAD_DIM))
    head_mean = jnp.asarray(np.kron(np.eye(LANES // HEAD_DIM), np.full((HEAD_DIM, HEAD_DIM), 1.0 / HEAD_DIM)), BF16)
    w_router = jnp.concatenate([w_fine, w_coarse,
                                jnp.zeros((depth, d, LANES - n_experts - n_groups), F32)], -1).astype(BF16)
    b_router = jnp.concatenate([b_fine, b_coarse,
                                jnp.zeros((depth, LANES - n_experts - n_groups), F32)], -1)

    for i in range(depth):
        mod = mod_all[i]
        u, q, k, v, gates = _mix_in(
            x, mod, g_norm1[i][None], w_in[i][:, in_perm].astype(BF16), cos_t, sin_t,
            gq_t[i][None], gk_t[i][None], head_mean, blk_seq, blk_pos, (d_ssm, d_attn, d_kv))

        gp = n_ssm_groups // 2
        u_pairs = u.reshape(nc, SSM_CHUNK, gp, 2, ssm_h).transpose(2, 0, 3, 1, 4).reshape(gp, nc, -1)
        ssm_w = _ssm_weights(ssm_a_re[i], ssm_a_im[i], ssm_log_dt[i], ssm_b_re[i], ssm_b_im[i],
                             ssm_c_re[i], ssm_c_im[i], ssm_d[i])
        y_pairs = _ssm_branch(u_pairs, *ssm_w, seq_groups)
        y = y_pairs.reshape(gp, nc, 2, SSM_CHUNK, ssm_h).transpose(1, 3, 0, 2, 4).reshape(t, d_ssm)

        outs = []
        off = 0
        for bsz, ln in ((b1, l1), (b2, l2)):
            n = bsz * ln
            qs = q[off:off + n].reshape(bsz, ln, d_attn)
            kts = k[off:off + n].reshape(bsz, ln // KV_TILE, KV_TILE, d_kv).transpose(0, 1, 3, 2)
            vs = v[off:off + n].reshape(bsz, ln, d_kv)
            outs.append(_attention(qs, kts, vs).reshape(n, d_attn))
            off += n
        o = jnp.concatenate(outs, axis=0)

        w_ba = w_branch_attn[i][col_perm].astype(BF16)
        x1, h2, route = _mix_out(
            x, y, o, gates, mod, w_glu[i].astype(BF16), b_glu[i][None], w_branch_ssm[i].astype(BF16),
            w_ba, w_out[i].astype(BF16), g_norm2[i][None], w_router[i], b_router[i][None],
            blk_seq, n_groups, per_group)

        dest, src_tok, tile_expert, n_live = _dispatch_plan(route, n_experts)
        xs = jnp.take(h2, src_tok, axis=0)
        ys = _experts(xs, tile_expert, n_live, w_expert_gate[i].astype(BF16),
                      w_expert_up[i].astype(BF16), w_expert_down[i].astype(BF16))
        y1 = jnp.take(ys, dest[:, 0], axis=0)
        y2 = jnp.take(ys, dest[:, 1], axis=0)
        x = _combine(x1, y1, y2, route, mod, blk_seq)

    y_prompt = x[:b1 * l1].reshape(b1, l1, d)
    y_sample = x[b1 * l1:].reshape(b2, l2, d)
    return (y_prompt, y_sample)
```

```python
import functools
import math

import jax
import jax.numpy as jnp
import numpy as np
from jax import lax
from jax.experimental import pallas as pl
from jax.experimental.pallas import tpu as pltpu

F32 = jnp.float32
BF16 = jnp.bfloat16

EPS = 1e-6
GRID_W = 64
ROPE_THETA = 10000.0
HEAD_DIM = 64
N_KV_HEADS = 2
FINE_TOP_K = 2
LOG2E = 1.4426950408889634
LANES = 128
SUBLANES = 8
SSM_CHUNK = 16

TOKEN_BLOCK = 512
CHUNKS_PER_BLOCK = TOKEN_BLOCK // SSM_CHUNK
Q_TILE = 256
KV_TILE = TOKEN_BLOCK
EXPERT_TILE = 256
SSM_ROW_BLOCK = 512
VMEM_LIMIT = 56 * 1024 * 1024


def _sigmoid(x):
    return 1.0 / (1.0 + jnp.exp(-x))


def _cparams(sem):
    return pltpu.CompilerParams(dimension_semantics=sem, vmem_limit_bytes=VMEM_LIMIT)


def _ada_kernel(c_ref, w_ref, b_ref, o_ref):
    c = c_ref[...]
    ca = c * _sigmoid(c)
    o_ref[0, 0] = jnp.dot(ca, w_ref[0], preferred_element_type=F32,
                          precision=lax.Precision.HIGHEST) + b_ref[0, 0]


def _ada_modulation(c, w_ada, b_ada):
    depth, d, _ = w_ada.shape
    nb = c.shape[0]
    n_mod = w_ada.shape[2] // d
    out = pl.pallas_call(
        _ada_kernel,
        grid=(depth, n_mod),
        in_specs=[
            pl.BlockSpec((nb, d), lambda l, j: (0, 0)),
            pl.BlockSpec((1, d, d), lambda l, j: (l, 0, j)),
            pl.BlockSpec((1, 1, 1, d), lambda l, j: (l, j, 0, 0)),
        ],
        out_specs=pl.BlockSpec((1, 1, nb, d), lambda l, j: (l, j, 0, 0)),
        out_shape=jax.ShapeDtypeStruct((depth, n_mod, nb, d), F32),
        compiler_params=_cparams(("arbitrary", "arbitrary")),
        name="ada_modulation",
    )(c, w_ada, b_ada.reshape(depth, n_mod, 1, d))
    return out.transpose(0, 2, 1, 3)


def _rope(x, cos, sin_signed, low_half):
    up = pltpu.roll(x, LANES - 16, axis=1)
    down = pltpu.roll(x, 16, axis=1)
    return x * cos + jnp.where(low_half, up, down) * sin_signed


def _mix_in_kernel(seq_ref, pos_ref, x_ref, mod_ref, g1_ref, w_ref, cos_ref, sin_ref,
                   gq_ref, gk_ref, hm_ref, u_ref, q_ref, kt_ref, v_ref, gate_ref,
                   *, d_ssm, d_attn, d_kv):
    del seq_ref, pos_ref
    x = x_ref[...]
    mod = mod_ref[0]
    xn = x * lax.rsqrt(jnp.mean(x * x, axis=-1, keepdims=True) + EPS) * g1_ref[...]
    h = (xn * (1.0 + mod[1:2]) + mod[0:1]).astype(BF16)

    def proj(lo, hi):
        return jnp.dot(h, w_ref[:, lo:hi], preferred_element_type=F32)

    u = proj(0, d_ssm).astype(BF16)
    n_c = u.shape[0] // SSM_CHUNK
    for t in range(SSM_CHUNK):
        u_ref[t] = u[t * n_c:(t + 1) * n_c]

    cos = cos_ref[...]
    sin_signed = sin_ref[...]
    lane = lax.broadcasted_iota(jnp.int32, cos.shape, 1)
    low_half = (lane % 32) < 16
    hm = hm_ref[...]

    def norm_rope(z, gain, scale):
        cols = []
        for j in range(z.shape[1] // LANES):
            zj = z[:, j * LANES:(j + 1) * LANES]
            msq = jnp.dot((zj * zj).astype(BF16), hm, preferred_element_type=F32)
            zn = zj * lax.rsqrt(msq + EPS) * gain
            cols.append(_rope(zn, cos, sin_signed, low_half) * scale)
        return jnp.concatenate(cols, axis=1) if len(cols) > 1 else cols[0]

    o = d_ssm
    q_ref[...] = norm_rope(proj(o, o + d_attn), gq_ref[...],
                           (HEAD_DIM ** -0.5) * LOG2E).astype(BF16)
    o += d_attn
    kt_ref[0] = norm_rope(proj(o, o + d_kv), gk_ref[...], 1.0).T.astype(BF16)
    o += d_kv
    v_ref[...] = proj(o, o + d_kv).astype(BF16)
    o += d_kv
    n_gate = gate_ref.shape[1]
    half = n_gate // 2
    gate_ref[:, :half] = _sigmoid(proj(o, o + half)).astype(BF16)
    gate_ref[:, half:] = _sigmoid(proj(o + half, o + n_gate)).astype(BF16)


def _mix_in(x, mod, g1, w_in, cos_t, sin_t, gq_t, gk_t, hm, blk_seq, blk_pos, dims):
    t, d = x.shape
    d_ssm, d_attn, d_kv = dims
    n_gate = w_in.shape[1] - d_ssm - d_attn - 2 * d_kv
    tb = TOKEN_BLOCK
    row = lambda i, s, p: (i, 0)
    const = lambda i, s, p: (0, 0)
    grid_spec = pltpu.PrefetchScalarGridSpec(
        num_scalar_prefetch=2,
        grid=(t // tb,),
        in_specs=[
            pl.BlockSpec((tb, d), row),
            pl.BlockSpec((1,) + mod.shape[1:], lambda i, s, p: (s[i], 0, 0)),
            pl.BlockSpec((1, d), const),
            pl.BlockSpec(w_in.shape, const),
            pl.BlockSpec((tb, LANES), lambda i, s, p: (p[i], 0)),
            pl.BlockSpec((tb, LANES), lambda i, s, p: (p[i], 0)),
            pl.BlockSpec((1, LANES), const),
            pl.BlockSpec((1, LANES), const),
            pl.BlockSpec((LANES, LANES), const),
        ],
        out_specs=[
            pl.BlockSpec((SSM_CHUNK, CHUNKS_PER_BLOCK, d_ssm), lambda i, s, p: (0, i, 0)),
            pl.BlockSpec((tb, d_attn), row),
            pl.BlockSpec((1, d_kv, tb), lambda i, s, p: (i, 0, 0)),
            pl.BlockSpec((tb, d_kv), row),
            pl.BlockSpec((tb, n_gate), row),
        ],
    )
    return pl.pallas_call(
        functools.partial(_mix_in_kernel, d_ssm=d_ssm, d_attn=d_attn, d_kv=d_kv),
        grid_spec=grid_spec,
        out_shape=[
            jax.ShapeDtypeStruct((SSM_CHUNK, t // SSM_CHUNK, d_ssm), BF16),
            jax.ShapeDtypeStruct((t, d_attn), BF16),
            jax.ShapeDtypeStruct((t // tb, d_kv, tb), BF16),
            jax.ShapeDtypeStruct((t, d_kv), BF16),
            jax.ShapeDtypeStruct((t, n_gate), BF16),
        ],
        compiler_params=_cparams(("arbitrary",)),
        name="mix_in",
    )(blk_seq, blk_pos, x, mod, g1, w_in, cos_t, sin_t, gq_t, gk_t, hm)


def _ssm_weights(a_re, a_im, log_dt, b_re, b_im, c_re, c_im, d_skip):
    n_dirs, g, p = a_re.shape
    hch = b_re.shape[-1]
    c = SSM_CHUNK
    nq = g * hch // LANES
    gq = g // nq
    lam = lax.complex(a_re.astype(F32), a_im.astype(F32))
    dt = jnp.exp(log_dt.astype(F32))[..., None]
    lam_dt = lam * dt
    coef = (jnp.exp(lam_dt) - 1.0) / lam
    steps = jnp.arange(c + 1, dtype=F32)
    pw = jnp.exp(lam_dt[:, None] * steps[None, :, None, None])
    bc = lax.complex(b_re.astype(F32), b_im.astype(F32)) * coef[..., None]
    cc = lax.complex(c_re.astype(F32), c_im.astype(F32))
    eye = jnp.eye(gq, dtype=F32)

    kern = jnp.einsum('rghp,rdgp,rgpk->rdghk', cc, pw[:, :c], bc).real
    lag = jnp.arange(c)[None, :] - jnp.arange(c)[:, None]
    kf = jnp.where((lag >= 0)[..., None, None, None], kern[0][jnp.clip(lag, 0, c - 1)], 0.0)
    kb = jnp.where((lag <= 0)[..., None, None, None], kern[1][jnp.clip(-lag, 0, c - 1)], 0.0)
    dmat = jnp.eye(hch, dtype=F32)[None] * d_skip.astype(F32).reshape(g, hch)[:, :, None]
    core = kf + kb + (lag == 0).astype(F32)[..., None, None, None] * dmat[None, None]
    core = core.reshape(c, c, nq, gq, hch, hch)
    w_intra = jnp.einsum('ijqkoh,gk->qighjko', core, eye).reshape(nq, c * gq * hch, c * gq * hch)

    win_f = pw[0, c - 1 - jnp.arange(c)][:, :, None, :] * bc[0].transpose(0, 2, 1)[None]
    win_b = pw[1, jnp.arange(c)][:, :, None, :] * bc[1].transpose(0, 2, 1)[None]

    def in_block(w):
        parts = jnp.stack([w.real, w.imag], 0).reshape(2, c, nq, gq, hch, p)
        return jnp.einsum('atqghp,gk->qtghakp', parts, eye).reshape(nq, c * gq * hch, 2 * gq * p)

    w_in = jnp.stack([in_block(win_f), in_block(win_b)], 1)

    cout_f = cc[0][None] * pw[0, 1 + jnp.arange(c)][:, :, None, :]
    cout_b = cc[1][None] * pw[1, c - jnp.arange(c)][:, :, None, :]

    def out_block(w):
        parts = jnp.stack([w.real, -w.imag], 0).reshape(2, c, nq, gq, hch, p)
        return jnp.einsum('atqkhp,gk->qagptkh', parts, eye).reshape(nq, 2 * gq * p, c * gq * hch)

    w_out = jnp.concatenate([out_block(cout_f), out_block(cout_b)], 1)

    def halves(w):
        half = w.shape[-1] // 2
        return jnp.stack([w[..., :half], w[..., half:]], 1)

    a_chunk = pw[:, c]
    a_rows = jnp.stack([a_chunk.real, a_chunk.imag], 1).reshape(n_dirs, 2, nq, gq * p)
    a_rows = a_rows.transpose(2, 0, 1, 3)
    a_rows = jnp.concatenate([a_rows, jnp.zeros((nq, n_dirs, SUBLANES - 2, gq * p), F32)], 2)
    return w_in.astype(BF16), halves(w_intra).astype(BF16), halves(w_out).astype(BF16), a_rows


def _ssm_states_kernel(len_ref, u_ref, win_ref, a_ref, s_ref, x_sc, s_sc):
    ucat = jnp.concatenate([u_ref[t] for t in range(SSM_CHUNK)], axis=1)
    x_sc[...] = jnp.dot(ucat, win_ref[0, 0], preferred_element_type=F32)
    n_rows, width = x_sc.shape
    half = width // 2
    sub = SUBLANES
    tile = (sub, half)
    a = a_ref[0, 0]
    a_re = jnp.broadcast_to(a[0:1], tile)
    a_im = jnp.broadcast_to(a[1:2], tile)
    row_id = lax.broadcasted_iota(jnp.int32, tile, 0)
    seq_rows = len_ref[pl.program_id(2)]
    n_tiles = n_rows // sub

    def run(backward):
        def body(i, carry):
            s_re, s_im = carry
            tile_i = n_tiles - 1 - i if backward else i
            row0 = pl.multiple_of(tile_i * sub, sub)
            first = (row0 + sub) if backward else row0
            keep = jnp.where(lax.rem(first, seq_rows) == 0, 0.0, 1.0).astype(F32)
            s_re = s_re * keep
            s_im = s_im * keep
            x_re = x_sc[pl.ds(row0, sub), 0:half]
            x_im = x_sc[pl.ds(row0, sub), half:width]
            t_re = jnp.zeros(tile, F32)
            t_im = jnp.zeros(tile, F32)
            for k in range(sub):
                r = sub - 1 - k if backward else k
                t_re = jnp.where(row_id == r, s_re, t_re)
                t_im = jnp.where(row_id == r, s_im, t_im)
                u_re = jnp.broadcast_to(x_re[r:r + 1], tile)
                u_im = jnp.broadcast_to(x_im[r:r + 1], tile)
                s_re, s_im = a_re * s_re - a_im * s_im + u_re, a_re * s_im + a_im * s_re + u_im
            s_sc[pl.ds(row0, sub), 0:half] = t_re
            s_sc[pl.ds(row0, sub), half:width] = t_im
            return s_re, s_im
        zero = jnp.zeros(tile, F32)
        lax.fori_loop(0, n_tiles, body, (zero, zero))

    direction = pl.program_id(1)

    @pl.when(direction == 0)
    def _():
        run(False)

    @pl.when(direction == 1)
    def _():
        run(True)

    s_ref[...] = s_sc[...].astype(s_ref.dtype)


def _ssm_states(u_tm, w_in, a_rows, seg_rows, seg_seq_rows):
    c, nc, d_ssm = u_tm.shape
    nq, n_dirs, k, n = w_in.shape
    grid_spec = pltpu.PrefetchScalarGridSpec(
        num_scalar_prefetch=1,
        grid=(nq, n_dirs, nc // seg_rows),
        in_specs=[
            pl.BlockSpec((c, seg_rows, LANES), lambda q, r, s, ln: (0, s, q)),
            pl.BlockSpec((1, 1, k, n), lambda q, r, s, ln: (q, r, 0, 0)),
            pl.BlockSpec((1, 1) + a_rows.shape[2:], lambda q, r, s, ln: (q, r, 0, 0)),
        ],
        out_specs=pl.BlockSpec((seg_rows, n), lambda q, r, s, ln: (s, q * n_dirs + r)),
        scratch_shapes=[pltpu.VMEM((seg_rows, n), F32), pltpu.VMEM((seg_rows, n), F32)],
    )
    return pl.pallas_call(
        _ssm_states_kernel,
        grid_spec=grid_spec,
        out_shape=jax.ShapeDtypeStruct((nc, nq * n_dirs * n), BF16),
        compiler_params=_cparams(("arbitrary", "arbitrary", "arbitrary")),
        name="ssm_states",
    )(seg_seq_rows, u_tm, w_in, a_rows)


def _ssm_output_kernel(u_ref, s_ref, wintra_ref, wout_ref, y_ref):
    ucat = jnp.concatenate([u_ref[t] for t in range(SSM_CHUNK)], axis=1)
    y = jnp.dot(ucat, wintra_ref[0, 0], preferred_element_type=F32)
    y += jnp.dot(s_ref[...], wout_ref[0, 0], preferred_element_type=F32)
    for t in range(y_ref.shape[0]):
        y_ref[t] = y[:, t * LANES:(t + 1) * LANES].astype(y_ref.dtype)


def _ssm_output(u_tm, s, w_intra, w_out):
    c, nc, d_ssm = u_tm.shape
    nq, n_half, k, n = w_intra.shape
    rb = min(SSM_ROW_BLOCK, nc)
    return pl.pallas_call(
        _ssm_output_kernel,
        grid=(nq, n_half, nc // rb),
        in_specs=[
            pl.BlockSpec((c, rb, LANES), lambda q, h, i: (0, i, q)),
            pl.BlockSpec((rb, w_out.shape[2]), lambda q, h, i: (i, q)),
            pl.BlockSpec((1, 1, k, n), lambda q, h, i: (q, h, 0, 0)),
            pl.BlockSpec((1, 1) + w_out.shape[2:], lambda q, h, i: (q, h, 0, 0)),
        ],
        out_specs=pl.BlockSpec((c // n_half, rb, LANES), lambda q, h, i: (h, i, q)),
        out_shape=jax.ShapeDtypeStruct((c, nc, d_ssm), BF16),
        compiler_params=_cparams(("arbitrary", "arbitrary", "arbitrary")),
        name="ssm_output",
    )(u_tm, s, w_intra, w_out)


def _attn_kernel(q_ref, kt_ref, v_ref, o_ref, m_sc, l_sc, acc_sc, *, n_kv_tiles):
    tq = q_ref.shape[0]
    n_slices = q_ref.shape[1] // LANES
    q = jnp.concatenate([q_ref[:, j * LANES:(j + 1) * LANES] for j in range(n_slices)], axis=0)
    m_sc[...] = jnp.full(m_sc.shape, -jnp.inf, F32)
    l_sc[...] = jnp.zeros(l_sc.shape, F32)
    acc_sc[...] = jnp.zeros(acc_sc.shape, F32)
    n_rep = KV_TILE // LANES
    lane_head = lax.broadcasted_iota(jnp.int32, acc_sc.shape, 1) // HEAD_DIM
    key_head = lax.broadcasted_iota(jnp.int32, kt_ref.shape[1:], 0) // HEAD_DIM

    def kv_step(kb, carry):
        start = pl.multiple_of(kb * KV_TILE, KV_TILE)
        kt = kt_ref[kb]
        v = v_ref[pl.ds(start, KV_TILE), :]
        acc = acc_sc[...]
        for g in range(N_KV_HEADS):
            ktg = jnp.where(key_head == g, kt, jnp.zeros_like(kt))
            s = jnp.dot(q, ktg, preferred_element_type=F32)
            m_old = m_sc[g]
            m_new = jnp.maximum(m_old, jnp.max(s, axis=-1, keepdims=True))
            alpha = jnp.exp2(m_old - m_new)
            p = jnp.exp2(s - jnp.concatenate([m_new] * n_rep, axis=1))
            l_part = p[:, 0:LANES]
            for c in range(1, n_rep):
                l_part = l_part + p[:, c * LANES:(c + 1) * LANES]
            l_sc[g] = alpha * l_sc[g] + l_part
            m_sc[g] = m_new
            pv = jnp.dot(p.astype(BF16), v, preferred_element_type=F32)
            acc = jnp.where(lane_head == g, alpha * acc + pv, acc)
        acc_sc[...] = acc
        return carry

    lax.fori_loop(0, n_kv_tiles, kv_step, 0)
    inv = [1.0 / jnp.sum(l_sc[g], axis=-1, keepdims=True) for g in range(N_KV_HEADS)]
    out = acc_sc[...] * jnp.where(lane_head == 0, inv[0], inv[1])
    for j in range(n_slices):
        o_ref[:, j * LANES:(j + 1) * LANES] = out[j * tq:(j + 1) * tq].astype(o_ref.dtype)


def _attention(q, kt, v, first_token, n_seqs, seq_len):
    d_attn = q.shape[1]
    d_kv = v.shape[1]
    tq = min(Q_TILE, seq_len)
    q_tiles = seq_len // tq
    kv_tiles = seq_len // KV_TILE
    assert first_token % seq_len == 0
    seq0 = first_token // seq_len
    rows = (d_attn // LANES) * tq
    return pl.pallas_call(
        functools.partial(_attn_kernel, n_kv_tiles=kv_tiles),
        grid=(n_seqs, q_tiles),
        in_specs=[
            pl.BlockSpec((tq, d_attn), lambda i, j: ((seq0 + i) * q_tiles + j, 0)),
            pl.BlockSpec((kv_tiles, d_kv, KV_TILE), lambda i, j: (seq0 + i, 0, 0)),
            pl.BlockSpec((seq_len, d_kv), lambda i, j: (seq0 + i, 0)),
        ],
        out_specs=pl.BlockSpec((tq, d_attn), lambda i, j: (i * q_tiles + j, 0)),
        out_shape=jax.ShapeDtypeStruct((n_seqs * seq_len, d_attn), BF16),
        scratch_shapes=[pltpu.VMEM((N_KV_HEADS, rows, LANES), F32),
                        pltpu.VMEM((N_KV_HEADS, rows, LANES), F32),
                        pltpu.VMEM((rows, LANES), F32)],
        compiler_params=_cparams(("arbitrary", "arbitrary")),
        name="attention",
    )(q, kt, v)


def _mix_out_kernel(seq_ref, x_ref, y_ref, o1_ref, o2_ref, gate_ref, mod_ref, wglu_ref, bglu_ref,
                    wbs_ref, wba_ref, wout_ref, g2_ref, wr_ref, br_ref,
                    x1_ref, h2_ref, route_ref, *, n_groups, per_group, blocks_first):
    del seq_ref
    mod = mod_ref[0]
    y = jnp.concatenate([y_ref[t] for t in range(SSM_CHUNK)], axis=0).astype(F32)
    y = 0.5 * y * (1.0 + jnp.tanh(math.sqrt(2.0 / math.pi) * (y + 0.044715 * (y * y * y))))
    glu = jnp.dot(y.astype(BF16), wglu_ref[...], preferred_element_type=F32) + bglu_ref[...]
    ys = (y * _sigmoid(glu)).astype(BF16)
    d = x_ref.shape[1]
    gate = gate_ref[...].astype(F32)
    o = jnp.where(pl.program_id(0) < blocks_first, o1_ref[...], o2_ref[...])
    merged = gate[:, :d] * jnp.dot(ys, wbs_ref[...], preferred_element_type=F32)
    merged += gate[:, d:] * jnp.dot(o, wba_ref[...], preferred_element_type=F32)
    x1 = x_ref[...] + mod[2:3] * jnp.dot(merged.astype(BF16), wout_ref[...],
                                         preferred_element_type=F32)
    x1_ref[...] = x1
    xn = x1 * lax.rsqrt(jnp.mean(x1 * x1, axis=-1, keepdims=True) + EPS) * g2_ref[...]
    h2 = (xn * (1.0 + mod[4:5]) + mod[3:4]).astype(BF16)
    h2_ref[...] = h2

    n_exp = n_groups * per_group
    logits = jnp.dot(h2, wr_ref[...], preferred_element_type=F32) + br_ref[...]
    lane = lax.broadcasted_iota(jnp.int32, logits.shape, 1).astype(F32)
    neg = jnp.float32(-jnp.inf)
    big = jnp.float32(1 << 20)
    lc = jnp.where(lane >= n_exp, jnp.where(lane < n_exp + n_groups, logits, neg), neg)
    c_max = jnp.max(lc, axis=-1, keepdims=True)
    g_idx = jnp.min(jnp.where(lc == c_max, lane - n_exp, big), axis=-1, keepdims=True)
    pc_top = 1.0 / jnp.sum(jnp.exp(lc - c_max), axis=-1, keepdims=True)
    lf = jnp.where(lane >= g_idx * per_group,
                   jnp.where(lane < (g_idx + 1.0) * per_group, logits, neg), neg)
    f1 = jnp.max(lf, axis=-1, keepdims=True)
    e1 = jnp.min(jnp.where(lf == f1, lane, big), axis=-1, keepdims=True)
    lf2 = jnp.where(lane == e1, neg, lf)
    f2 = jnp.max(lf2, axis=-1, keepdims=True)
    e2 = jnp.min(jnp.where(lf2 == f2, lane, big), axis=-1, keepdims=True)
    r = jnp.exp(f2 - f1)
    w1 = pc_top / (1.0 + r)
    w2 = pc_top * r / (1.0 + r)
    route = jnp.where(lane == 0, e1,
                      jnp.where(lane == 1, e2,
                                jnp.where(lane == 2, w1, jnp.where(lane == 3, w2, 0.0))))
    route_ref[...] = route


def _mix_out(x, y_tm, o_first, o_second, gates, mod, w_glu, b_glu, w_bs, w_ba, w_out, g2,
             w_router, b_router, blk_seq, n_groups, per_group):
    t, d = x.shape
    tb = TOKEN_BLOCK
    blocks_first = o_first.shape[0] // tb
    row = lambda i, s: (i, 0)
    const = lambda i, s: (0, 0)
    grid_spec = pltpu.PrefetchScalarGridSpec(
        num_scalar_prefetch=1,
        grid=(t // tb,),
        in_specs=[
            pl.BlockSpec((tb, d), row),
            pl.BlockSpec((SSM_CHUNK, CHUNKS_PER_BLOCK, y_tm.shape[2]), lambda i, s: (0, i, 0)),
            pl.BlockSpec((tb, o_first.shape[1]), lambda i, s: (jnp.minimum(i, blocks_first - 1), 0)),
            pl.BlockSpec((tb, o_second.shape[1]), lambda i, s: (jnp.maximum(i - blocks_first, 0), 0)),
            pl.BlockSpec((tb, gates.shape[1]), row),
            pl.BlockSpec((1,) + mod.shape[1:], lambda i, s: (s[i], 0, 0)),
            pl.BlockSpec(w_glu.shape, const),
            pl.BlockSpec(b_glu.shape, const),
            pl.BlockSpec(w_bs.shape, const),
            pl.BlockSpec(w_ba.shape, const),
            pl.BlockSpec(w_out.shape, const),
            pl.BlockSpec(g2.shape, const),
            pl.BlockSpec(w_router.shape, const),
            pl.BlockSpec(b_router.shape, const),
        ],
        out_specs=[
            pl.BlockSpec((tb, d), row),
            pl.BlockSpec((tb, d), row),
            pl.BlockSpec((tb, LANES), row),
        ],
    )
    return pl.pallas_call(
        functools.partial(_mix_out_kernel, n_groups=n_groups, per_group=per_group,
                          blocks_first=blocks_first),
        grid_spec=grid_spec,
        out_shape=[
            jax.ShapeDtypeStruct((t, d), F32),
            jax.ShapeDtypeStruct((t, d), BF16),
            jax.ShapeDtypeStruct((t, LANES), F32),
        ],
        compiler_params=_cparams(("arbitrary",)),
        name="mix_out",
    )(blk_seq, x, y_tm, o_first, o_second, gates, mod, w_glu, b_glu, w_bs, w_ba, w_out, g2,
      w_router, b_router)


def _expert_kernel(te_ref, nt_ref, xs_ref, wg_ref, wu_ref, wd_ref, ys_ref):
    del te_ref
    live = pl.program_id(0) < nt_ref[0]

    @pl.when(live)
    def _():
        x = xs_ref[...]
        hg = jnp.dot(x, wg_ref[0].astype(BF16), preferred_element_type=F32)
        hu = jnp.dot(x, wu_ref[0].astype(BF16), preferred_element_type=F32)
        act = (hg * _sigmoid(hg) * hu).astype(BF16)
        ys_ref[...] = jnp.dot(act, wd_ref[0].astype(BF16),
                              preferred_element_type=F32).astype(ys_ref.dtype)

    @pl.when(jnp.logical_not(live))
    def _():
        ys_ref[...] = jnp.zeros(ys_ref.shape, ys_ref.dtype)


def _experts(xs, tile_expert, n_live, w_gate, w_up, w_down):
    p, d = xs.shape
    tm = EXPERT_TILE
    f = w_gate.shape[2]
    grid_spec = pltpu.PrefetchScalarGridSpec(
        num_scalar_prefetch=2,
        grid=(p // tm,),
        in_specs=[
            pl.BlockSpec((tm, d), lambda i, te, nt: (i, 0)),
            pl.BlockSpec((1, d, f), lambda i, te, nt: (te[i], 0, 0)),
            pl.BlockSpec((1, d, f), lambda i, te, nt: (te[i], 0, 0)),
            pl.BlockSpec((1, f, d), lambda i, te, nt: (te[i], 0, 0)),
        ],
        out_specs=pl.BlockSpec((tm, d), lambda i, te, nt: (i, 0)),
    )
    return pl.pallas_call(
        _expert_kernel,
        grid_spec=grid_spec,
        out_shape=jax.ShapeDtypeStruct((p, d), BF16),
        compiler_params=_cparams(("arbitrary",)),
        name="experts",
    )(tile_expert, n_live, xs, w_gate, w_up, w_down)


def _combine_kernel(seq_ref, x_ref, y1_ref, y2_ref, route_ref, mod_ref, o_ref):
    del seq_ref
    route = route_ref[...]
    w1 = route[:, 2:3]
    w2 = route[:, 3:4]
    moe = w1 * y1_ref[...].astype(F32) + w2 * y2_ref[...].astype(F32)
    o_ref[...] = x_ref[...] + mod_ref[0][5:6] * moe


def _combine(x1, y1, y2, route, mod, blk_seq):
    t, d = x1.shape
    tb = TOKEN_BLOCK
    row = lambda i, s: (i, 0)
    grid_spec = pltpu.PrefetchScalarGridSpec(
        num_scalar_prefetch=1,
        grid=(t // tb,),
        in_specs=[
            pl.BlockSpec((tb, d), row),
            pl.BlockSpec((tb, d), row),
            pl.BlockSpec((tb, d), row),
            pl.BlockSpec((tb, LANES), row),
            pl.BlockSpec((1,) + mod.shape[1:], lambda i, s: (s[i], 0, 0)),
        ],
        out_specs=pl.BlockSpec((tb, d), row),
    )
    return pl.pallas_call(
        _combine_kernel,
        grid_spec=grid_spec,
        out_shape=jax.ShapeDtypeStruct((t, d), F32),
        compiler_params=_cparams(("arbitrary",)),
        name="moe_combine",
    )(blk_seq, x1, y1, y2, route, mod)


def _dispatch_plan(route, n_experts):
    t = route.shape[0]
    tm = EXPERT_TILE
    eid = route[:, :FINE_TOP_K].astype(jnp.int32).reshape(-1)
    onehot = (eid[:, None] == jnp.arange(n_experts, dtype=jnp.int32)[None, :]).astype(jnp.int32)
    csum = jnp.cumsum(onehot, axis=0)
    rank = jnp.take_along_axis(csum, eid[:, None], axis=1)[:, 0] - 1
    counts = csum[-1]
    padded = ((counts + tm - 1) // tm) * tm
    pad_end = jnp.cumsum(padded)
    pad_off = pad_end - padded
    dest = pad_off[eid] + rank
    n_rows = FINE_TOP_K * t + n_experts * tm
    n_tiles = n_rows // tm
    src_tok = (jnp.arange(n_rows, dtype=jnp.int32) % t).at[dest].set(
        jnp.arange(FINE_TOP_K * t, dtype=jnp.int32) // FINE_TOP_K)
    tile_start = jnp.arange(n_tiles, dtype=jnp.int32) * tm
    tile_expert = jnp.minimum(jnp.searchsorted(pad_end, tile_start, side='right'),
                              n_experts - 1).astype(jnp.int32)
    n_live = (pad_end[-1] // tm).astype(jnp.int32).reshape(1)
    last_live = jnp.maximum(n_live[0] - 1, 0)
    tile_expert = jnp.where(tile_start // tm < n_live[0], tile_expert, tile_expert[last_live])
    return dest.reshape(t, FINE_TOP_K), src_tok, tile_expert, n_live


def _rope_tables(max_len):
    rows = max_len // GRID_W
    row = jnp.repeat(jnp.arange(rows, dtype=F32), GRID_W)
    col = jnp.tile(jnp.arange(GRID_W, dtype=F32), rows)
    half = HEAD_DIM // 2
    inv_freq = 1.0 / (ROPE_THETA ** (jnp.arange(0, half, 2, dtype=F32) / half))
    ang_r = row[:, None] * inv_freq[None, :]
    ang_c = col[:, None] * inv_freq[None, :]
    ang = jnp.concatenate([ang_r, ang_r, ang_c, ang_c], axis=-1)
    sign = jnp.where((jnp.arange(HEAD_DIM) % (HEAD_DIM // 2)) < HEAD_DIM // 4, -1.0, 1.0)
    reps = LANES // HEAD_DIM
    return jnp.tile(jnp.cos(ang), (1, reps)), jnp.tile(jnp.sin(ang) * sign[None, :], (1, reps))


def _to_block_order(a):
    n, d = a.shape
    return a.reshape(n // TOKEN_BLOCK, CHUNKS_PER_BLOCK, SSM_CHUNK, d).transpose(0, 2, 1, 3).reshape(n, d)


def _from_block_order(a):
    n, d = a.shape
    return a.reshape(n // TOKEN_BLOCK, SSM_CHUNK, CHUNKS_PER_BLOCK, d).transpose(0, 2, 1, 3).reshape(n, d)


def kernel(x_prompt, x_sample, c_prompt, c_sample, w_ada, b_ada, g_norm1, g_norm2, w_in, ssm_a_re, ssm_a_im, ssm_log_dt, ssm_b_re, ssm_b_im, ssm_c_re, ssm_c_im, ssm_d, w_glu, b_glu, g_q, g_k, w_branch_ssm, w_branch_attn, w_out, w_coarse, b_coarse, w_fine, b_fine, w_expert_gate, w_expert_up, w_expert_down):
    b1, l1, d = x_prompt.shape
    b2, l2, _ = x_sample.shape
    depth = w_in.shape[0]
    d_ssm = w_glu.shape[-1]
    d_attn = w_branch_attn.shape[1]
    d_kv = N_KV_HEADS * HEAD_DIM
    n_heads = d_attn // HEAD_DIM
    n_groups = w_coarse.shape[-1]
    n_experts = w_fine.shape[-1]
    per_group = n_experts // n_groups
    n_ssm_groups = ssm_b_re.shape[2]
    ssm_h = ssm_b_re.shape[-1]
    tb = TOKEN_BLOCK
    assert l1 % tb == 0 and l2 % tb == 0 and l1 % KV_TILE == 0 and l2 % KV_TILE == 0
    assert d_attn == N_KV_HEADS * (n_heads // N_KV_HEADS) * HEAD_DIM and 2 * HEAD_DIM == LANES
    assert n_ssm_groups * ssm_h == d_ssm and d_ssm % LANES == 0 and LANES % ssm_h == 0
    assert n_experts + n_groups <= LANES

    seq_lens = [l1] * b1 + [l2] * b2
    t = sum(seq_lens)
    x = jnp.concatenate([x_prompt.reshape(b1 * l1, d), x_sample.reshape(b2 * l2, d)], axis=0)
    x = _to_block_order(x)
    c = jnp.concatenate([c_prompt, c_sample], axis=0)
    blk_seq = np.concatenate([np.full(n // tb, s, np.int32) for s, n in enumerate(seq_lens)])
    blk_pos = np.concatenate([np.arange(n // tb, dtype=np.int32) for n in seq_lens])
    blk_seq, blk_pos = jnp.asarray(blk_seq), jnp.asarray(blk_pos)
    cos_t, sin_t = (_to_block_order(tab) for tab in _rope_tables(max(l1, l2)))

    seg_rows = max(seq_lens) // SSM_CHUNK
    seg_seq_rows = []
    filled = 0
    for n in seq_lens:
        rows = n // SSM_CHUNK
        assert rows % SUBLANES == 0 and seg_rows % rows == 0
        if filled == 0:
            seg_seq_rows.append(rows)
        assert seg_seq_rows[-1] == rows
        filled = (filled + rows) % seg_rows
    assert filled == 0
    seg_seq_rows = jnp.asarray(np.array(seg_seq_rows, np.int32))

    rep = n_heads // N_KV_HEADS
    slot_head = np.array([g * rep + j for j in range(rep) for g in range(N_KV_HEADS)])
    col_perm = (slot_head[:, None] * HEAD_DIM + np.arange(HEAD_DIM)[None, :]).reshape(-1)
    in_perm = np.concatenate([np.arange(d_ssm), d_ssm + col_perm, np.arange(d_ssm + d_attn, w_in.shape[-1])])

    mod_all = _ada_modulation(c, w_ada, b_ada)
    gq_t = jnp.tile(g_q.astype(F32), (1, LANES // HEAD_DIM))
    gk_t = jnp.tile(g_k.astype(F32), (1, LANES // HEAD_DIM))
    head_mean = jnp.asarray(np.kron(np.eye(LANES // HEAD_DIM), np.full((HEAD_DIM, HEAD_DIM), 1.0 / HEAD_DIM)), BF16)
    w_router = jnp.concatenate([w_fine, w_coarse,
                                jnp.zeros((depth, d, LANES - n_experts - n_groups), F32)], -1).astype(BF16)
    b_router = jnp.concatenate([b_fine, b_coarse,
                                jnp.zeros((depth, LANES - n_experts - n_groups), F32)], -1)

    for i in range(depth):
        mod = mod_all[i]
        u_tm, q, kt, v, gates = _mix_in(
            x, mod, g_norm1[i][None], w_in[i][:, in_perm].astype(BF16), cos_t, sin_t,
            gq_t[i][None], gk_t[i][None], head_mean, blk_seq, blk_pos, (d_ssm, d_attn, d_kv))

        ssm_w_in, ssm_w_intra, ssm_w_out, ssm_a = _ssm_weights(
            ssm_a_re[i], ssm_a_im[i], ssm_log_dt[i], ssm_b_re[i], ssm_b_im[i],
            ssm_c_re[i], ssm_c_im[i], ssm_d[i])
        states = _ssm_states(u_tm, ssm_w_in, ssm_a, seg_rows, seg_seq_rows)
        y_tm = _ssm_output(u_tm, states, ssm_w_intra, ssm_w_out)

        o_first = _attention(q, kt, v, 0, b1, l1)
        o_second = _attention(q, kt, v, b1 * l1, b2, l2)

        w_ba = w_branch_attn[i][col_perm].astype(BF16)
        x1, h2, route = _mix_out(
            x, y_tm, o_first, o_second, gates, mod, w_glu[i].astype(BF16), b_glu[i][None],
            w_branch_ssm[i].astype(BF16), w_ba, w_out[i].astype(BF16), g_norm2[i][None],
            w_router[i], b_router[i][None], blk_seq, n_groups, per_group)

        dest, src_tok, tile_expert, n_live = _dispatch_plan(route, n_experts)
        xs = jnp.take(h2, src_tok, axis=0)
        ys = _experts(xs, tile_expert, n_live, w_expert_gate[i], w_expert_up[i], w_expert_down[i])
        y1 = jnp.take(ys, dest[:, 0], axis=0)
        y2 = jnp.take(ys, dest[:, 1], axis=0)
        x = _combine(x1, y1, y2, route, mod, blk_seq)

    x = _from_block_order(x)
    y_prompt = x[:b1 * l1].reshape(b1, l1, d)
    y_sample = x[b1 * l1:].reshape(b2, l2, d)
    return (y_prompt, y_sample)
```

```python
import functools
import math

import jax
import jax.numpy as jnp
import numpy as np
from jax import lax
from jax.experimental import pallas as pl
from jax.experimental.pallas import tpu as pltpu

F32 = jnp.float32
BF16 = jnp.bfloat16

EPS = 1e-6
GRID_W = 64
ROPE_THETA = 10000.0
HEAD_DIM = 64
N_KV_HEADS = 2
FINE_TOP_K = 2
LOG2E = 1.4426950408889634
LANES = 128
SUBLANES = 8
SSM_CHUNK = 16

TOKEN_BLOCK = 512
CHUNKS_PER_BLOCK = TOKEN_BLOCK // SSM_CHUNK
Q_TILE = 256
KV_TILE = TOKEN_BLOCK
KV_STEP_TILES = 2
EXPERT_TILE = 256
SSM_ROW_BLOCK = 512
VMEM_LIMIT = 56 * 1024 * 1024


def _sigmoid(x):
    return 1.0 / (1.0 + jnp.exp(-x))


def _cparams(sem):
    return pltpu.CompilerParams(dimension_semantics=sem, vmem_limit_bytes=VMEM_LIMIT)


def _ada_kernel(c_ref, w_ref, b_ref, o_ref):
    c = c_ref[...]
    ca = c * _sigmoid(c)
    o_ref[0, 0] = jnp.dot(ca, w_ref[0], preferred_element_type=F32,
                          precision=lax.Precision.HIGHEST) + b_ref[0, 0]


def _ada_modulation(c, w_ada, b_ada):
    depth, d, _ = w_ada.shape
    nb = c.shape[0]
    n_mod = w_ada.shape[2] // d
    out = pl.pallas_call(
        _ada_kernel,
        grid=(depth, n_mod),
        in_specs=[
            pl.BlockSpec((nb, d), lambda l, j: (0, 0)),
            pl.BlockSpec((1, d, d), lambda l, j: (l, 0, j)),
            pl.BlockSpec((1, 1, 1, d), lambda l, j: (l, j, 0, 0)),
        ],
        out_specs=pl.BlockSpec((1, 1, nb, d), lambda l, j: (l, j, 0, 0)),
        out_shape=jax.ShapeDtypeStruct((depth, n_mod, nb, d), F32),
        compiler_params=_cparams(("arbitrary", "arbitrary")),
        name="ada_modulation",
    )(c, w_ada, b_ada.reshape(depth, n_mod, 1, d))
    return out.transpose(0, 2, 1, 3)


def _rope(x, cos, sin_signed, low_half):
    up = pltpu.roll(x, LANES - 16, axis=1)
    down = pltpu.roll(x, 16, axis=1)
    return x * cos + jnp.where(low_half, up, down) * sin_signed


def _mix_in_kernel(seq_ref, pos_ref, x_ref, mod_ref, g1_ref, w_ref, cos_ref, sin_ref,
                   gq_ref, gk_ref, hm_ref, u_ref, q_ref, kt_ref, v_ref, gate_ref,
                   *, d_ssm, d_attn, d_kv):
    del seq_ref, pos_ref
    x = x_ref[...]
    mod = mod_ref[0]
    xn = x * lax.rsqrt(jnp.mean(x * x, axis=-1, keepdims=True) + EPS) * g1_ref[...]
    h = (xn * (1.0 + mod[1:2]) + mod[0:1]).astype(BF16)

    def proj(lo, hi):
        return jnp.dot(h, w_ref[:, lo:hi], preferred_element_type=F32)

    u = proj(0, d_ssm).astype(BF16)
    n_c = u.shape[0] // SSM_CHUNK
    for t in range(SSM_CHUNK):
        u_ref[t] = u[t * n_c:(t + 1) * n_c]

    cos = cos_ref[...]
    sin_signed = sin_ref[...]
    lane = lax.broadcasted_iota(jnp.int32, cos.shape, 1)
    low_half = (lane % 32) < 16
    hm = hm_ref[...]

    def norm_rope(z, gain, scale):
        cols = []
        for j in range(z.shape[1] // LANES):
            zj = z[:, j * LANES:(j + 1) * LANES]
            msq = jnp.dot((zj * zj).astype(BF16), hm, preferred_element_type=F32)
            zn = zj * lax.rsqrt(msq + EPS) * gain
            cols.append(_rope(zn, cos, sin_signed, low_half) * scale)
        return jnp.concatenate(cols, axis=1) if len(cols) > 1 else cols[0]

    o = d_ssm
    q_ref[...] = norm_rope(proj(o, o + d_attn), gq_ref[...],
                           (HEAD_DIM ** -0.5) * LOG2E).astype(BF16)
    o += d_attn
    kt_ref[0] = norm_rope(proj(o, o + d_kv), gk_ref[...], 1.0).T.astype(BF16)
    o += d_kv
    v_ref[...] = proj(o, o + d_kv).astype(BF16)
    o += d_kv
    n_gate = gate_ref.shape[1]
    half = n_gate // 2
    gate_ref[:, :half] = _sigmoid(proj(o, o + half)).astype(BF16)
    gate_ref[:, half:] = _sigmoid(proj(o + half, o + n_gate)).astype(BF16)


def _mix_in(x, mod, g1, w_in, cos_t, sin_t, gq_t, gk_t, hm, blk_seq, blk_pos, dims):
    t, d = x.shape
    d_ssm, d_attn, d_kv = dims
    n_gate = w_in.shape[1] - d_ssm - d_attn - 2 * d_kv
    tb = TOKEN_BLOCK
    row = lambda i, s, p: (i, 0)
    const = lambda i, s, p: (0, 0)
    grid_spec = pltpu.PrefetchScalarGridSpec(
        num_scalar_prefetch=2,
        grid=(t // tb,),
        in_specs=[
            pl.BlockSpec((tb, d), row),
            pl.BlockSpec((1,) + mod.shape[1:], lambda i, s, p: (s[i], 0, 0)),
            pl.BlockSpec((1, d), const),
            pl.BlockSpec(w_in.shape, const),
            pl.BlockSpec((tb, LANES), lambda i, s, p: (p[i], 0)),
            pl.BlockSpec((tb, LANES), lambda i, s, p: (p[i], 0)),
            pl.BlockSpec((1, LANES), const),
            pl.BlockSpec((1, LANES), const),
            pl.BlockSpec((LANES, LANES), const),
        ],
        out_specs=[
            pl.BlockSpec((SSM_CHUNK, CHUNKS_PER_BLOCK, d_ssm), lambda i, s, p: (0, i, 0)),
            pl.BlockSpec((tb, d_attn), row),
            pl.BlockSpec((1, d_kv, tb), lambda i, s, p: (i, 0, 0)),
            pl.BlockSpec((tb, d_kv), row),
            pl.BlockSpec((tb, n_gate), row),
        ],
    )
    return pl.pallas_call(
        functools.partial(_mix_in_kernel, d_ssm=d_ssm, d_attn=d_attn, d_kv=d_kv),
        grid_spec=grid_spec,
        out_shape=[
            jax.ShapeDtypeStruct((SSM_CHUNK, t // SSM_CHUNK, d_ssm), BF16),
            jax.ShapeDtypeStruct((t, d_attn), BF16),
            jax.ShapeDtypeStruct((t // tb, d_kv, tb), BF16),
            jax.ShapeDtypeStruct((t, d_kv), BF16),
            jax.ShapeDtypeStruct((t, n_gate), BF16),
        ],
        compiler_params=_cparams(("arbitrary",)),
        name="mix_in",
    )(blk_seq, blk_pos, x, mod, g1, w_in, cos_t, sin_t, gq_t, gk_t, hm)


def _ssm_weights(a_re, a_im, log_dt, b_re, b_im, c_re, c_im, d_skip):
    n_dirs, g, p = a_re.shape
    hch = b_re.shape[-1]
    c = SSM_CHUNK
    nq = g * hch // LANES
    gq = g // nq
    lam = lax.complex(a_re.astype(F32), a_im.astype(F32))
    dt = jnp.exp(log_dt.astype(F32))[..., None]
    lam_dt = lam * dt
    coef = (jnp.exp(lam_dt) - 1.0) / lam
    steps = jnp.arange(c + 1, dtype=F32)
    pw = jnp.exp(lam_dt[:, None] * steps[None, :, None, None])
    bc = lax.complex(b_re.astype(F32), b_im.astype(F32)) * coef[..., None]
    cc = lax.complex(c_re.astype(F32), c_im.astype(F32))
    eye = jnp.eye(gq, dtype=F32)

    kern = jnp.einsum('rghp,rdgp,rgpk->rdghk', cc, pw[:, :c], bc).real
    dmat = jnp.eye(hch, dtype=F32)[None] * d_skip.astype(F32).reshape(g, hch)[:, :, None]
    lag0 = (kern[0, 0] + kern[1, 0] + dmat)[None]
    by_lag = jnp.concatenate([kern[1, 1:][::-1], lag0, kern[0, 1:]], 0)
    by_lag = by_lag.reshape(2 * c - 1, nq, gq, hch, hch)
    lag_blocks = jnp.einsum('dqkoh,gk->qdghko', by_lag, eye).reshape(nq, 2 * c - 1, LANES, LANES)

    win_f = pw[0, c - 1 - jnp.arange(c)][:, :, None, :] * bc[0].transpose(0, 2, 1)[None]
    win_b = pw[1, jnp.arange(c)][:, :, None, :] * bc[1].transpose(0, 2, 1)[None]

    def in_rows(w):
        parts = jnp.stack([w.real, w.imag], 0).reshape(2, c, nq, gq, hch, p)
        parts = parts.transpose(2, 0, 1, 3, 4, 5).reshape(nq, 2, c * gq * hch, p)
        return jnp.concatenate([parts, parts], -1)

    w_in_rows = jnp.stack([in_rows(win_f), in_rows(win_b)], 1)

    cout_f = cc[0][None] * pw[0, 1 + jnp.arange(c)][:, :, None, :]
    cout_b = cc[1][None] * pw[1, c - jnp.arange(c)][:, :, None, :]

    def out_cols(w):
        parts = jnp.stack([w.real, -w.imag], 0).reshape(2, c, nq, gq, hch, p)
        return parts.transpose(2, 0, 5, 1, 3, 4).reshape(nq, 2, p, c * gq * hch)

    w_out_cols = jnp.concatenate([out_cols(cout_f), out_cols(cout_b)], 1)

    a_chunk = pw[:, c]
    a_rows = jnp.stack([a_chunk.real, a_chunk.imag], 1).reshape(n_dirs, 2, nq, gq * p)
    a_rows = a_rows.transpose(2, 0, 1, 3)
    a_rows = jnp.concatenate([a_rows, jnp.zeros((nq, n_dirs, SUBLANES - 2, gq * p), F32)], 2)
    return w_in_rows.astype(BF16), lag_blocks.astype(BF16), w_out_cols.astype(BF16), a_rows


def _group_of(index, per_group):
    return (index // per_group) % (LANES // per_group)


def _ssm_states_kernel(len_ref, u_ref, win_ref, a_ref, s_ref, w_sc, x_sc, s_sc, *, ssm_h):
    n_rows, width = x_sc.shape
    half = width // 2

    @pl.when(pl.program_id(2) == 0)
    def _():
        shape = win_ref.shape[3:]
        row_group = _group_of(lax.broadcasted_iota(jnp.int32, shape, 0), ssm_h)
        lane_half = lax.broadcasted_iota(jnp.int32, shape, 1) // (LANES // 2)
        for a in range(2):
            rows = win_ref[0, 0, a]
            for j in range(half // LANES):
                col = a * half + j * LANES
                w_sc[:, col:col + LANES] = jnp.where(row_group == 2 * j + lane_half, rows,
                                                     jnp.zeros_like(rows))

    ucat = jnp.concatenate([u_ref[t] for t in range(SSM_CHUNK)], axis=1)
    x_sc[...] = jnp.dot(ucat, w_sc[...], preferred_element_type=F32)
    sub = SUBLANES
    tile = (sub, half)
    a = a_ref[0, 0]
    a_re = jnp.broadcast_to(a[0:1], tile)
    a_im = jnp.broadcast_to(a[1:2], tile)
    row_id = lax.broadcasted_iota(jnp.int32, tile, 0)
    seq_rows = len_ref[pl.program_id(2)]
    n_tiles = n_rows // sub

    def run(backward):
        def body(i, carry):
            s_re, s_im = carry
            tile_i = n_tiles - 1 - i if backward else i
            row0 = pl.multiple_of(tile_i * sub, sub)
            first = (row0 + sub) if backward else row0
            keep = jnp.where(lax.rem(first, seq_rows) == 0, 0.0, 1.0).astype(F32)
            s_re = s_re * keep
            s_im = s_im * keep
            x_re = x_sc[pl.ds(row0, sub), 0:half]
            x_im = x_sc[pl.ds(row0, sub), half:width]
            t_re = jnp.zeros(tile, F32)
            t_im = jnp.zeros(tile, F32)
            for k in range(sub):
                r = sub - 1 - k if backward else k
                t_re = jnp.where(row_id == r, s_re, t_re)
                t_im = jnp.where(row_id == r, s_im, t_im)
                u_re = jnp.broadcast_to(x_re[r:r + 1], tile)
                u_im = jnp.broadcast_to(x_im[r:r + 1], tile)
                s_re, s_im = a_re * s_re - a_im * s_im + u_re, a_re * s_im + a_im * s_re + u_im
            s_sc[pl.ds(row0, sub), 0:half] = t_re
            s_sc[pl.ds(row0, sub), half:width] = t_im
            return s_re, s_im
        zero = jnp.zeros(tile, F32)
        lax.fori_loop(0, n_tiles, body, (zero, zero))

    direction = pl.program_id(1)

    @pl.when(direction == 0)
    def _():
        run(False)

    @pl.when(direction == 1)
    def _():
        run(True)

    s_ref[...] = s_sc[...].astype(s_ref.dtype)


def _ssm_states(u_tm, w_in_rows, a_rows, seg_rows, seg_seq_rows, ssm_h):
    c, nc, d_ssm = u_tm.shape
    nq, n_dirs, _, k, two_p = w_in_rows.shape
    n = a_rows.shape[-1] * 2
    assert two_p == LANES and k == c * LANES
    grid_spec = pltpu.PrefetchScalarGridSpec(
        num_scalar_prefetch=1,
        grid=(nq, n_dirs, nc // seg_rows),
        in_specs=[
            pl.BlockSpec((c, seg_rows, LANES), lambda q, r, s, ln: (0, s, q)),
            pl.BlockSpec((1, 1, 2, k, two_p), lambda q, r, s, ln: (q, r, 0, 0, 0)),
            pl.BlockSpec((1, 1) + a_rows.shape[2:], lambda q, r, s, ln: (q, r, 0, 0)),
        ],
        out_specs=pl.BlockSpec((seg_rows, n), lambda q, r, s, ln: (s, q * n_dirs + r)),
        scratch_shapes=[pltpu.VMEM((k, n), BF16), pltpu.VMEM((seg_rows, n), F32),
                        pltpu.VMEM((seg_rows, n), F32)],
    )
    return pl.pallas_call(
        functools.partial(_ssm_states_kernel, ssm_h=ssm_h),
        grid_spec=grid_spec,
        out_shape=jax.ShapeDtypeStruct((nc, nq * n_dirs * n), BF16),
        compiler_params=_cparams(("arbitrary", "arbitrary", "arbitrary")),
        name="ssm_states",
    )(seg_seq_rows, u_tm, w_in_rows, a_rows)


def _ssm_output_kernel(u_ref, s_ref, lag_ref, wcol_ref, y_ref, wintra_sc, wout_sc, *, ssm_h):
    n_t_out = y_ref.shape[0]

    @pl.when(pl.program_id(2) == 0)
    def _():
        t_out0 = pl.program_id(1) * n_t_out
        for t_in in range(SSM_CHUNK):
            for j in range(n_t_out):
                wintra_sc[t_in * LANES:(t_in + 1) * LANES, j * LANES:(j + 1) * LANES] = (
                    lag_ref[0, t_out0 + j - t_in + SSM_CHUNK - 1])
        n_planes, p, n_cols = wcol_ref.shape[1:]
        col_group = _group_of(lax.broadcasted_iota(jnp.int32, (p, n_cols), 1), ssm_h)
        for r in range(n_planes):
            cols = wcol_ref[0, r]
            for k in range(LANES // ssm_h):
                row = (r * (LANES // ssm_h) + k) * p
                wout_sc[row:row + p, :] = jnp.where(col_group == k, cols, jnp.zeros_like(cols))

    ucat = jnp.concatenate([u_ref[t] for t in range(SSM_CHUNK)], axis=1)
    y = jnp.dot(ucat, wintra_sc[...], preferred_element_type=F32)
    y += jnp.dot(s_ref[...], wout_sc[...], preferred_element_type=F32)
    for t in range(n_t_out):
        y_ref[t] = y[:, t * LANES:(t + 1) * LANES].astype(y_ref.dtype)


def _ssm_output(u_tm, s, lag_blocks, w_out_cols, ssm_h):
    c, nc, d_ssm = u_tm.shape
    nq, n_planes, p, k = w_out_cols.shape
    n_half = 2
    n = k // n_half
    s_width = n_planes * (LANES // ssm_h) * p
    rb = min(SSM_ROW_BLOCK, nc)
    return pl.pallas_call(
        functools.partial(_ssm_output_kernel, ssm_h=ssm_h),
        grid=(nq, n_half, nc // rb),
        in_specs=[
            pl.BlockSpec((c, rb, LANES), lambda q, h, i: (0, i, q)),
            pl.BlockSpec((rb, s_width), lambda q, h, i: (i, q)),
            pl.BlockSpec((1,) + lag_blocks.shape[1:], lambda q, h, i: (q, 0, 0, 0)),
            pl.BlockSpec((1, n_planes, p, n), lambda q, h, i: (q, 0, 0, h)),
        ],
        out_specs=pl.BlockSpec((c // n_half, rb, LANES), lambda q, h, i: (h, i, q)),
        out_shape=jax.ShapeDtypeStruct((c, nc, d_ssm), BF16),
        scratch_shapes=[pltpu.VMEM((k, n), BF16), pltpu.VMEM((s_width, n), BF16)],
        compiler_params=_cparams(("arbitrary", "arbitrary", "arbitrary")),
        name="ssm_output",
    )(u_tm, s, lag_blocks, w_out_cols)


def _attn_kernel(q_ref, kt_ref, v_ref, o_ref, m_sc, l_sc, acc_sc, *, n_kv_tiles):
    tq = q_ref.shape[0]
    n_slices = q_ref.shape[1] // LANES
    q = jnp.concatenate([q_ref[:, j * LANES:(j + 1) * LANES] for j in range(n_slices)], axis=0)
    m_sc[...] = jnp.full(m_sc.shape, -jnp.inf, F32)
    l_sc[...] = jnp.zeros(l_sc.shape, F32)
    acc_sc[...] = jnp.zeros(acc_sc.shape, F32)
    step_tiles = min(KV_STEP_TILES, n_kv_tiles)
    kv_step_len = step_tiles * KV_TILE
    n_rep = kv_step_len // LANES
    lane_head = lax.broadcasted_iota(jnp.int32, acc_sc.shape, 1) // HEAD_DIM
    key_head = lax.broadcasted_iota(jnp.int32, (kt_ref.shape[1], kv_step_len), 0) // HEAD_DIM

    def kv_step(kb, carry):
        start = pl.multiple_of(kb * kv_step_len, kv_step_len)
        kt = jnp.concatenate([kt_ref[kb * step_tiles + i] for i in range(step_tiles)], axis=1)
        v = v_ref[pl.ds(start, kv_step_len), :]
        acc = acc_sc[...]
        for g in range(N_KV_HEADS):
            ktg = jnp.where(key_head == g, kt, jnp.zeros_like(kt))
            s = jnp.dot(q, ktg, preferred_element_type=F32)
            m_old = m_sc[g]
            m_new = jnp.maximum(m_old, jnp.max(s, axis=-1, keepdims=True))
            alpha = jnp.exp2(m_old - m_new)
            p = jnp.exp2(s - jnp.concatenate([m_new] * n_rep, axis=1))
            l_part = p[:, 0:LANES]
            for c in range(1, n_rep):
                l_part = l_part + p[:, c * LANES:(c + 1) * LANES]
            l_sc[g] = alpha * l_sc[g] + l_part
            m_sc[g] = m_new
            pv = jnp.dot(p.astype(BF16), v, preferred_element_type=F32)
            acc = jnp.where(lane_head == g, alpha * acc + pv, acc)
        acc_sc[...] = acc
        return carry

    lax.fori_loop(0, n_kv_tiles // step_tiles, kv_step, 0)
    inv = [1.0 / jnp.sum(l_sc[g], axis=-1, keepdims=True) for g in range(N_KV_HEADS)]
    out = acc_sc[...] * jnp.where(lane_head == 0, inv[0], inv[1])
    for j in range(n_slices):
        o_ref[:, j * LANES:(j + 1) * LANES] = out[j * tq:(j + 1) * tq].astype(o_ref.dtype)


def _attention(q, kt, v, first_token, n_seqs, seq_len):
    d_attn = q.shape[1]
    d_kv = v.shape[1]
    tq = min(Q_TILE, seq_len)
    q_tiles = seq_len // tq
    kv_tiles = seq_len // KV_TILE
    assert first_token % seq_len == 0
    seq0 = first_token // seq_len
    rows = (d_attn // LANES) * tq
    return pl.pallas_call(
        functools.partial(_attn_kernel, n_kv_tiles=kv_tiles),
        grid=(n_seqs, q_tiles),
        in_specs=[
            pl.BlockSpec((tq, d_attn), lambda i, j: ((seq0 + i) * q_tiles + j, 0)),
            pl.BlockSpec((kv_tiles, d_kv, KV_TILE), lambda i, j: (seq0 + i, 0, 0)),
            pl.BlockSpec((seq_len, d_kv), lambda i, j: (seq0 + i, 0)),
        ],
        out_specs=pl.BlockSpec((tq, d_attn), lambda i, j: (i * q_tiles + j, 0)),
        out_shape=jax.ShapeDtypeStruct((n_seqs * seq_len, d_attn), BF16),
        scratch_shapes=[pltpu.VMEM((N_KV_HEADS, rows, LANES), F32),
                        pltpu.VMEM((N_KV_HEADS, rows, LANES), F32),
                        pltpu.VMEM((rows, LANES), F32)],
        compiler_params=_cparams(("arbitrary", "arbitrary")),
        name="attention",
    )(q, kt, v)


def _mix_out_kernel(seq_ref, x_ref, y_ref, o1_ref, o2_ref, gate_ref, mod_ref, wglu_ref, bglu_ref,
                    wbs_ref, wba_ref, wout_ref, g2_ref, wr_ref, br_ref, ltri_ref,
                    x1_ref, h2_ref, route_ref, count_ref, cnt_sc,
                    *, n_groups, per_group, blocks_first):
    del seq_ref
    mod = mod_ref[0]
    y = jnp.concatenate([y_ref[t] for t in range(SSM_CHUNK)], axis=0).astype(F32)
    y = 0.5 * y * (1.0 + jnp.tanh(math.sqrt(2.0 / math.pi) * (y + 0.044715 * (y * y * y))))
    glu = jnp.dot(y.astype(BF16), wglu_ref[...], preferred_element_type=F32) + bglu_ref[...]
    ys = (y * _sigmoid(glu)).astype(BF16)
    d = x_ref.shape[1]
    gate = gate_ref[...].astype(F32)
    o = jnp.where(pl.program_id(0) < blocks_first, o1_ref[...], o2_ref[...])
    merged = gate[:, :d] * jnp.dot(ys, wbs_ref[...], preferred_element_type=F32)
    merged += gate[:, d:] * jnp.dot(o, wba_ref[...], preferred_element_type=F32)
    x1 = x_ref[...] + mod[2:3] * jnp.dot(merged.astype(BF16), wout_ref[...],
                                         preferred_element_type=F32)
    x1_ref[...] = x1
    xn = x1 * lax.rsqrt(jnp.mean(x1 * x1, axis=-1, keepdims=True) + EPS) * g2_ref[...]
    h2 = (xn * (1.0 + mod[4:5]) + mod[3:4]).astype(BF16)
    h2_ref[...] = h2

    n_exp = n_groups * per_group
    logits = jnp.dot(h2, wr_ref[...], preferred_element_type=F32) + br_ref[...]
    lane = lax.broadcasted_iota(jnp.int32, logits.shape, 1).astype(F32)
    neg = jnp.float32(-jnp.inf)
    big = jnp.float32(1 << 20)
    lc = jnp.where(lane >= n_exp, jnp.where(lane < n_exp + n_groups, logits, neg), neg)
    c_max = jnp.max(lc, axis=-1, keepdims=True)
    g_idx = jnp.min(jnp.where(lc == c_max, lane - n_exp, big), axis=-1, keepdims=True)
    pc_top = 1.0 / jnp.sum(jnp.exp(lc - c_max), axis=-1, keepdims=True)
    lf = jnp.where(lane >= g_idx * per_group,
                   jnp.where(lane < (g_idx + 1.0) * per_group, logits, neg), neg)
    f1 = jnp.max(lf, axis=-1, keepdims=True)
    e1 = jnp.min(jnp.where(lf == f1, lane, big), axis=-1, keepdims=True)
    lf2 = jnp.where(lane == e1, neg, lf)
    f2 = jnp.max(lf2, axis=-1, keepdims=True)
    e2 = jnp.min(jnp.where(lf2 == f2, lane, big), axis=-1, keepdims=True)
    r = jnp.exp(f2 - f1)
    w1 = pc_top / (1.0 + r)
    w2 = pc_top * r / (1.0 + r)

    @pl.when(pl.program_id(0) == 0)
    def _():
        cnt_sc[...] = jnp.zeros(cnt_sc.shape, F32)

    hot = jnp.where(lane == e1, 1.0, 0.0) + jnp.where(lane == e2, 1.0, 0.0)
    before = jnp.dot(ltri_ref[...], hot.astype(BF16), preferred_element_type=F32) + cnt_sc[0:1, :]
    r1 = jnp.sum(jnp.where(lane == e1, before, 0.0), axis=-1, keepdims=True)
    r2 = jnp.sum(jnp.where(lane == e2, before, 0.0), axis=-1, keepdims=True)
    cnt_sc[...] = cnt_sc[...] + jnp.sum(hot, axis=0, keepdims=True)
    count_ref[...] = cnt_sc[...]

    route = jnp.zeros(logits.shape, F32)
    for col, val in enumerate((e1, e2, w1, w2, r1, r2)):
        route = jnp.where(lane == col, val, route)
    route_ref[...] = route


def _mix_out(x, y_tm, o_first, o_second, gates, mod, w_glu, b_glu, w_bs, w_ba, w_out, g2,
             w_router, b_router, blk_seq, n_groups, per_group):
    t, d = x.shape
    tb = TOKEN_BLOCK
    blocks_first = o_first.shape[0] // tb
    row = lambda i, s: (i, 0)
    const = lambda i, s: (0, 0)
    grid_spec = pltpu.PrefetchScalarGridSpec(
        num_scalar_prefetch=1,
        grid=(t // tb,),
        in_specs=[
            pl.BlockSpec((tb, d), row),
            pl.BlockSpec((SSM_CHUNK, CHUNKS_PER_BLOCK, y_tm.shape[2]), lambda i, s: (0, i, 0)),
            pl.BlockSpec((tb, o_first.shape[1]), lambda i, s: (jnp.minimum(i, blocks_first - 1), 0)),
            pl.BlockSpec((tb, o_second.shape[1]), lambda i, s: (jnp.maximum(i - blocks_first, 0), 0)),
            pl.BlockSpec((tb, gates.shape[1]), row),
            pl.BlockSpec((1,) + mod.shape[1:], lambda i, s: (s[i], 0, 0)),
            pl.BlockSpec(w_glu.shape, const),
            pl.BlockSpec(b_glu.shape, const),
            pl.BlockSpec(w_bs.shape, const),
            pl.BlockSpec(w_ba.shape, const),
            pl.BlockSpec(w_out.shape, const),
            pl.BlockSpec(g2.shape, const),
            pl.BlockSpec(w_router.shape, const),
            pl.BlockSpec(b_router.shape, const),
            pl.BlockSpec((tb, tb), const),
        ],
        out_specs=[
            pl.BlockSpec((tb, d), row),
            pl.BlockSpec((tb, d), row),
            pl.BlockSpec((tb, LANES), row),
            pl.BlockSpec((SUBLANES, LANES), const),
        ],
        scratch_shapes=[pltpu.VMEM((SUBLANES, LANES), F32)],
    )
    strictly_lower = jnp.asarray(np.tril(np.ones((tb, tb), np.float32), -1), BF16)
    return pl.pallas_call(
        functools.partial(_mix_out_kernel, n_groups=n_groups, per_group=per_group,
                          blocks_first=blocks_first),
        grid_spec=grid_spec,
        out_shape=[
            jax.ShapeDtypeStruct((t, d), F32),
            jax.ShapeDtypeStruct((t, d), BF16),
            jax.ShapeDtypeStruct((t, LANES), F32),
            jax.ShapeDtypeStruct((SUBLANES, LANES), F32),
        ],
        compiler_params=_cparams(("arbitrary",)),
        name="mix_out",
    )(blk_seq, x, y_tm, o_first, o_second, gates, mod, w_glu, b_glu, w_bs, w_ba, w_out, g2,
      w_router, b_router, strictly_lower)


def _expert_kernel(te_ref, nt_ref, xs_ref, wg_ref, wu_ref, wd_ref, ys_ref):
    del te_ref
    live = pl.program_id(0) < nt_ref[0]

    @pl.when(live)
    def _():
        x = xs_ref[...]
        hg = jnp.dot(x, wg_ref[0].astype(BF16), preferred_element_type=F32)
        hu = jnp.dot(x, wu_ref[0].astype(BF16), preferred_element_type=F32)
        act = (hg * _sigmoid(hg) * hu).astype(BF16)
        ys_ref[...] = jnp.dot(act, wd_ref[0].astype(BF16),
                              preferred_element_type=F32).astype(ys_ref.dtype)

    @pl.when(jnp.logical_not(live))
    def _():
        ys_ref[...] = jnp.zeros(ys_ref.shape, ys_ref.dtype)


def _experts(xs, tile_expert, n_live, w_gate, w_up, w_down):
    p, d = xs.shape
    tm = EXPERT_TILE
    f = w_gate.shape[2]
    grid_spec = pltpu.PrefetchScalarGridSpec(
        num_scalar_prefetch=2,
        grid=(p // tm,),
        in_specs=[
            pl.BlockSpec((tm, d), lambda i, te, nt: (i, 0)),
            pl.BlockSpec((1, d, f), lambda i, te, nt: (te[i], 0, 0)),
            pl.BlockSpec((1, d, f), lambda i, te, nt: (te[i], 0, 0)),
            pl.BlockSpec((1, f, d), lambda i, te, nt: (te[i], 0, 0)),
        ],
        out_specs=pl.BlockSpec((tm, d), lambda i, te, nt: (i, 0)),
    )
    return pl.pallas_call(
        _expert_kernel,
        grid_spec=grid_spec,
        out_shape=jax.ShapeDtypeStruct((p, d), BF16),
        compiler_params=_cparams(("arbitrary",)),
        name="experts",
    )(tile_expert, n_live, xs, w_gate, w_up, w_down)


def _combine_kernel(seq_ref, x_ref, y1_ref, y2_ref, route_ref, mod_ref, o_ref):
    del seq_ref
    route = route_ref[...]
    w1 = route[:, 2:3]
    w2 = route[:, 3:4]
    moe = w1 * y1_ref[...].astype(F32) + w2 * y2_ref[...].astype(F32)
    o_ref[...] = x_ref[...] + mod_ref[0][5:6] * moe


def _combine(x1, y1, y2, route, mod, blk_seq):
    t, d = x1.shape
    tb = TOKEN_BLOCK
    row = lambda i, s: (i, 0)
    grid_spec = pltpu.PrefetchScalarGridSpec(
        num_scalar_prefetch=1,
        grid=(t // tb,),
        in_specs=[
            pl.BlockSpec((tb, d), row),
            pl.BlockSpec((tb, d), row),
            pl.BlockSpec((tb, d), row),
            pl.BlockSpec((tb, LANES), row),
            pl.BlockSpec((1,) + mod.shape[1:], lambda i, s: (s[i], 0, 0)),
        ],
        out_specs=pl.BlockSpec((tb, d), row),
    )
    return pl.pallas_call(
        _combine_kernel,
        grid_spec=grid_spec,
        out_shape=jax.ShapeDtypeStruct((t, d), F32),
        compiler_params=_cparams(("arbitrary",)),
        name="moe_combine",
    )(blk_seq, x1, y1, y2, route, mod)


def _dispatch_plan(route, counts, n_experts):
    t = route.shape[0]
    tm = EXPERT_TILE
    eid = route[:, :FINE_TOP_K].astype(jnp.int32).reshape(-1)
    rank = route[:, 2 * FINE_TOP_K:3 * FINE_TOP_K].astype(jnp.int32).reshape(-1)
    counts = counts[0, :n_experts].astype(jnp.int32)
    padded = ((counts + tm - 1) // tm) * tm
    pad_end = jnp.cumsum(padded)
    pad_off = pad_end - padded
    dest = pad_off[eid] + rank
    n_rows = FINE_TOP_K * t + n_experts * tm
    n_tiles = n_rows // tm
    src_tok = (jnp.arange(n_rows, dtype=jnp.int32) % t).at[dest].set(
        jnp.arange(FINE_TOP_K * t, dtype=jnp.int32) // FINE_TOP_K)
    tile_start = jnp.arange(n_tiles, dtype=jnp.int32) * tm
    tile_expert = jnp.minimum(jnp.searchsorted(pad_end, tile_start, side='right'),
                              n_experts - 1).astype(jnp.int32)
    n_live = (pad_end[-1] // tm).astype(jnp.int32).reshape(1)
    last_live = jnp.maximum(n_live[0] - 1, 0)
    tile_expert = jnp.where(tile_start // tm < n_live[0], tile_expert, tile_expert[last_live])
    return dest.reshape(t, FINE_TOP_K), src_tok, tile_expert, n_live


def _rope_tables(max_len):
    rows = max_len // GRID_W
    row = jnp.repeat(jnp.arange(rows, dtype=F32), GRID_W)
    col = jnp.tile(jnp.arange(GRID_W, dtype=F32), rows)
    half = HEAD_DIM // 2
    inv_freq = 1.0 / (ROPE_THETA ** (jnp.arange(0, half, 2, dtype=F32) / half))
    ang_r = row[:, None] * inv_freq[None, :]
    ang_c = col[:, None] * inv_freq[None, :]
    ang = jnp.concatenate([ang_r, ang_r, ang_c, ang_c], axis=-1)
    sign = jnp.where((jnp.arange(HEAD_DIM) % (HEAD_DIM // 2)) < HEAD_DIM // 4, -1.0, 1.0)
    reps = LANES // HEAD_DIM
    return jnp.tile(jnp.cos(ang), (1, reps)), jnp.tile(jnp.sin(ang) * sign[None, :], (1, reps))


def _to_block_order(a):
    n, d = a.shape
    return a.reshape(n // TOKEN_BLOCK, CHUNKS_PER_BLOCK, SSM_CHUNK, d).transpose(0, 2, 1, 3).reshape(n, d)


def _from_block_order(a):
    n, d = a.shape
    return a.reshape(n // TOKEN_BLOCK, SSM_CHUNK, CHUNKS_PER_BLOCK, d).transpose(0, 2, 1, 3).reshape(n, d)


def kernel(x_prompt, x_sample, c_prompt, c_sample, w_ada, b_ada, g_norm1, g_norm2, w_in, ssm_a_re, ssm_a_im, ssm_log_dt, ssm_b_re, ssm_b_im, ssm_c_re, ssm_c_im, ssm_d, w_glu, b_glu, g_q, g_k, w_branch_ssm, w_branch_attn, w_out, w_coarse, b_coarse, w_fine, b_fine, w_expert_gate, w_expert_up, w_expert_down):
    b1, l1, d = x_prompt.shape
    b2, l2, _ = x_sample.shape
    depth = w_in.shape[0]
    d_ssm = w_glu.shape[-1]
    d_attn = w_branch_attn.shape[1]
    d_kv = N_KV_HEADS * HEAD_DIM
    n_heads = d_attn // HEAD_DIM
    n_groups = w_coarse.shape[-1]
    n_experts = w_fine.shape[-1]
    per_group = n_experts // n_groups
    n_ssm_groups = ssm_b_re.shape[2]
    ssm_h = ssm_b_re.shape[-1]
    tb = TOKEN_BLOCK
    assert l1 % tb == 0 and l2 % tb == 0 and l1 % KV_TILE == 0 and l2 % KV_TILE == 0
    assert d_attn == N_KV_HEADS * (n_heads // N_KV_HEADS) * HEAD_DIM and 2 * HEAD_DIM == LANES
    assert n_ssm_groups * ssm_h == d_ssm and d_ssm % LANES == 0 and LANES % ssm_h == 0
    assert 2 * ssm_a_re.shape[-1] == LANES
    assert n_experts + n_groups <= LANES

    seq_lens = [l1] * b1 + [l2] * b2
    t = sum(seq_lens)
    x = jnp.concatenate([x_prompt.reshape(b1 * l1, d), x_sample.reshape(b2 * l2, d)], axis=0)
    x = _to_block_order(x)
    c = jnp.concatenate([c_prompt, c_sample], axis=0)
    blk_seq = np.concatenate([np.full(n // tb, s, np.int32) for s, n in enumerate(seq_lens)])
    blk_pos = np.concatenate([np.arange(n // tb, dtype=np.int32) for n in seq_lens])
    blk_seq, blk_pos = jnp.asarray(blk_seq), jnp.asarray(blk_pos)
    cos_t, sin_t = (_to_block_order(tab) for tab in _rope_tables(max(l1, l2)))

    seg_rows = max(seq_lens) // SSM_CHUNK
    seg_seq_rows = []
    filled = 0
    for n in seq_lens:
        rows = n // SSM_CHUNK
        assert rows % SUBLANES == 0 and seg_rows % rows == 0
        if filled == 0:
            seg_seq_rows.append(rows)
        assert seg_seq_rows[-1] == rows
        filled = (filled + rows) % seg_rows
    assert filled == 0
    seg_seq_rows = jnp.asarray(np.array(seg_seq_rows, np.int32))

    rep = n_heads // N_KV_HEADS
    slot_head = np.array([g * rep + j for j in range(rep) for g in range(N_KV_HEADS)])
    col_perm = (slot_head[:, None] * HEAD_DIM + np.arange(HEAD_DIM)[None, :]).reshape(-1)
    in_perm = np.concatenate([np.arange(d_ssm), d_ssm + col_perm, np.arange(d_ssm + d_attn, w_in.shape[-1])])

    mod_all = _ada_modulation(c, w_ada, b_ada)
    gq_t = jnp.tile(g_q.astype(F32), (1, LANES // HEAD_DIM))
    gk_t = jnp.tile(g_k.astype(F32), (1, LANES // HEAD_DIM))
    head_mean = jnp.asarray(np.kron(np.eye(LANES // HEAD_DIM), np.full((HEAD_DIM, HEAD_DIM), 1.0 / HEAD_DIM)), BF16)
    w_router = jnp.concatenate([w_fine, w_coarse,
                                jnp.zeros((depth, d, LANES - n_experts - n_groups), F32)], -1).astype(BF16)
    b_router = jnp.concatenate([b_fine, b_coarse,
                                jnp.zeros((depth, LANES - n_experts - n_groups), F32)], -1)

    for i in range(depth):
        mod = mod_all[i]
        u_tm, q, kt, v, gates = _mix_in(
            x, mod, g_norm1[i][None], w_in[i][:, in_perm].astype(BF16), cos_t, sin_t,
            gq_t[i][None], gk_t[i][None], head_mean, blk_seq, blk_pos, (d_ssm, d_attn, d_kv))

        w_in_rows, lag_blocks, w_out_cols, ssm_a = _ssm_weights(
            ssm_a_re[i], ssm_a_im[i], ssm_log_dt[i], ssm_b_re[i], ssm_b_im[i],
            ssm_c_re[i], ssm_c_im[i], ssm_d[i])
        states = _ssm_states(u_tm, w_in_rows, ssm_a, seg_rows, seg_seq_rows, ssm_h)
        y_tm = _ssm_output(u_tm, states, lag_blocks, w_out_cols, ssm_h)

        o_first = _attention(q, kt, v, 0, b1, l1)
        o_second = _attention(q, kt, v, b1 * l1, b2, l2)

        w_ba = w_branch_attn[i][col_perm].astype(BF16)
        x1, h2, route, counts = _mix_out(
            x, y_tm, o_first, o_second, gates, mod, w_glu[i].astype(BF16), b_glu[i][None],
            w_branch_ssm[i].astype(BF16), w_ba, w_out[i].astype(BF16), g_norm2[i][None],
            w_router[i], b_router[i][None], blk_seq, n_groups, per_group)

        dest, src_tok, tile_expert, n_live = _dispatch_plan(route, counts, n_experts)
        xs = jnp.take(h2, src_tok, axis=0)
        ys = _experts(xs, tile_expert, n_live, w_expert_gate[i], w_expert_up[i], w_expert_down[i])
        y1 = jnp.take(ys, dest[:, 0], axis=0)
        y2 = jnp.take(ys, dest[:, 1], axis=0)
        x = _combine(x1, y1, y2, route, mod, blk_seq)

    x = _from_block_order(x)
    y_prompt = x[:b1 * l1].reshape(b1, l1, d)
    y_sample = x[b1 * l1:].reshape(b2, l2, d)
    return (y_prompt, y_sample)
```

```python
import functools
import math

import jax
import jax.numpy as jnp
import numpy as np
from jax import lax
from jax.experimental import pallas as pl
from jax.experimental.pallas import tpu as pltpu

F32 = jnp.float32
BF16 = jnp.bfloat16

EPS = 1e-6
GRID_W = 64
ROPE_THETA = 10000.0
HEAD_DIM = 64
N_KV_HEADS = 2
FINE_TOP_K = 2
LOG2E = 1.4426950408889634
LANES = 128
SUBLANES = 8
SSM_CHUNK = 16

TOKEN_BLOCK = 512
CHUNKS_PER_BLOCK = TOKEN_BLOCK // SSM_CHUNK
Q_TILE = 256
KV_TILE = TOKEN_BLOCK
KV_STEP_TILES = 4
EXPERT_TILE = 512
SSM_ROW_BLOCK = 512
VMEM_LIMIT = 56 * 1024 * 1024


def _sigmoid(x):
    return 1.0 / (1.0 + jnp.exp(-x))


def _cparams(sem):
    return pltpu.CompilerParams(dimension_semantics=sem, vmem_limit_bytes=VMEM_LIMIT)


def _ada_kernel(c_ref, w_ref, b_ref, o_ref):
    c = c_ref[...]
    ca = c * _sigmoid(c)
    o_ref[0, 0] = jnp.dot(ca, w_ref[0], preferred_element_type=F32,
                          precision=lax.Precision.HIGHEST) + b_ref[0, 0]


def _ada_modulation(c, w_ada, b_ada):
    depth, d, _ = w_ada.shape
    nb = c.shape[0]
    n_mod = w_ada.shape[2] // d
    out = pl.pallas_call(
        _ada_kernel,
        grid=(depth, n_mod),
        in_specs=[
            pl.BlockSpec((nb, d), lambda l, j: (0, 0)),
            pl.BlockSpec((1, d, d), lambda l, j: (l, 0, j)),
            pl.BlockSpec((1, 1, 1, d), lambda l, j: (l, j, 0, 0)),
        ],
        out_specs=pl.BlockSpec((1, 1, nb, d), lambda l, j: (l, j, 0, 0)),
        out_shape=jax.ShapeDtypeStruct((depth, n_mod, nb, d), F32),
        compiler_params=_cparams(("arbitrary", "arbitrary")),
        name="ada_modulation",
    )(c, w_ada, b_ada.reshape(depth, n_mod, 1, d))
    return out.transpose(0, 2, 1, 3)


def _rope(x, cos, sin_signed, low_half):
    up = pltpu.roll(x, LANES - 16, axis=1)
    down = pltpu.roll(x, 16, axis=1)
    return x * cos + jnp.where(low_half, up, down) * sin_signed


def _mix_in_kernel(seq_ref, pos_ref, x_ref, mod_ref, g1_ref, w_ref, cos_ref, sin_ref,
                   gq_ref, gk_ref, hm_ref, u_ref, q_ref, kt_ref, v_ref, gate_ref,
                   *, d_ssm, d_attn, d_kv):
    del seq_ref, pos_ref
    x = x_ref[...]
    mod = mod_ref[0]
    xn = x * lax.rsqrt(jnp.mean(x * x, axis=-1, keepdims=True) + EPS) * g1_ref[...]
    h = (xn * (1.0 + mod[1:2]) + mod[0:1]).astype(BF16)

    def proj(lo, hi):
        return jnp.dot(h, w_ref[:, lo:hi], preferred_element_type=F32)

    u = proj(0, d_ssm).astype(BF16)
    n_c = u.shape[0] // SSM_CHUNK
    for t in range(SSM_CHUNK):
        u_ref[t] = u[t * n_c:(t + 1) * n_c]

    cos = cos_ref[...]
    sin_signed = sin_ref[...]
    lane = lax.broadcasted_iota(jnp.int32, cos.shape, 1)
    low_half = (lane % 32) < 16
    hm = hm_ref[...]

    def norm_rope(z, gain, scale):
        cols = []
        for j in range(z.shape[1] // LANES):
            zj = z[:, j * LANES:(j + 1) * LANES]
            msq = jnp.dot((zj * zj).astype(BF16), hm, preferred_element_type=F32)
            zn = zj * lax.rsqrt(msq + EPS) * gain
            cols.append(_rope(zn, cos, sin_signed, low_half) * scale)
        return jnp.concatenate(cols, axis=1) if len(cols) > 1 else cols[0]

    o = d_ssm
    q_ref[...] = norm_rope(proj(o, o + d_attn), gq_ref[...],
                           (HEAD_DIM ** -0.5) * LOG2E).astype(BF16)
    o += d_attn
    kt_ref[0] = norm_rope(proj(o, o + d_kv), gk_ref[...], 1.0).T.astype(BF16)
    o += d_kv
    v_ref[...] = proj(o, o + d_kv).astype(BF16)
    o += d_kv
    n_gate = gate_ref.shape[1]
    half = n_gate // 2
    gate_ref[:, :half] = _sigmoid(proj(o, o + half)).astype(BF16)
    gate_ref[:, half:] = _sigmoid(proj(o + half, o + n_gate)).astype(BF16)


def _mix_in(x, mod, g1, w_in, cos_t, sin_t, gq_t, gk_t, hm, blk_seq, blk_pos, dims):
    t, d = x.shape
    d_ssm, d_attn, d_kv = dims
    n_gate = w_in.shape[1] - d_ssm - d_attn - 2 * d_kv
    tb = TOKEN_BLOCK
    row = lambda i, s, p: (i, 0)
    const = lambda i, s, p: (0, 0)
    grid_spec = pltpu.PrefetchScalarGridSpec(
        num_scalar_prefetch=2,
        grid=(t // tb,),
        in_specs=[
            pl.BlockSpec((tb, d), row),
            pl.BlockSpec((1,) + mod.shape[1:], lambda i, s, p: (s[i], 0, 0)),
            pl.BlockSpec((1, d), const),
            pl.BlockSpec(w_in.shape, const),
            pl.BlockSpec((tb, LANES), lambda i, s, p: (p[i], 0)),
            pl.BlockSpec((tb, LANES), lambda i, s, p: (p[i], 0)),
            pl.BlockSpec((1, LANES), const),
            pl.BlockSpec((1, LANES), const),
            pl.BlockSpec((LANES, LANES), const),
        ],
        out_specs=[
            pl.BlockSpec((SSM_CHUNK, CHUNKS_PER_BLOCK, d_ssm), lambda i, s, p: (0, i, 0)),
            pl.BlockSpec((tb, d_attn), row),
            pl.BlockSpec((1, d_kv, tb), lambda i, s, p: (i, 0, 0)),
            pl.BlockSpec((tb, d_kv), row),
            pl.BlockSpec((tb, n_gate), row),
        ],
    )
    return pl.pallas_call(
        functools.partial(_mix_in_kernel, d_ssm=d_ssm, d_attn=d_attn, d_kv=d_kv),
        grid_spec=grid_spec,
        out_shape=[
            jax.ShapeDtypeStruct((SSM_CHUNK, t // SSM_CHUNK, d_ssm), BF16),
            jax.ShapeDtypeStruct((t, d_attn), BF16),
            jax.ShapeDtypeStruct((t // tb, d_kv, tb), BF16),
            jax.ShapeDtypeStruct((t, d_kv), BF16),
            jax.ShapeDtypeStruct((t, n_gate), BF16),
        ],
        compiler_params=_cparams(("arbitrary",)),
        name="mix_in",
    )(blk_seq, blk_pos, x, mod, g1, w_in, cos_t, sin_t, gq_t, gk_t, hm)


def _ssm_weights(a_re, a_im, log_dt, b_re, b_im, c_re, c_im, d_skip):
    n_dirs, g, p = a_re.shape
    hch = b_re.shape[-1]
    c = SSM_CHUNK
    nq = g * hch // LANES
    gq = g // nq
    lam = lax.complex(a_re.astype(F32), a_im.astype(F32))
    dt = jnp.exp(log_dt.astype(F32))[..., None]
    lam_dt = lam * dt
    coef = (jnp.exp(lam_dt) - 1.0) / lam
    steps = jnp.arange(c + 1, dtype=F32)
    pw = jnp.exp(lam_dt[:, None] * steps[None, :, None, None])
    bc = lax.complex(b_re.astype(F32), b_im.astype(F32)) * coef[..., None]
    cc = lax.complex(c_re.astype(F32), c_im.astype(F32))
    eye = jnp.eye(gq, dtype=F32)

    kern = jnp.einsum('rghp,rdgp,rgpk->rdghk', cc, pw[:, :c], bc).real
    dmat = jnp.eye(hch, dtype=F32)[None] * d_skip.astype(F32).reshape(g, hch)[:, :, None]
    lag0 = (kern[0, 0] + kern[1, 0] + dmat)[None]
    by_lag = jnp.concatenate([kern[1, 1:][::-1], lag0, kern[0, 1:]], 0)
    by_lag = by_lag.reshape(2 * c - 1, nq, gq, hch, hch)
    lag_blocks = jnp.einsum('dqkoh,gk->qdghko', by_lag, eye).reshape(nq, 2 * c - 1, LANES, LANES)

    win_f = pw[0, c - 1 - jnp.arange(c)][:, :, None, :] * bc[0].transpose(0, 2, 1)[None]
    win_b = pw[1, jnp.arange(c)][:, :, None, :] * bc[1].transpose(0, 2, 1)[None]

    def in_rows(w):
        parts = jnp.stack([w.real, w.imag], 0).reshape(2, c, nq, gq, hch, p)
        parts = parts.transpose(2, 0, 1, 3, 4, 5).reshape(nq, 2, c * gq * hch, p)
        return jnp.concatenate([parts, parts], -1)

    w_in_rows = jnp.stack([in_rows(win_f), in_rows(win_b)], 1)

    cout_f = cc[0][None] * pw[0, 1 + jnp.arange(c)][:, :, None, :]
    cout_b = cc[1][None] * pw[1, c - jnp.arange(c)][:, :, None, :]

    def out_cols(w):
        parts = jnp.stack([w.real, -w.imag], 0).reshape(2, c, nq, gq, hch, p)
        return parts.transpose(2, 0, 5, 1, 3, 4).reshape(nq, 2, p, c * gq * hch)

    w_out_cols = jnp.concatenate([out_cols(cout_f), out_cols(cout_b)], 1)

    a_chunk = pw[:, c]
    a_rows = jnp.stack([a_chunk.real, a_chunk.imag], 1).reshape(n_dirs, 2, nq, gq * p)
    a_rows = a_rows.transpose(2, 0, 1, 3)
    a_rows = jnp.concatenate([a_rows, jnp.zeros((nq, n_dirs, SUBLANES - 2, gq * p), F32)], 2)
    return w_in_rows.astype(BF16), lag_blocks.astype(BF16), w_out_cols.astype(BF16), a_rows


def _group_of(index, per_group):
    return (index // per_group) % (LANES // per_group)


def _ssm_states_kernel(len_ref, u_ref, win_ref, a_ref, s_ref, w_sc, x_sc, s_sc, *, ssm_h):
    n_rows, width = x_sc.shape
    half = width // 2

    @pl.when(pl.program_id(2) == 0)
    def _():
        shape = win_ref.shape[3:]
        row_group = _group_of(lax.broadcasted_iota(jnp.int32, shape, 0), ssm_h)
        lane_half = lax.broadcasted_iota(jnp.int32, shape, 1) // (LANES // 2)
        for a in range(2):
            rows = win_ref[0, 0, a]
            for j in range(half // LANES):
                col = a * half + j * LANES
                w_sc[:, col:col + LANES] = jnp.where(row_group == 2 * j + lane_half, rows,
                                                     jnp.zeros_like(rows))

    ucat = jnp.concatenate([u_ref[t] for t in range(SSM_CHUNK)], axis=1)
    x_sc[...] = jnp.dot(ucat, w_sc[...], preferred_element_type=F32)
    sub = SUBLANES
    tile = (sub, half)
    a = a_ref[0, 0]
    a_re = jnp.broadcast_to(a[0:1], tile)
    a_im = jnp.broadcast_to(a[1:2], tile)
    row_id = lax.broadcasted_iota(jnp.int32, tile, 0)
    seq_rows = len_ref[pl.program_id(2)]
    n_tiles = n_rows // sub

    def run(backward):
        def body(i, carry):
            s_re, s_im = carry
            tile_i = n_tiles - 1 - i if backward else i
            row0 = pl.multiple_of(tile_i * sub, sub)
            first = (row0 + sub) if backward else row0
            keep = jnp.where(lax.rem(first, seq_rows) == 0, 0.0, 1.0).astype(F32)
            s_re = s_re * keep
            s_im = s_im * keep
            x_re = x_sc[pl.ds(row0, sub), 0:half]
            x_im = x_sc[pl.ds(row0, sub), half:width]
            t_re = jnp.zeros(tile, F32)
            t_im = jnp.zeros(tile, F32)
            for k in range(sub):
                r = sub - 1 - k if backward else k
                t_re = jnp.where(row_id == r, s_re, t_re)
                t_im = jnp.where(row_id == r, s_im, t_im)
                u_re = jnp.broadcast_to(x_re[r:r + 1], tile)
                u_im = jnp.broadcast_to(x_im[r:r + 1], tile)
                s_re, s_im = a_re * s_re - a_im * s_im + u_re, a_re * s_im + a_im * s_re + u_im
            s_sc[pl.ds(row0, sub), 0:half] = t_re
            s_sc[pl.ds(row0, sub), half:width] = t_im
            return s_re, s_im
        zero = jnp.zeros(tile, F32)
        lax.fori_loop(0, n_tiles, body, (zero, zero))

    direction = pl.program_id(1)

    @pl.when(direction == 0)
    def _():
        run(False)

    @pl.when(direction == 1)
    def _():
        run(True)

    s_ref[...] = s_sc[...].astype(s_ref.dtype)


def _ssm_states(u_tm, w_in_rows, a_rows, seg_rows, seg_seq_rows, ssm_h):
    c, nc, d_ssm = u_tm.shape
    nq, n_dirs, _, k, two_p = w_in_rows.shape
    n = a_rows.shape[-1] * 2
    assert two_p == LANES and k == c * LANES
    grid_spec = pltpu.PrefetchScalarGridSpec(
        num_scalar_prefetch=1,
        grid=(nq, n_dirs, nc // seg_rows),
        in_specs=[
            pl.BlockSpec((c, seg_rows, LANES), lambda q, r, s, ln: (0, s, q)),
            pl.BlockSpec((1, 1, 2, k, two_p), lambda q, r, s, ln: (q, r, 0, 0, 0)),
            pl.BlockSpec((1, 1) + a_rows.shape[2:], lambda q, r, s, ln: (q, r, 0, 0)),
        ],
        out_specs=pl.BlockSpec((seg_rows, n), lambda q, r, s, ln: (s, q * n_dirs + r)),
        scratch_shapes=[pltpu.VMEM((k, n), BF16), pltpu.VMEM((seg_rows, n), F32),
                        pltpu.VMEM((seg_rows, n), F32)],
    )
    return pl.pallas_call(
        functools.partial(_ssm_states_kernel, ssm_h=ssm_h),
        grid_spec=grid_spec,
        out_shape=jax.ShapeDtypeStruct((nc, nq * n_dirs * n), BF16),
        compiler_params=_cparams(("arbitrary", "arbitrary", "arbitrary")),
        name="ssm_states",
    )(seg_seq_rows, u_tm, w_in_rows, a_rows)


def _ssm_output_kernel(u_ref, s_ref, lag_ref, wcol_ref, y_ref, wintra_sc, wout_sc, *, ssm_h):
    n_t_out = y_ref.shape[0]

    @pl.when(pl.program_id(2) == 0)
    def _():
        t_out0 = pl.program_id(1) * n_t_out
        for t_in in range(SSM_CHUNK):
            for j in range(n_t_out):
                wintra_sc[t_in * LANES:(t_in + 1) * LANES, j * LANES:(j + 1) * LANES] = (
                    lag_ref[0, t_out0 + j - t_in + SSM_CHUNK - 1])
        n_planes, p, n_cols = wcol_ref.shape[1:]
        col_group = _group_of(lax.broadcasted_iota(jnp.int32, (p, n_cols), 1), ssm_h)
        for r in range(n_planes):
            cols = wcol_ref[0, r]
            for k in range(LANES // ssm_h):
                row = (r * (LANES // ssm_h) + k) * p
                wout_sc[row:row + p, :] = jnp.where(col_group == k, cols, jnp.zeros_like(cols))

    ucat = jnp.concatenate([u_ref[t] for t in range(SSM_CHUNK)], axis=1)
    y = jnp.dot(ucat, wintra_sc[...], preferred_element_type=F32)
    y += jnp.dot(s_ref[...], wout_sc[...], preferred_element_type=F32)
    for t in range(n_t_out):
        y_ref[t] = y[:, t * LANES:(t + 1) * LANES].astype(y_ref.dtype)


def _ssm_output(u_tm, s, lag_blocks, w_out_cols, ssm_h):
    c, nc, d_ssm = u_tm.shape
    nq, n_planes, p, k = w_out_cols.shape
    n_half = 2
    n = k // n_half
    s_width = n_planes * (LANES // ssm_h) * p
    rb = min(SSM_ROW_BLOCK, nc)
    return pl.pallas_call(
        functools.partial(_ssm_output_kernel, ssm_h=ssm_h),
        grid=(nq, n_half, nc // rb),
        in_specs=[
            pl.BlockSpec((c, rb, LANES), lambda q, h, i: (0, i, q)),
            pl.BlockSpec((rb, s_width), lambda q, h, i: (i, q)),
            pl.BlockSpec((1,) + lag_blocks.shape[1:], lambda q, h, i: (q, 0, 0, 0)),
            pl.BlockSpec((1, n_planes, p, n), lambda q, h, i: (q, 0, 0, h)),
        ],
        out_specs=pl.BlockSpec((c // n_half, rb, LANES), lambda q, h, i: (h, i, q)),
        out_shape=jax.ShapeDtypeStruct((c, nc, d_ssm), BF16),
        scratch_shapes=[pltpu.VMEM((k, n), BF16), pltpu.VMEM((s_width, n), BF16)],
        compiler_params=_cparams(("arbitrary", "arbitrary", "arbitrary")),
        name="ssm_output",
    )(u_tm, s, lag_blocks, w_out_cols)


def _attn_kernel(q_ref, kt_ref, v_ref, o_ref, m_sc, l_sc, acc_sc, *, n_kv_tiles):
    tq = q_ref.shape[0]
    n_slices = q_ref.shape[1] // LANES
    q = jnp.concatenate([q_ref[:, j * LANES:(j + 1) * LANES] for j in range(n_slices)], axis=0)
    m_sc[...] = jnp.full(m_sc.shape, -jnp.inf, F32)
    l_sc[...] = jnp.zeros(l_sc.shape, F32)
    acc_sc[...] = jnp.zeros(acc_sc.shape, F32)
    step_tiles = min(KV_STEP_TILES, n_kv_tiles)
    kv_step_len = step_tiles * KV_TILE
    n_rep = kv_step_len // LANES
    lane_head = lax.broadcasted_iota(jnp.int32, acc_sc.shape, 1) // HEAD_DIM
    key_head = lax.broadcasted_iota(jnp.int32, (kt_ref.shape[1], kv_step_len), 0) // HEAD_DIM

    def kv_step(kb, carry):
        start = pl.multiple_of(kb * kv_step_len, kv_step_len)
        kt = jnp.concatenate([kt_ref[kb * step_tiles + i] for i in range(step_tiles)], axis=1)
        v = v_ref[pl.ds(start, kv_step_len), :]
        acc = acc_sc[...]
        for g in range(N_KV_HEADS):
            ktg = jnp.where(key_head == g, kt, jnp.zeros_like(kt))
            s = jnp.dot(q, ktg, preferred_element_type=F32)
            m_old = m_sc[g]
            m_new = jnp.maximum(m_old, jnp.max(s, axis=-1, keepdims=True))
            alpha = jnp.exp2(m_old - m_new)
            p = jnp.exp2(s - jnp.concatenate([m_new] * n_rep, axis=1))
            l_part = p[:, 0:LANES]
            for c in range(1, n_rep):
                l_part = l_part + p[:, c * LANES:(c + 1) * LANES]
            l_sc[g] = alpha * l_sc[g] + l_part
            m_sc[g] = m_new
            pv = jnp.dot(p.astype(BF16), v, preferred_element_type=F32)
            acc = jnp.where(lane_head == g, alpha * acc + pv, acc)
        acc_sc[...] = acc
        return carry

    lax.fori_loop(0, n_kv_tiles // step_tiles, kv_step, 0)
    inv = [1.0 / jnp.sum(l_sc[g], axis=-1, keepdims=True) for g in range(N_KV_HEADS)]
    out = acc_sc[...] * jnp.where(lane_head == 0, inv[0], inv[1])
    for j in range(n_slices):
        o_ref[:, j * LANES:(j + 1) * LANES] = out[j * tq:(j + 1) * tq].astype(o_ref.dtype)


def _attention(q, kt, v, first_token, n_seqs, seq_len):
    d_attn = q.shape[1]
    d_kv = v.shape[1]
    tq = min(Q_TILE, seq_len)
    q_tiles = seq_len // tq
    kv_tiles = seq_len // KV_TILE
    assert first_token % seq_len == 0
    seq0 = first_token // seq_len
    rows = (d_attn // LANES) * tq
    return pl.pallas_call(
        functools.partial(_attn_kernel, n_kv_tiles=kv_tiles),
        grid=(n_seqs, q_tiles),
        in_specs=[
            pl.BlockSpec((tq, d_attn), lambda i, j: ((seq0 + i) * q_tiles + j, 0)),
            pl.BlockSpec((kv_tiles, d_kv, KV_TILE), lambda i, j: (seq0 + i, 0, 0)),
            pl.BlockSpec((seq_len, d_kv), lambda i, j: (seq0 + i, 0)),
        ],
        out_specs=pl.BlockSpec((tq, d_attn), lambda i, j: (i * q_tiles + j, 0)),
        out_shape=jax.ShapeDtypeStruct((n_seqs * seq_len, d_attn), BF16),
        scratch_shapes=[pltpu.VMEM((N_KV_HEADS, rows, LANES), F32),
                        pltpu.VMEM((N_KV_HEADS, rows, LANES), F32),
                        pltpu.VMEM((rows, LANES), F32)],
        compiler_params=_cparams(("arbitrary", "arbitrary")),
        name="attention",
    )(q, kt, v)


def _mix_out_kernel(seq_ref, x_ref, y_ref, o1_ref, o2_ref, gate_ref, mod_ref, wglu_ref, bglu_ref,
                    wbs_ref, wba_ref, wout_ref, g2_ref, wr_ref, br_ref, ltri_ref,
                    x1_ref, h2_ref, route_ref, count_ref, cnt_sc,
                    *, n_groups, per_group, blocks_first):
    del seq_ref
    mod = mod_ref[0]
    y = jnp.concatenate([y_ref[t] for t in range(SSM_CHUNK)], axis=0).astype(F32)
    y = 0.5 * y * (1.0 + jnp.tanh(math.sqrt(2.0 / math.pi) * (y + 0.044715 * (y * y * y))))
    glu = jnp.dot(y.astype(BF16), wglu_ref[...], preferred_element_type=F32) + bglu_ref[...]
    ys = (y * _sigmoid(glu)).astype(BF16)
    d = x_ref.shape[1]
    gate = gate_ref[...].astype(F32)
    o = jnp.where(pl.program_id(0) < blocks_first, o1_ref[...], o2_ref[...])
    merged = gate[:, :d] * jnp.dot(ys, wbs_ref[...], preferred_element_type=F32)
    merged += gate[:, d:] * jnp.dot(o, wba_ref[...], preferred_element_type=F32)
    x1 = x_ref[...] + mod[2:3] * jnp.dot(merged.astype(BF16), wout_ref[...],
                                         preferred_element_type=F32)
    x1_ref[...] = x1
    xn = x1 * lax.rsqrt(jnp.mean(x1 * x1, axis=-1, keepdims=True) + EPS) * g2_ref[...]
    h2 = (xn * (1.0 + mod[4:5]) + mod[3:4]).astype(BF16)
    h2_ref[...] = h2

    n_exp = n_groups * per_group
    logits = jnp.dot(h2, wr_ref[...], preferred_element_type=F32) + br_ref[...]
    lane = lax.broadcasted_iota(jnp.int32, logits.shape, 1).astype(F32)
    neg = jnp.float32(-jnp.inf)
    big = jnp.float32(1 << 20)
    lc = jnp.where(lane >= n_exp, jnp.where(lane < n_exp + n_groups, logits, neg), neg)
    c_max = jnp.max(lc, axis=-1, keepdims=True)
    g_idx = jnp.min(jnp.where(lc == c_max, lane - n_exp, big), axis=-1, keepdims=True)
    pc_top = 1.0 / jnp.sum(jnp.exp(lc - c_max), axis=-1, keepdims=True)
    lf = jnp.where(lane >= g_idx * per_group,
                   jnp.where(lane < (g_idx + 1.0) * per_group, logits, neg), neg)
    f1 = jnp.max(lf, axis=-1, keepdims=True)
    e1 = jnp.min(jnp.where(lf == f1, lane, big), axis=-1, keepdims=True)
    lf2 = jnp.where(lane == e1, neg, lf)
    f2 = jnp.max(lf2, axis=-1, keepdims=True)
    e2 = jnp.min(jnp.where(lf2 == f2, lane, big), axis=-1, keepdims=True)
    r = jnp.exp(f2 - f1)
    w1 = pc_top / (1.0 + r)
    w2 = pc_top * r / (1.0 + r)

    @pl.when(pl.program_id(0) == 0)
    def _():
        cnt_sc[...] = jnp.zeros(cnt_sc.shape, F32)

    hot = jnp.where(lane == e1, 1.0, 0.0) + jnp.where(lane == e2, 1.0, 0.0)
    before = jnp.dot(ltri_ref[...], hot.astype(BF16), preferred_element_type=F32) + cnt_sc[0:1, :]
    r1 = jnp.sum(jnp.where(lane == e1, before, 0.0), axis=-1, keepdims=True)
    r2 = jnp.sum(jnp.where(lane == e2, before, 0.0), axis=-1, keepdims=True)
    cnt_sc[...] = cnt_sc[...] + jnp.sum(hot, axis=0, keepdims=True)
    count_ref[...] = cnt_sc[...]

    route = jnp.zeros(logits.shape, F32)
    for col, val in enumerate((e1, e2, w1, w2, r1, r2)):
        route = jnp.where(lane == col, val, route)
    route_ref[...] = route


def _mix_out(x, y_tm, o_first, o_second, gates, mod, w_glu, b_glu, w_bs, w_ba, w_out, g2,
             w_router, b_router, blk_seq, n_groups, per_group):
    t, d = x.shape
    tb = TOKEN_BLOCK
    blocks_first = o_first.shape[0] // tb
    row = lambda i, s: (i, 0)
    const = lambda i, s: (0, 0)
    grid_spec = pltpu.PrefetchScalarGridSpec(
        num_scalar_prefetch=1,
        grid=(t // tb,),
        in_specs=[
            pl.BlockSpec((tb, d), row),
            pl.BlockSpec((SSM_CHUNK, CHUNKS_PER_BLOCK, y_tm.shape[2]), lambda i, s: (0, i, 0)),
            pl.BlockSpec((tb, o_first.shape[1]), lambda i, s: (jnp.minimum(i, blocks_first - 1), 0)),
            pl.BlockSpec((tb, o_second.shape[1]), lambda i, s: (jnp.maximum(i - blocks_first, 0), 0)),
            pl.BlockSpec((tb, gates.shape[1]), row),
            pl.BlockSpec((1,) + mod.shape[1:], lambda i, s: (s[i], 0, 0)),
            pl.BlockSpec(w_glu.shape, const),
            pl.BlockSpec(b_glu.shape, const),
            pl.BlockSpec(w_bs.shape, const),
            pl.BlockSpec(w_ba.shape, const),
            pl.BlockSpec(w_out.shape, const),
            pl.BlockSpec(g2.shape, const),
            pl.BlockSpec(w_router.shape, const),
            pl.BlockSpec(b_router.shape, const),
            pl.BlockSpec((tb, tb), const),
        ],
        out_specs=[
            pl.BlockSpec((tb, d), row),
            pl.BlockSpec((tb, d), row),
            pl.BlockSpec((tb, LANES), row),
            pl.BlockSpec((SUBLANES, LANES), const),
        ],
        scratch_shapes=[pltpu.VMEM((SUBLANES, LANES), F32)],
    )
    strictly_lower = jnp.asarray(np.tril(np.ones((tb, tb), np.float32), -1), BF16)
    return pl.pallas_call(
        functools.partial(_mix_out_kernel, n_groups=n_groups, per_group=per_group,
                          blocks_first=blocks_first),
        grid_spec=grid_spec,
        out_shape=[
            jax.ShapeDtypeStruct((t, d), F32),
            jax.ShapeDtypeStruct((t, d), BF16),
            jax.ShapeDtypeStruct((t, LANES), F32),
            jax.ShapeDtypeStruct((SUBLANES, LANES), F32),
        ],
        compiler_params=_cparams(("arbitrary",)),
        name="mix_out",
    )(blk_seq, x, y_tm, o_first, o_second, gates, mod, w_glu, b_glu, w_bs, w_ba, w_out, g2,
      w_router, b_router, strictly_lower)


def _expert_kernel(te_ref, nt_ref, xs_ref, wg_ref, wu_ref, wd_ref, ys_ref):
    del te_ref
    live = pl.program_id(0) < nt_ref[0]

    @pl.when(live)
    def _():
        x = xs_ref[...]
        hg = jnp.dot(x, wg_ref[0].astype(BF16), preferred_element_type=F32)
        hu = jnp.dot(x, wu_ref[0].astype(BF16), preferred_element_type=F32)
        act = (hg * _sigmoid(hg) * hu).astype(BF16)
        ys_ref[...] = jnp.dot(act, wd_ref[0].astype(BF16),
                              preferred_element_type=F32).astype(ys_ref.dtype)

    @pl.when(jnp.logical_not(live))
    def _():
        ys_ref[...] = jnp.zeros(ys_ref.shape, ys_ref.dtype)


def _experts(xs, tile_expert, n_live, w_gate, w_up, w_down):
    p, d = xs.shape
    tm = EXPERT_TILE
    f = w_gate.shape[2]
    grid_spec = pltpu.PrefetchScalarGridSpec(
        num_scalar_prefetch=2,
        grid=(p // tm,),
        in_specs=[
            pl.BlockSpec((tm, d), lambda i, te, nt: (i, 0)),
            pl.BlockSpec((1, d, f), lambda i, te, nt: (te[i], 0, 0)),
            pl.BlockSpec((1, d, f), lambda i, te, nt: (te[i], 0, 0)),
            pl.BlockSpec((1, f, d), lambda i, te, nt: (te[i], 0, 0)),
        ],
        out_specs=pl.BlockSpec((tm, d), lambda i, te, nt: (i, 0)),
    )
    return pl.pallas_call(
        _expert_kernel,
        grid_spec=grid_spec,
        out_shape=jax.ShapeDtypeStruct((p, d), BF16),
        compiler_params=_cparams(("arbitrary",)),
        name="experts",
    )(tile_expert, n_live, xs, w_gate, w_up, w_down)


def _combine_kernel(seq_ref, x_ref, y1_ref, y2_ref, route_ref, mod_ref, o_ref):
    del seq_ref
    route = route_ref[...]
    w1 = route[:, 2:3]
    w2 = route[:, 3:4]
    moe = w1 * y1_ref[...].astype(F32) + w2 * y2_ref[...].astype(F32)
    o_ref[...] = x_ref[...] + mod_ref[0][5:6] * moe


def _combine(x1, y1, y2, route, mod, blk_seq):
    t, d = x1.shape
    tb = TOKEN_BLOCK
    row = lambda i, s: (i, 0)
    grid_spec = pltpu.PrefetchScalarGridSpec(
        num_scalar_prefetch=1,
        grid=(t // tb,),
        in_specs=[
            pl.BlockSpec((tb, d), row),
            pl.BlockSpec((tb, d), row),
            pl.BlockSpec((tb, d), row),
            pl.BlockSpec((tb, LANES), row),
            pl.BlockSpec((1,) + mod.shape[1:], lambda i, s: (s[i], 0, 0)),
        ],
        out_specs=pl.BlockSpec((tb, d), row),
    )
    return pl.pallas_call(
        _combine_kernel,
        grid_spec=grid_spec,
        out_shape=jax.ShapeDtypeStruct((t, d), F32),
        compiler_params=_cparams(("arbitrary",)),
        name="moe_combine",
    )(blk_seq, x1, y1, y2, route, mod)


def _dispatch_plan(route, counts, n_experts):
    t = route.shape[0]
    tm = EXPERT_TILE
    eid = route[:, :FINE_TOP_K].astype(jnp.int32).reshape(-1)
    rank = route[:, 2 * FINE_TOP_K:3 * FINE_TOP_K].astype(jnp.int32).reshape(-1)
    counts = counts[0, :n_experts].astype(jnp.int32)
    padded = ((counts + tm - 1) // tm) * tm
    pad_end = jnp.cumsum(padded)
    pad_off = pad_end - padded
    dest = pad_off[eid] + rank
    n_rows = FINE_TOP_K * t + n_experts * tm
    n_tiles = n_rows // tm
    src_tok = (jnp.arange(n_rows, dtype=jnp.int32) % t).at[dest].set(
        jnp.arange(FINE_TOP_K * t, dtype=jnp.int32) // FINE_TOP_K,
        unique_indices=True, mode='promise_in_bounds')
    tile_start = jnp.arange(n_tiles, dtype=jnp.int32) * tm
    tile_expert = jnp.minimum(jnp.sum((pad_end[None, :] <= tile_start[:, None]).astype(jnp.int32), axis=1),
                              n_experts - 1)
    n_live = (pad_end[-1] // tm).astype(jnp.int32).reshape(1)
    last_live = jnp.maximum(n_live[0] - 1, 0)
    tile_expert = jnp.where(tile_start // tm < n_live[0], tile_expert, tile_expert[last_live])
    return dest.reshape(t, FINE_TOP_K), src_tok, tile_expert, n_live


def _rope_tables(max_len):
    rows = max_len // GRID_W
    row = jnp.repeat(jnp.arange(rows, dtype=F32), GRID_W)
    col = jnp.tile(jnp.arange(GRID_W, dtype=F32), rows)
    half = HEAD_DIM // 2
    inv_freq = 1.0 / (ROPE_THETA ** (jnp.arange(0, half, 2, dtype=F32) / half))
    ang_r = row[:, None] * inv_freq[None, :]
    ang_c = col[:, None] * inv_freq[None, :]
    ang = jnp.concatenate([ang_r, ang_r, ang_c, ang_c], axis=-1)
    sign = jnp.where((jnp.arange(HEAD_DIM) % (HEAD_DIM // 2)) < HEAD_DIM // 4, -1.0, 1.0)
    reps = LANES // HEAD_DIM
    return jnp.tile(jnp.cos(ang), (1, reps)), jnp.tile(jnp.sin(ang) * sign[None, :], (1, reps))


def _to_block_order(a):
    n, d = a.shape
    return a.reshape(n // TOKEN_BLOCK, CHUNKS_PER_BLOCK, SSM_CHUNK, d).transpose(0, 2, 1, 3).reshape(n, d)


def _from_block_order(a):
    n, d = a.shape
    return a.reshape(n // TOKEN_BLOCK, SSM_CHUNK, CHUNKS_PER_BLOCK, d).transpose(0, 2, 1, 3).reshape(n, d)


def kernel(x_prompt, x_sample, c_prompt, c_sample, w_ada, b_ada, g_norm1, g_norm2, w_in, ssm_a_re, ssm_a_im, ssm_log_dt, ssm_b_re, ssm_b_im, ssm_c_re, ssm_c_im, ssm_d, w_glu, b_glu, g_q, g_k, w_branch_ssm, w_branch_attn, w_out, w_coarse, b_coarse, w_fine, b_fine, w_expert_gate, w_expert_up, w_expert_down):
    b1, l1, d = x_prompt.shape
    b2, l2, _ = x_sample.shape
    depth = w_in.shape[0]
    d_ssm = w_glu.shape[-1]
    d_attn = w_branch_attn.shape[1]
    d_kv = N_KV_HEADS * HEAD_DIM
    n_heads = d_attn // HEAD_DIM
    n_groups = w_coarse.shape[-1]
    n_experts = w_fine.shape[-1]
    per_group = n_experts // n_groups
    n_ssm_groups = ssm_b_re.shape[2]
    ssm_h = ssm_b_re.shape[-1]
    tb = TOKEN_BLOCK
    assert l1 % tb == 0 and l2 % tb == 0 and l1 % KV_TILE == 0 and l2 % KV_TILE == 0
    assert d_attn == N_KV_HEADS * (n_heads // N_KV_HEADS) * HEAD_DIM and 2 * HEAD_DIM == LANES
    assert n_ssm_groups * ssm_h == d_ssm and d_ssm % LANES == 0 and LANES % ssm_h == 0
    assert 2 * ssm_a_re.shape[-1] == LANES
    assert n_experts + n_groups <= LANES

    seq_lens = [l1] * b1 + [l2] * b2
    t = sum(seq_lens)
    x = jnp.concatenate([x_prompt.reshape(b1 * l1, d), x_sample.reshape(b2 * l2, d)], axis=0)
    x = _to_block_order(x)
    c = jnp.concatenate([c_prompt, c_sample], axis=0)
    blk_seq = np.concatenate([np.full(n // tb, s, np.int32) for s, n in enumerate(seq_lens)])
    blk_pos = np.concatenate([np.arange(n // tb, dtype=np.int32) for n in seq_lens])
    blk_seq, blk_pos = jnp.asarray(blk_seq), jnp.asarray(blk_pos)
    cos_t, sin_t = (_to_block_order(tab) for tab in _rope_tables(max(l1, l2)))

    seg_rows = max(seq_lens) // SSM_CHUNK
    seg_seq_rows = []
    filled = 0
    for n in seq_lens:
        rows = n // SSM_CHUNK
        assert rows % SUBLANES == 0 and seg_rows % rows == 0
        if filled == 0:
            seg_seq_rows.append(rows)
        assert seg_seq_rows[-1] == rows
        filled = (filled + rows) % seg_rows
    assert filled == 0
    seg_seq_rows = jnp.asarray(np.array(seg_seq_rows, np.int32))

    rep = n_heads // N_KV_HEADS
    slot_head = np.array([g * rep + j for j in range(rep) for g in range(N_KV_HEADS)])
    col_perm = (slot_head[:, None] * HEAD_DIM + np.arange(HEAD_DIM)[None, :]).reshape(-1)
    in_perm = np.concatenate([np.arange(d_ssm), d_ssm + col_perm, np.arange(d_ssm + d_attn, w_in.shape[-1])])

    mod_all = _ada_modulation(c, w_ada, b_ada)
    gq_t = jnp.tile(g_q.astype(F32), (1, LANES // HEAD_DIM))
    gk_t = jnp.tile(g_k.astype(F32), (1, LANES // HEAD_DIM))
    head_mean = jnp.asarray(np.kron(np.eye(LANES // HEAD_DIM), np.full((HEAD_DIM, HEAD_DIM), 1.0 / HEAD_DIM)), BF16)
    w_router = jnp.concatenate([w_fine, w_coarse,
                                jnp.zeros((depth, d, LANES - n_experts - n_groups), F32)], -1).astype(BF16)
    b_router = jnp.concatenate([b_fine, b_coarse,
                                jnp.zeros((depth, LANES - n_experts - n_groups), F32)], -1)

    for i in range(depth):
        mod = mod_all[i]
        u_tm, q, kt, v, gates = _mix_in(
            x, mod, g_norm1[i][None], w_in[i][:, in_perm].astype(BF16), cos_t, sin_t,
            gq_t[i][None], gk_t[i][None], head_mean, blk_seq, blk_pos, (d_ssm, d_attn, d_kv))

        w_in_rows, lag_blocks, w_out_cols, ssm_a = _ssm_weights(
            ssm_a_re[i], ssm_a_im[i], ssm_log_dt[i], ssm_b_re[i], ssm_b_im[i],
            ssm_c_re[i], ssm_c_im[i], ssm_d[i])
        states = _ssm_states(u_tm, w_in_rows, ssm_a, seg_rows, seg_seq_rows, ssm_h)
        y_tm = _ssm_output(u_tm, states, lag_blocks, w_out_cols, ssm_h)

        o_first = _attention(q, kt, v, 0, b1, l1)
        o_second = _attention(q, kt, v, b1 * l1, b2, l2)

        w_ba = w_branch_attn[i][col_perm].astype(BF16)
        x1, h2, route, counts = _mix_out(
            x, y_tm, o_first, o_second, gates, mod, w_glu[i].astype(BF16), b_glu[i][None],
            w_branch_ssm[i].astype(BF16), w_ba, w_out[i].astype(BF16), g_norm2[i][None],
            w_router[i], b_router[i][None], blk_seq, n_groups, per_group)

        dest, src_tok, tile_expert, n_live = _dispatch_plan(route, counts, n_experts)
        xs = h2.at[src_tok].get(mode='promise_in_bounds')
        ys = _experts(xs, tile_expert, n_live, w_expert_gate[i], w_expert_up[i], w_expert_down[i])
        y1 = ys.at[dest[:, 0]].get(mode='promise_in_bounds')
        y2 = ys.at[dest[:, 1]].get(mode='promise_in_bounds')
        x = _combine(x1, y1, y2, route, mod, blk_seq)

    x = _from_block_order(x)
    y_prompt = x[:b1 * l1].reshape(b1, l1, d)
    y_sample = x[b1 * l1:].reshape(b2, l2, d)
    return (y_prompt, y_sample)
```

```python
import functools
import math

import jax
import jax.numpy as jnp
import numpy as np
from jax import lax
from jax.experimental import pallas as pl
from jax.experimental.pallas import tpu as pltpu

F32 = jnp.float32
BF16 = jnp.bfloat16

EPS = 1e-6
GRID_W = 64
ROPE_THETA = 10000.0
HEAD_DIM = 64
N_KV_HEADS = 2
FINE_TOP_K = 2
LOG2E = 1.4426950408889634
LANES = 128
SUBLANES = 8
SSM_CHUNK = 16

TOKEN_BLOCK = 512
CHUNKS_PER_BLOCK = TOKEN_BLOCK // SSM_CHUNK
Q_TILE = 256
KV_TILE = TOKEN_BLOCK
KV_STEP_TILES = 4
EXPERT_TILE = 512
SSM_ROW_BLOCK = 512
ROW_DIGIT = 16
VMEM_LIMIT = 56 * 1024 * 1024


def _sigmoid(x):
    return 1.0 / (1.0 + jnp.exp(-x))


def _cparams(sem):
    return pltpu.CompilerParams(dimension_semantics=sem, vmem_limit_bytes=VMEM_LIMIT)


def _ada_kernel(c_ref, w_ref, b_ref, o_ref):
    c = c_ref[...]
    ca = c * _sigmoid(c)
    o_ref[0, 0] = jnp.dot(ca, w_ref[0], preferred_element_type=F32,
                          precision=lax.Precision.HIGHEST) + b_ref[0, 0]


def _ada_modulation(c, w_ada, b_ada):
    depth, d, _ = w_ada.shape
    nb = c.shape[0]
    n_mod = w_ada.shape[2] // d
    out = pl.pallas_call(
        _ada_kernel,
        grid=(depth, n_mod),
        in_specs=[
            pl.BlockSpec((nb, d), lambda l, j: (0, 0)),
            pl.BlockSpec((1, d, d), lambda l, j: (l, 0, j)),
            pl.BlockSpec((1, 1, 1, d), lambda l, j: (l, j, 0, 0)),
        ],
        out_specs=pl.BlockSpec((1, 1, nb, d), lambda l, j: (l, j, 0, 0)),
        out_shape=jax.ShapeDtypeStruct((depth, n_mod, nb, d), F32),
        compiler_params=_cparams(("arbitrary", "arbitrary")),
        name="ada_modulation",
    )(c, w_ada, b_ada.reshape(depth, n_mod, 1, d))
    return out.transpose(0, 2, 1, 3)


def _rope(x, cos, sin_signed, low_half):
    up = pltpu.roll(x, LANES - 16, axis=1)
    down = pltpu.roll(x, 16, axis=1)
    return x * cos + jnp.where(low_half, up, down) * sin_signed


def _mix_in_kernel(seq_ref, pos_ref, x_ref, mod_ref, g1_ref, w_ref, cos_ref, sin_ref,
                   gq_ref, gk_ref, hm_ref, u_ref, q_ref, kt_ref, v_ref, gate_ref,
                   *, d_ssm, d_attn, d_kv):
    del seq_ref, pos_ref
    x = x_ref[...]
    mod = mod_ref[0]
    xn = x * lax.rsqrt(jnp.mean(x * x, axis=-1, keepdims=True) + EPS) * g1_ref[...]
    h = (xn * (1.0 + mod[1:2]) + mod[0:1]).astype(BF16)

    def proj(lo, hi):
        return jnp.dot(h, w_ref[:, lo:hi], preferred_element_type=F32)

    u = proj(0, d_ssm).astype(BF16)
    n_c = u.shape[0] // SSM_CHUNK
    for t in range(SSM_CHUNK):
        u_ref[t] = u[t * n_c:(t + 1) * n_c]

    cos = cos_ref[...]
    sin_signed = sin_ref[...]
    lane = lax.broadcasted_iota(jnp.int32, cos.shape, 1)
    low_half = (lane % 32) < 16
    hm = hm_ref[...]

    def norm_rope(z, gain, scale):
        cols = []
        for j in range(z.shape[1] // LANES):
            zj = z[:, j * LANES:(j + 1) * LANES]
            msq = jnp.dot((zj * zj).astype(BF16), hm, preferred_element_type=F32)
            zn = zj * lax.rsqrt(msq + EPS) * gain
            cols.append(_rope(zn, cos, sin_signed, low_half) * scale)
        return jnp.concatenate(cols, axis=1) if len(cols) > 1 else cols[0]

    o = d_ssm
    q_ref[...] = norm_rope(proj(o, o + d_attn), gq_ref[...],
                           (HEAD_DIM ** -0.5) * LOG2E).astype(BF16)
    o += d_attn
    kt_ref[0] = norm_rope(proj(o, o + d_kv), gk_ref[...], 1.0).T.astype(BF16)
    o += d_kv
    v_ref[...] = proj(o, o + d_kv).astype(BF16)
    o += d_kv
    n_gate = gate_ref.shape[1]
    half = n_gate // 2
    gate_ref[:, :half] = _sigmoid(proj(o, o + half)).astype(BF16)
    gate_ref[:, half:] = _sigmoid(proj(o + half, o + n_gate)).astype(BF16)


def _mix_in(x, mod, g1, w_in, cos_t, sin_t, gq_t, gk_t, hm, blk_seq, blk_pos, dims):
    t, d = x.shape
    d_ssm, d_attn, d_kv = dims
    n_gate = w_in.shape[1] - d_ssm - d_attn - 2 * d_kv
    tb = TOKEN_BLOCK
    row = lambda i, s, p: (i, 0)
    const = lambda i, s, p: (0, 0)
    grid_spec = pltpu.PrefetchScalarGridSpec(
        num_scalar_prefetch=2,
        grid=(t // tb,),
        in_specs=[
            pl.BlockSpec((tb, d), row),
            pl.BlockSpec((1,) + mod.shape[1:], lambda i, s, p: (s[i], 0, 0)),
            pl.BlockSpec((1, d), const),
            pl.BlockSpec(w_in.shape, const),
            pl.BlockSpec((tb, LANES), lambda i, s, p: (p[i], 0)),
            pl.BlockSpec((tb, LANES), lambda i, s, p: (p[i], 0)),
            pl.BlockSpec((1, LANES), const),
            pl.BlockSpec((1, LANES), const),
            pl.BlockSpec((LANES, LANES), const),
        ],
        out_specs=[
            pl.BlockSpec((SSM_CHUNK, CHUNKS_PER_BLOCK, d_ssm), lambda i, s, p: (0, i, 0)),
            pl.BlockSpec((tb, d_attn), row),
            pl.BlockSpec((1, d_kv, tb), lambda i, s, p: (i, 0, 0)),
            pl.BlockSpec((tb, d_kv), row),
            pl.BlockSpec((tb, n_gate), row),
        ],
    )
    return pl.pallas_call(
        functools.partial(_mix_in_kernel, d_ssm=d_ssm, d_attn=d_attn, d_kv=d_kv),
        grid_spec=grid_spec,
        out_shape=[
            jax.ShapeDtypeStruct((SSM_CHUNK, t // SSM_CHUNK, d_ssm), BF16),
            jax.ShapeDtypeStruct((t, d_attn), BF16),
            jax.ShapeDtypeStruct((t // tb, d_kv, tb), BF16),
            jax.ShapeDtypeStruct((t, d_kv), BF16),
            jax.ShapeDtypeStruct((t, n_gate), BF16),
        ],
        compiler_params=_cparams(("arbitrary",)),
        name="mix_in",
    )(blk_seq, blk_pos, x, mod, g1, w_in, cos_t, sin_t, gq_t, gk_t, hm)


def _ssm_weights(a_re, a_im, log_dt, b_re, b_im, c_re, c_im, d_skip):
    n_dirs, g, p = a_re.shape
    hch = b_re.shape[-1]
    c = SSM_CHUNK
    nq = g * hch // LANES
    gq = g // nq
    lam = lax.complex(a_re.astype(F32), a_im.astype(F32))
    dt = jnp.exp(log_dt.astype(F32))[..., None]
    lam_dt = lam * dt
    coef = (jnp.exp(lam_dt) - 1.0) / lam
    steps = jnp.arange(c + 1, dtype=F32)
    pw = jnp.exp(lam_dt[:, None] * steps[None, :, None, None])
    bc = lax.complex(b_re.astype(F32), b_im.astype(F32)) * coef[..., None]
    cc = lax.complex(c_re.astype(F32), c_im.astype(F32))
    eye = jnp.eye(gq, dtype=F32)

    kern = jnp.einsum('rghp,rdgp,rgpk->rdghk', cc, pw[:, :c], bc).real
    dmat = jnp.eye(hch, dtype=F32)[None] * d_skip.astype(F32).reshape(g, hch)[:, :, None]
    lag0 = (kern[0, 0] + kern[1, 0] + dmat)[None]
    by_lag = jnp.concatenate([kern[1, 1:][::-1], lag0, kern[0, 1:]], 0)
    by_lag = by_lag.reshape(2 * c - 1, nq, gq, hch, hch)
    lag_blocks = jnp.einsum('dqkoh,gk->qdghko', by_lag, eye).reshape(nq, 2 * c - 1, LANES, LANES)

    win_f = pw[0, c - 1 - jnp.arange(c)][:, :, None, :] * bc[0].transpose(0, 2, 1)[None]
    win_b = pw[1, jnp.arange(c)][:, :, None, :] * bc[1].transpose(0, 2, 1)[None]

    def in_rows(w):
        parts = jnp.stack([w.real, w.imag], 0).reshape(2, c, nq, gq, hch, p)
        parts = parts.transpose(2, 0, 1, 3, 4, 5).reshape(nq, 2, c * gq * hch, p)
        return jnp.concatenate([parts, parts], -1)

    w_in_rows = jnp.stack([in_rows(win_f), in_rows(win_b)], 1)

    cout_f = cc[0][None] * pw[0, 1 + jnp.arange(c)][:, :, None, :]
    cout_b = cc[1][None] * pw[1, c - jnp.arange(c)][:, :, None, :]

    def out_cols(w):
        parts = jnp.stack([w.real, -w.imag], 0).reshape(2, c, nq, gq, hch, p)
        return parts.transpose(2, 0, 5, 1, 3, 4).reshape(nq, 2, p, c * gq * hch)

    w_out_cols = jnp.concatenate([out_cols(cout_f), out_cols(cout_b)], 1)

    a_chunk = pw[:, c]
    a_rows = jnp.stack([a_chunk.real, a_chunk.imag], 1).reshape(n_dirs, 2, nq, gq * p)
    a_rows = a_rows.transpose(2, 0, 1, 3)
    a_rows = jnp.concatenate([a_rows, jnp.zeros((nq, n_dirs, SUBLANES - 2, gq * p), F32)], 2)
    return w_in_rows.astype(BF16), lag_blocks.astype(BF16), w_out_cols.astype(BF16), a_rows


def _group_of(index, per_group):
    return (index // per_group) % (LANES // per_group)


def _ssm_states_kernel(len_ref, u_ref, win_ref, a_ref, s_ref, w_sc, x_sc, s_sc, *, ssm_h):
    n_rows, width = x_sc.shape
    half = width // 2

    @pl.when(pl.program_id(2) == 0)
    def _():
        shape = win_ref.shape[3:]
        row_group = _group_of(lax.broadcasted_iota(jnp.int32, shape, 0), ssm_h)
        lane_half = lax.broadcasted_iota(jnp.int32, shape, 1) // (LANES // 2)
        for a in range(2):
            rows = win_ref[0, 0, a]
            for j in range(half // LANES):
                col = a * half + j * LANES
                w_sc[:, col:col + LANES] = jnp.where(row_group == 2 * j + lane_half, rows,
                                                     jnp.zeros_like(rows))

    ucat = jnp.concatenate([u_ref[t] for t in range(SSM_CHUNK)], axis=1)
    x_sc[...] = jnp.dot(ucat, w_sc[...], preferred_element_type=F32)
    sub = SUBLANES
    tile = (sub, half)
    a = a_ref[0, 0]
    a_re = jnp.broadcast_to(a[0:1], tile)
    a_im = jnp.broadcast_to(a[1:2], tile)
    row_id = lax.broadcasted_iota(jnp.int32, tile, 0)
    seq_rows = len_ref[pl.program_id(2)]
    n_tiles = n_rows // sub

    def run(backward):
        def body(i, carry):
            s_re, s_im = carry
            tile_i = n_tiles - 1 - i if backward else i
            row0 = pl.multiple_of(tile_i * sub, sub)
            first = (row0 + sub) if backward else row0
            keep = jnp.where(lax.rem(first, seq_rows) == 0, 0.0, 1.0).astype(F32)
            s_re = s_re * keep
            s_im = s_im * keep
            x_re = x_sc[pl.ds(row0, sub), 0:half]
            x_im = x_sc[pl.ds(row0, sub), half:width]
            t_re = jnp.zeros(tile, F32)
            t_im = jnp.zeros(tile, F32)
            for k in range(sub):
                r = sub - 1 - k if backward else k
                t_re = jnp.where(row_id == r, s_re, t_re)
                t_im = jnp.where(row_id == r, s_im, t_im)
                u_re = jnp.broadcast_to(x_re[r:r + 1], tile)
                u_im = jnp.broadcast_to(x_im[r:r + 1], tile)
                s_re, s_im = a_re * s_re - a_im * s_im + u_re, a_re * s_im + a_im * s_re + u_im
            s_sc[pl.ds(row0, sub), 0:half] = t_re
            s_sc[pl.ds(row0, sub), half:width] = t_im
            return s_re, s_im
        zero = jnp.zeros(tile, F32)
        lax.fori_loop(0, n_tiles, body, (zero, zero))

    direction = pl.program_id(1)

    @pl.when(direction == 0)
    def _():
        run(False)

    @pl.when(direction == 1)
    def _():
        run(True)

    s_ref[...] = s_sc[...].astype(s_ref.dtype)


def _ssm_states(u_tm, w_in_rows, a_rows, seg_rows, seg_seq_rows, ssm_h):
    c, nc, d_ssm = u_tm.shape
    nq, n_dirs, _, k, two_p = w_in_rows.shape
    n = a_rows.shape[-1] * 2
    assert two_p == LANES and k == c * LANES
    grid_spec = pltpu.PrefetchScalarGridSpec(
        num_scalar_prefetch=1,
        grid=(nq, n_dirs, nc // seg_rows),
        in_specs=[
            pl.BlockSpec((c, seg_rows, LANES), lambda q, r, s, ln: (0, s, q)),
            pl.BlockSpec((1, 1, 2, k, two_p), lambda q, r, s, ln: (q, r, 0, 0, 0)),
            pl.BlockSpec((1, 1) + a_rows.shape[2:], lambda q, r, s, ln: (q, r, 0, 0)),
        ],
        out_specs=pl.BlockSpec((seg_rows, n), lambda q, r, s, ln: (s, q * n_dirs + r)),
        scratch_shapes=[pltpu.VMEM((k, n), BF16), pltpu.VMEM((seg_rows, n), F32),
                        pltpu.VMEM((seg_rows, n), F32)],
    )
    return pl.pallas_call(
        functools.partial(_ssm_states_kernel, ssm_h=ssm_h),
        grid_spec=grid_spec,
        out_shape=jax.ShapeDtypeStruct((nc, nq * n_dirs * n), BF16),
        compiler_params=_cparams(("arbitrary", "arbitrary", "arbitrary")),
        name="ssm_states",
    )(seg_seq_rows, u_tm, w_in_rows, a_rows)


def _ssm_output_kernel(u_ref, s_ref, lag_ref, wcol_ref, y_ref, wintra_sc, wout_sc, *, ssm_h):
    n_t_out = y_ref.shape[0]

    @pl.when(pl.program_id(2) == 0)
    def _():
        t_out0 = pl.program_id(1) * n_t_out
        for t_in in range(SSM_CHUNK):
            for j in range(n_t_out):
                wintra_sc[t_in * LANES:(t_in + 1) * LANES, j * LANES:(j + 1) * LANES] = (
                    lag_ref[0, t_out0 + j - t_in + SSM_CHUNK - 1])
        n_planes, p, n_cols = wcol_ref.shape[1:]
        col_group = _group_of(lax.broadcasted_iota(jnp.int32, (p, n_cols), 1), ssm_h)
        for r in range(n_planes):
            cols = wcol_ref[0, r]
            for k in range(LANES // ssm_h):
                row = (r * (LANES // ssm_h) + k) * p
                wout_sc[row:row + p, :] = jnp.where(col_group == k, cols, jnp.zeros_like(cols))

    ucat = jnp.concatenate([u_ref[t] for t in range(SSM_CHUNK)], axis=1)
    y = jnp.dot(ucat, wintra_sc[...], preferred_element_type=F32)
    y += jnp.dot(s_ref[...], wout_sc[...], preferred_element_type=F32)
    for t in range(n_t_out):
        y_ref[t] = y[:, t * LANES:(t + 1) * LANES].astype(y_ref.dtype)


def _ssm_output(u_tm, s, lag_blocks, w_out_cols, ssm_h):
    c, nc, d_ssm = u_tm.shape
    nq, n_planes, p, k = w_out_cols.shape
    n_half = 2
    n = k // n_half
    s_width = n_planes * (LANES // ssm_h) * p
    rb = min(SSM_ROW_BLOCK, nc)
    return pl.pallas_call(
        functools.partial(_ssm_output_kernel, ssm_h=ssm_h),
        grid=(nq, n_half, nc // rb),
        in_specs=[
            pl.BlockSpec((c, rb, LANES), lambda q, h, i: (0, i, q)),
            pl.BlockSpec((rb, s_width), lambda q, h, i: (i, q)),
            pl.BlockSpec((1,) + lag_blocks.shape[1:], lambda q, h, i: (q, 0, 0, 0)),
            pl.BlockSpec((1, n_planes, p, n), lambda q, h, i: (q, 0, 0, h)),
        ],
        out_specs=pl.BlockSpec((c // n_half, rb, LANES), lambda q, h, i: (h, i, q)),
        out_shape=jax.ShapeDtypeStruct((c, nc, d_ssm), BF16),
        scratch_shapes=[pltpu.VMEM((k, n), BF16), pltpu.VMEM((s_width, n), BF16)],
        compiler_params=_cparams(("arbitrary", "arbitrary", "arbitrary")),
        name="ssm_output",
    )(u_tm, s, lag_blocks, w_out_cols)


def _attn_kernel(q_ref, kt_ref, v_ref, o_ref, m_sc, l_sc, acc_sc, *, n_kv_tiles):
    tq = q_ref.shape[0]
    n_slices = q_ref.shape[1] // LANES
    q = jnp.concatenate([q_ref[:, j * LANES:(j + 1) * LANES] for j in range(n_slices)], axis=0)
    m_sc[...] = jnp.full(m_sc.shape, -jnp.inf, F32)
    l_sc[...] = jnp.zeros(l_sc.shape, F32)
    acc_sc[...] = jnp.zeros(acc_sc.shape, F32)
    step_tiles = min(KV_STEP_TILES, n_kv_tiles)
    kv_step_len = step_tiles * KV_TILE
    n_rep = kv_step_len // LANES
    lane_head = lax.broadcasted_iota(jnp.int32, acc_sc.shape, 1) // HEAD_DIM
    key_head = lax.broadcasted_iota(jnp.int32, (kt_ref.shape[1], kv_step_len), 0) // HEAD_DIM

    def kv_step(kb, carry):
        start = pl.multiple_of(kb * kv_step_len, kv_step_len)
        kt = jnp.concatenate([kt_ref[kb * step_tiles + i] for i in range(step_tiles)], axis=1)
        v = v_ref[pl.ds(start, kv_step_len), :]
        acc = acc_sc[...]
        for g in range(N_KV_HEADS):
            ktg = jnp.where(key_head == g, kt, jnp.zeros_like(kt))
            s = jnp.dot(q, ktg, preferred_element_type=F32)
            m_old = m_sc[g]
            m_new = jnp.maximum(m_old, jnp.max(s, axis=-1, keepdims=True))
            alpha = jnp.exp2(m_old - m_new)
            p = jnp.exp2(s - jnp.concatenate([m_new] * n_rep, axis=1))
            l_part = p[:, 0:LANES]
            for c in range(1, n_rep):
                l_part = l_part + p[:, c * LANES:(c + 1) * LANES]
            l_sc[g] = alpha * l_sc[g] + l_part
            m_sc[g] = m_new
            pv = jnp.dot(p.astype(BF16), v, preferred_element_type=F32)
            acc = jnp.where(lane_head == g, alpha * acc + pv, acc)
        acc_sc[...] = acc
        return carry

    lax.fori_loop(0, n_kv_tiles // step_tiles, kv_step, 0)
    inv = [1.0 / jnp.sum(l_sc[g], axis=-1, keepdims=True) for g in range(N_KV_HEADS)]
    out = acc_sc[...] * jnp.where(lane_head == 0, inv[0], inv[1])
    for j in range(n_slices):
        o_ref[:, j * LANES:(j + 1) * LANES] = out[j * tq:(j + 1) * tq].astype(o_ref.dtype)


def _attention(q, kt, v, first_token, n_seqs, seq_len):
    d_attn = q.shape[1]
    d_kv = v.shape[1]
    tq = min(Q_TILE, seq_len)
    q_tiles = seq_len // tq
    kv_tiles = seq_len // KV_TILE
    assert first_token % seq_len == 0
    seq0 = first_token // seq_len
    rows = (d_attn // LANES) * tq
    return pl.pallas_call(
        functools.partial(_attn_kernel, n_kv_tiles=kv_tiles),
        grid=(n_seqs, q_tiles),
        in_specs=[
            pl.BlockSpec((tq, d_attn), lambda i, j: ((seq0 + i) * q_tiles + j, 0)),
            pl.BlockSpec((kv_tiles, d_kv, KV_TILE), lambda i, j: (seq0 + i, 0, 0)),
            pl.BlockSpec((seq_len, d_kv), lambda i, j: (seq0 + i, 0)),
        ],
        out_specs=pl.BlockSpec((tq, d_attn), lambda i, j: (i * q_tiles + j, 0)),
        out_shape=jax.ShapeDtypeStruct((n_seqs * seq_len, d_attn), BF16),
        scratch_shapes=[pltpu.VMEM((N_KV_HEADS, rows, LANES), F32),
                        pltpu.VMEM((N_KV_HEADS, rows, LANES), F32),
                        pltpu.VMEM((rows, LANES), F32)],
        compiler_params=_cparams(("arbitrary", "arbitrary")),
        name="attention",
    )(q, kt, v)


def _mix_out_kernel(seq_ref, x_ref, y_ref, o1_ref, o2_ref, gate_ref, mod_ref, wglu_ref, bglu_ref,
                    wbs_ref, wba_ref, wout_ref, g2_ref, wr_ref, br_ref, ltri_ref, upper_ref,
                    rowid_ref, x1_ref, h2_ref, route_ref, count_ref, order_ref,
                    *, n_groups, per_group, blocks_first):
    del seq_ref
    mod = mod_ref[0]
    y = jnp.concatenate([y_ref[t] for t in range(SSM_CHUNK)], axis=0).astype(F32)
    y = 0.5 * y * (1.0 + jnp.tanh(math.sqrt(2.0 / math.pi) * (y + 0.044715 * (y * y * y))))
    glu = jnp.dot(y.astype(BF16), wglu_ref[...], preferred_element_type=F32) + bglu_ref[...]
    ys = (y * _sigmoid(glu)).astype(BF16)
    d = x_ref.shape[1]
    gate = gate_ref[...].astype(F32)
    o = jnp.where(pl.program_id(0) < blocks_first, o1_ref[...], o2_ref[...])
    merged = gate[:, :d] * jnp.dot(ys, wbs_ref[...], preferred_element_type=F32)
    merged += gate[:, d:] * jnp.dot(o, wba_ref[...], preferred_element_type=F32)
    x1 = x_ref[...] + mod[2:3] * jnp.dot(merged.astype(BF16), wout_ref[...],
                                         preferred_element_type=F32)
    x1_ref[...] = x1
    xn = x1 * lax.rsqrt(jnp.mean(x1 * x1, axis=-1, keepdims=True) + EPS) * g2_ref[...]
    h2 = (xn * (1.0 + mod[4:5]) + mod[3:4]).astype(BF16)
    h2_ref[...] = h2

    n_exp = n_groups * per_group
    logits = jnp.dot(h2, wr_ref[...], preferred_element_type=F32) + br_ref[...]
    lane = lax.broadcasted_iota(jnp.int32, logits.shape, 1).astype(F32)
    neg = jnp.float32(-jnp.inf)
    big = jnp.float32(1 << 20)
    lc = jnp.where(lane >= n_exp, jnp.where(lane < n_exp + n_groups, logits, neg), neg)
    c_max = jnp.max(lc, axis=-1, keepdims=True)
    g_idx = jnp.min(jnp.where(lc == c_max, lane - n_exp, big), axis=-1, keepdims=True)
    pc_top = 1.0 / jnp.sum(jnp.exp(lc - c_max), axis=-1, keepdims=True)
    lf = jnp.where(lane >= g_idx * per_group,
                   jnp.where(lane < (g_idx + 1.0) * per_group, logits, neg), neg)
    f1 = jnp.max(lf, axis=-1, keepdims=True)
    e1 = jnp.min(jnp.where(lf == f1, lane, big), axis=-1, keepdims=True)
    lf2 = jnp.where(lane == e1, neg, lf)
    f2 = jnp.max(lf2, axis=-1, keepdims=True)
    e2 = jnp.min(jnp.where(lf2 == f2, lane, big), axis=-1, keepdims=True)
    r = jnp.exp(f2 - f1)
    w1 = pc_top / (1.0 + r)
    w2 = pc_top * r / (1.0 + r)

    hot = jnp.where(lane == e1, 1.0, 0.0) + jnp.where(lane == e2, 1.0, 0.0)
    before = jnp.dot(ltri_ref[...], hot.astype(BF16), preferred_element_type=F32)
    r1 = jnp.sum(jnp.where(lane == e1, before, 0.0), axis=-1, keepdims=True)
    r2 = jnp.sum(jnp.where(lane == e2, before, 0.0), axis=-1, keepdims=True)
    counts = jnp.broadcast_to(jnp.sum(hot, axis=0, keepdims=True), count_ref.shape)
    count_ref[...] = counts
    expert_start = jnp.dot(counts, upper_ref[...], preferred_element_type=F32,
                           precision=lax.Precision.HIGHEST)[0:1]
    p1 = r1 + jnp.sum(jnp.where(lane == e1, expert_start, 0.0), axis=-1, keepdims=True)
    p2 = r2 + jnp.sum(jnp.where(lane == e2, expert_start, 0.0), axis=-1, keepdims=True)
    pos = lax.broadcasted_iota(jnp.int32, (logits.shape[0], order_ref.shape[0]), 1).astype(F32)
    at_pos = jnp.where(pos == p1, 1.0, jnp.where(pos == p2, 1.0, 0.0)).astype(BF16)
    order_ref[...] = lax.dot_general(at_pos, rowid_ref[...], (((0,), (0,)), ((), ())),
                                     preferred_element_type=F32)

    route = jnp.zeros(logits.shape, F32)
    for col, val in enumerate((e1, e2, w1, w2, r1, r2)):
        route = jnp.where(lane == col, val, route)
    route_ref[...] = route


def _mix_out(x, y_tm, o_first, o_second, gates, mod, w_glu, b_glu, w_bs, w_ba, w_out, g2,
             w_router, b_router, blk_seq, n_groups, per_group):
    t, d = x.shape
    tb = TOKEN_BLOCK
    blocks_first = o_first.shape[0] // tb
    row = lambda i, s: (i, 0)
    const = lambda i, s: (0, 0)
    grid_spec = pltpu.PrefetchScalarGridSpec(
        num_scalar_prefetch=1,
        grid=(t // tb,),
        in_specs=[
            pl.BlockSpec((tb, d), row),
            pl.BlockSpec((SSM_CHUNK, CHUNKS_PER_BLOCK, y_tm.shape[2]), lambda i, s: (0, i, 0)),
            pl.BlockSpec((tb, o_first.shape[1]), lambda i, s: (jnp.minimum(i, blocks_first - 1), 0)),
            pl.BlockSpec((tb, o_second.shape[1]), lambda i, s: (jnp.maximum(i - blocks_first, 0), 0)),
            pl.BlockSpec((tb, gates.shape[1]), row),
            pl.BlockSpec((1,) + mod.shape[1:], lambda i, s: (s[i], 0, 0)),
            pl.BlockSpec(w_glu.shape, const),
            pl.BlockSpec(b_glu.shape, const),
            pl.BlockSpec(w_bs.shape, const),
            pl.BlockSpec(w_ba.shape, const),
            pl.BlockSpec(w_out.shape, const),
            pl.BlockSpec(g2.shape, const),
            pl.BlockSpec(w_router.shape, const),
            pl.BlockSpec(b_router.shape, const),
            pl.BlockSpec((tb, tb), const),
            pl.BlockSpec((LANES, LANES), const),
            pl.BlockSpec((tb, LANES), const),
        ],
        out_specs=[
            pl.BlockSpec((tb, d), row),
            pl.BlockSpec((tb, d), row),
            pl.BlockSpec((tb, LANES), row),
            pl.BlockSpec((SUBLANES, LANES), row),
            pl.BlockSpec((FINE_TOP_K * tb, LANES), row),
        ],
    )
    strictly_lower = jnp.asarray(np.tril(np.ones((tb, tb), np.float32), -1), BF16)
    strictly_upper = jnp.asarray(np.triu(np.ones((LANES, LANES), np.float32), 1))
    row_digits = np.zeros((tb, LANES), np.float32)
    row_digits[:, 0] = np.arange(tb) // ROW_DIGIT
    row_digits[:, 1] = np.arange(tb) % ROW_DIGIT
    n_blk = t // tb
    return pl.pallas_call(
        functools.partial(_mix_out_kernel, n_groups=n_groups, per_group=per_group,
                          blocks_first=blocks_first),
        grid_spec=grid_spec,
        out_shape=[
            jax.ShapeDtypeStruct((t, d), F32),
            jax.ShapeDtypeStruct((t, d), BF16),
            jax.ShapeDtypeStruct((t, LANES), F32),
            jax.ShapeDtypeStruct((n_blk * SUBLANES, LANES), F32),
            jax.ShapeDtypeStruct((n_blk * FINE_TOP_K * tb, LANES), F32),
        ],
        compiler_params=_cparams(("arbitrary",)),
        name="mix_out",
    )(blk_seq, x, y_tm, o_first, o_second, gates, mod, w_glu, b_glu, w_bs, w_ba, w_out, g2,
      w_router, b_router, strictly_lower, strictly_upper, jnp.asarray(row_digits, BF16))


def _expert_kernel(te_ref, nt_ref, xs_ref, wg_ref, wu_ref, wd_ref, ys_ref):
    del te_ref
    live = pl.program_id(0) < nt_ref[0]

    @pl.when(live)
    def _():
        x = xs_ref[...]
        hg = jnp.dot(x, wg_ref[0].astype(BF16), preferred_element_type=F32)
        hu = jnp.dot(x, wu_ref[0].astype(BF16), preferred_element_type=F32)
        act = (hg * _sigmoid(hg) * hu).astype(BF16)
        ys_ref[...] = jnp.dot(act, wd_ref[0].astype(BF16),
                              preferred_element_type=F32).astype(ys_ref.dtype)

    @pl.when(jnp.logical_not(live))
    def _():
        ys_ref[...] = jnp.zeros(ys_ref.shape, ys_ref.dtype)


def _experts(xs, tile_expert, n_live, w_gate, w_up, w_down):
    p, d = xs.shape
    tm = EXPERT_TILE
    f = w_gate.shape[2]
    grid_spec = pltpu.PrefetchScalarGridSpec(
        num_scalar_prefetch=2,
        grid=(p // tm,),
        in_specs=[
            pl.BlockSpec((tm, d), lambda i, te, nt: (i, 0)),
            pl.BlockSpec((1, d, f), lambda i, te, nt: (te[i], 0, 0)),
            pl.BlockSpec((1, d, f), lambda i, te, nt: (te[i], 0, 0)),
            pl.BlockSpec((1, f, d), lambda i, te, nt: (te[i], 0, 0)),
        ],
        out_specs=pl.BlockSpec((tm, d), lambda i, te, nt: (i, 0)),
    )
    return pl.pallas_call(
        _expert_kernel,
        grid_spec=grid_spec,
        out_shape=jax.ShapeDtypeStruct((p, d), BF16),
        compiler_params=_cparams(("arbitrary",)),
        name="experts",
    )(tile_expert, n_live, xs, w_gate, w_up, w_down)


def _combine_kernel(seq_ref, x_ref, y1_ref, y2_ref, route_ref, mod_ref, o_ref):
    del seq_ref
    route = route_ref[...]
    w1 = route[:, 2:3]
    w2 = route[:, 3:4]
    moe = w1 * y1_ref[...].astype(F32) + w2 * y2_ref[...].astype(F32)
    o_ref[...] = x_ref[...] + mod_ref[0][5:6] * moe


def _combine(x1, y1, y2, route, mod, blk_seq):
    t, d = x1.shape
    tb = TOKEN_BLOCK
    row = lambda i, s: (i, 0)
    grid_spec = pltpu.PrefetchScalarGridSpec(
        num_scalar_prefetch=1,
        grid=(t // tb,),
        in_specs=[
            pl.BlockSpec((tb, d), row),
            pl.BlockSpec((tb, d), row),
            pl.BlockSpec((tb, d), row),
            pl.BlockSpec((tb, LANES), row),
            pl.BlockSpec((1,) + mod.shape[1:], lambda i, s: (s[i], 0, 0)),
        ],
        out_specs=pl.BlockSpec((tb, d), row),
    )
    return pl.pallas_call(
        _combine_kernel,
        grid_spec=grid_spec,
        out_shape=jax.ShapeDtypeStruct((t, d), F32),
        compiler_params=_cparams(("arbitrary",)),
        name="moe_combine",
    )(blk_seq, x1, y1, y2, route, mod)


def _dispatch_plan(route, block_counts, order, n_experts):
    t = route.shape[0]
    tb = TOKEN_BLOCK
    tm = EXPERT_TILE
    n_blk = t // tb
    i32 = jnp.int32
    cnt = block_counts.reshape(n_blk, SUBLANES, LANES)[:, 0, :n_experts].astype(i32)
    blk_end = jnp.cumsum(cnt, axis=0)
    blk_base = (blk_end - cnt).reshape(-1)
    exp_start = (jnp.cumsum(cnt, axis=1) - cnt).reshape(-1)
    counts = blk_end[-1]
    padded = ((counts + tm - 1) // tm) * tm
    pad_end = jnp.cumsum(padded)
    pad_off = pad_end - padded

    eid = route[:, :FINE_TOP_K].astype(i32)
    local_rank = route[:, 2 * FINE_TOP_K:3 * FINE_TOP_K].astype(i32)
    tok_blk = (jnp.arange(t, dtype=i32) // tb)[:, None]
    dest = pad_off[eid] + blk_base[tok_blk * n_experts + eid] + local_rank

    n_rows = FINE_TOP_K * t + n_experts * tm
    n_tiles = n_rows // tm
    tile_start = jnp.arange(n_tiles, dtype=i32) * tm
    tile_expert = jnp.minimum(jnp.sum((pad_end[None, :] <= tile_start[:, None]).astype(i32), axis=1),
                              n_experts - 1)
    n_live = (pad_end[-1] // tm).astype(i32).reshape(1)
    last_live = jnp.maximum(n_live[0] - 1, 0)
    live = (tile_start // tm < n_live[0])
    tile_expert = jnp.where(live, tile_expert, tile_expert[last_live])

    k = tile_start[:, None] + jnp.arange(tm, dtype=i32)[None, :] - pad_off[tile_expert][:, None]
    ends = blk_end.T[tile_expert]
    blk = jnp.minimum(jnp.sum((ends[:, None, :] <= k[:, :, None]).astype(i32), axis=-1), n_blk - 1)
    cell = blk * n_experts + tile_expert[:, None]
    pos = jnp.clip(exp_start[cell] + k - blk_base[cell], 0, FINE_TOP_K * tb - 1)
    order_row = (order[:, 0] * ROW_DIGIT + order[:, 1]).astype(i32)
    src = blk * tb + order_row[blk * (FINE_TOP_K * tb) + pos]
    real = live[:, None] & (k < counts[tile_expert][:, None])
    spread = jnp.arange(n_rows, dtype=i32).reshape(n_tiles, tm) % t
    src_tok = jnp.where(real, src, spread).reshape(-1)
    return dest, src_tok, tile_expert, n_live


def _rope_tables(max_len):
    rows = max_len // GRID_W
    row = jnp.repeat(jnp.arange(rows, dtype=F32), GRID_W)
    col = jnp.tile(jnp.arange(GRID_W, dtype=F32), rows)
    half = HEAD_DIM // 2
    inv_freq = 1.0 / (ROPE_THETA ** (jnp.arange(0, half, 2, dtype=F32) / half))
    ang_r = row[:, None] * inv_freq[None, :]
    ang_c = col[:, None] * inv_freq[None, :]
    ang = jnp.concatenate([ang_r, ang_r, ang_c, ang_c], axis=-1)
    sign = jnp.where((jnp.arange(HEAD_DIM) % (HEAD_DIM // 2)) < HEAD_DIM // 4, -1.0, 1.0)
    reps = LANES // HEAD_DIM
    return jnp.tile(jnp.cos(ang), (1, reps)), jnp.tile(jnp.sin(ang) * sign[None, :], (1, reps))


def _to_block_order(a):
    n, d = a.shape
    return a.reshape(n // TOKEN_BLOCK, CHUNKS_PER_BLOCK, SSM_CHUNK, d).transpose(0, 2, 1, 3).reshape(n, d)


def _from_block_order(a):
    n, d = a.shape
    return a.reshape(n // TOKEN_BLOCK, SSM_CHUNK, CHUNKS_PER_BLOCK, d).transpose(0, 2, 1, 3).reshape(n, d)


def kernel(x_prompt, x_sample, c_prompt, c_sample, w_ada, b_ada, g_norm1, g_norm2, w_in, ssm_a_re, ssm_a_im, ssm_log_dt, ssm_b_re, ssm_b_im, ssm_c_re, ssm_c_im, ssm_d, w_glu, b_glu, g_q, g_k, w_branch_ssm, w_branch_attn, w_out, w_coarse, b_coarse, w_fine, b_fine, w_expert_gate, w_expert_up, w_expert_down):
    b1, l1, d = x_prompt.shape
    b2, l2, _ = x_sample.shape
    depth = w_in.shape[0]
    d_ssm = w_glu.shape[-1]
    d_attn = w_branch_attn.shape[1]
    d_kv = N_KV_HEADS * HEAD_DIM
    n_heads = d_attn // HEAD_DIM
    n_groups = w_coarse.shape[-1]
    n_experts = w_fine.shape[-1]
    per_group = n_experts // n_groups
    n_ssm_groups = ssm_b_re.shape[2]
    ssm_h = ssm_b_re.shape[-1]
    tb = TOKEN_BLOCK
    assert l1 % tb == 0 and l2 % tb == 0 and l1 % KV_TILE == 0 and l2 % KV_TILE == 0
    assert d_attn == N_KV_HEADS * (n_heads // N_KV_HEADS) * HEAD_DIM and 2 * HEAD_DIM == LANES
    assert n_ssm_groups * ssm_h == d_ssm and d_ssm % LANES == 0 and LANES % ssm_h == 0
    assert 2 * ssm_a_re.shape[-1] == LANES
    assert n_experts + n_groups <= LANES

    seq_lens = [l1] * b1 + [l2] * b2
    t = sum(seq_lens)
    x = jnp.concatenate([x_prompt.reshape(b1 * l1, d), x_sample.reshape(b2 * l2, d)], axis=0)
    x = _to_block_order(x)
    c = jnp.concatenate([c_prompt, c_sample], axis=0)
    blk_seq = np.concatenate([np.full(n // tb, s, np.int32) for s, n in enumerate(seq_lens)])
    blk_pos = np.concatenate([np.arange(n // tb, dtype=np.int32) for n in seq_lens])
    blk_seq, blk_pos = jnp.asarray(blk_seq), jnp.asarray(blk_pos)
    cos_t, sin_t = (_to_block_order(tab) for tab in _rope_tables(max(l1, l2)))

    seg_rows = max(seq_lens) // SSM_CHUNK
    seg_seq_rows = []
    filled = 0
    for n in seq_lens:
        rows = n // SSM_CHUNK
        assert rows % SUBLANES == 0 and seg_rows % rows == 0
        if filled == 0:
            seg_seq_rows.append(rows)
        assert seg_seq_rows[-1] == rows
        filled = (filled + rows) % seg_rows
    assert filled == 0
    seg_seq_rows = jnp.asarray(np.array(seg_seq_rows, np.int32))

    rep = n_heads // N_KV_HEADS
    slot_head = np.array([g * rep + j for j in range(rep) for g in range(N_KV_HEADS)])
    col_perm = (slot_head[:, None] * HEAD_DIM + np.arange(HEAD_DIM)[None, :]).reshape(-1)
    in_perm = np.concatenate([np.arange(d_ssm), d_ssm + col_perm, np.arange(d_ssm + d_attn, w_in.shape[-1])])

    mod_all = _ada_modulation(c, w_ada, b_ada)
    gq_t = jnp.tile(g_q.astype(F32), (1, LANES // HEAD_DIM))
    gk_t = jnp.tile(g_k.astype(F32), (1, LANES // HEAD_DIM))
    head_mean = jnp.asarray(np.kron(np.eye(LANES // HEAD_DIM), np.full((HEAD_DIM, HEAD_DIM), 1.0 / HEAD_DIM)), BF16)
    w_router = jnp.concatenate([w_fine, w_coarse,
                                jnp.zeros((depth, d, LANES - n_experts - n_groups), F32)], -1).astype(BF16)
    b_router = jnp.concatenate([b_fine, b_coarse,
                                jnp.zeros((depth, LANES - n_experts - n_groups), F32)], -1)

    for i in range(depth):
        mod = mod_all[i]
        u_tm, q, kt, v, gates = _mix_in(
            x, mod, g_norm1[i][None], w_in[i][:, in_perm].astype(BF16), cos_t, sin_t,
            gq_t[i][None], gk_t[i][None], head_mean, blk_seq, blk_pos, (d_ssm, d_attn, d_kv))

        w_in_rows, lag_blocks, w_out_cols, ssm_a = _ssm_weights(
            ssm_a_re[i], ssm_a_im[i], ssm_log_dt[i], ssm_b_re[i], ssm_b_im[i],
            ssm_c_re[i], ssm_c_im[i], ssm_d[i])
        states = _ssm_states(u_tm, w_in_rows, ssm_a, seg_rows, seg_seq_rows, ssm_h)
        y_tm = _ssm_output(u_tm, states, lag_blocks, w_out_cols, ssm_h)

        o_first = _attention(q, kt, v, 0, b1, l1)
        o_second = _attention(q, kt, v, b1 * l1, b2, l2)

        w_ba = w_branch_attn[i][col_perm].astype(BF16)
        x1, h2, route, block_counts, order = _mix_out(
            x, y_tm, o_first, o_second, gates, mod, w_glu[i].astype(BF16), b_glu[i][None],
            w_branch_ssm[i].astype(BF16), w_ba, w_out[i].astype(BF16), g_norm2[i][None],
            w_router[i], b_router[i][None], blk_seq, n_groups, per_group)

        dest, src_tok, tile_expert, n_live = _dispatch_plan(route, block_counts, order, n_experts)
        xs = h2.at[src_tok].get(mode='promise_in_bounds')
        ys = _experts(xs, tile_expert, n_live, w_expert_gate[i], w_expert_up[i], w_expert_down[i])
        y1 = ys.at[dest[:, 0]].get(mode='promise_in_bounds')
        y2 = ys.at[dest[:, 1]].get(mode='promise_in_bounds')
        x = _combine(x1, y1, y2, route, mod, blk_seq)

    x = _from_block_order(x)
    y_prompt = x[:b1 * l1].reshape(b1, l1, d)
    y_sample = x[b1 * l1:].reshape(b2, l2, d)
    return (y_prompt, y_sample)
```

```python
import functools
import math

import jax
import jax.numpy as jnp
import numpy as np
from jax import lax
from jax.experimental import pallas as pl
from jax.experimental.pallas import tpu as pltpu

F32 = jnp.float32
BF16 = jnp.bfloat16

EPS = 1e-6
GRID_W = 64
ROPE_THETA = 10000.0
HEAD_DIM = 64
N_KV_HEADS = 2
FINE_TOP_K = 2
LOG2E = 1.4426950408889634
LANES = 128
SUBLANES = 8
SSM_CHUNK = 16

TOKEN_BLOCK = 512
CHUNKS_PER_BLOCK = TOKEN_BLOCK // SSM_CHUNK
Q_TILE = 256
KV_TILE = TOKEN_BLOCK
KV_STEP_TILES = 4
EXPERT_TILE = 512
SSM_ROW_BLOCK = 512
VMEM_LIMIT = 56 * 1024 * 1024


def _sigmoid(x):
    return 1.0 / (1.0 + jnp.exp(-x))


def _cparams(sem):
    return pltpu.CompilerParams(dimension_semantics=sem, vmem_limit_bytes=VMEM_LIMIT)


def _ada_kernel(c_ref, w_ref, b_ref, o_ref):
    c = c_ref[...]
    ca = c * _sigmoid(c)
    o_ref[0, 0] = jnp.dot(ca, w_ref[0], preferred_element_type=F32,
                          precision=lax.Precision.HIGHEST) + b_ref[0, 0]


def _ada_modulation(c, w_ada, b_ada):
    depth, d, _ = w_ada.shape
    nb = c.shape[0]
    n_mod = w_ada.shape[2] // d
    out = pl.pallas_call(
        _ada_kernel,
        grid=(depth, n_mod),
        in_specs=[
            pl.BlockSpec((nb, d), lambda l, j: (0, 0)),
            pl.BlockSpec((1, d, d), lambda l, j: (l, 0, j)),
            pl.BlockSpec((1, 1, 1, d), lambda l, j: (l, j, 0, 0)),
        ],
        out_specs=pl.BlockSpec((1, 1, nb, d), lambda l, j: (l, j, 0, 0)),
        out_shape=jax.ShapeDtypeStruct((depth, n_mod, nb, d), F32),
        compiler_params=_cparams(("arbitrary", "arbitrary")),
        name="ada_modulation",
    )(c, w_ada, b_ada.reshape(depth, n_mod, 1, d))
    return out.transpose(0, 2, 1, 3)


def _rope(x, cos, sin_signed, low_half):
    up = pltpu.roll(x, LANES - 16, axis=1)
    down = pltpu.roll(x, 16, axis=1)
    return x * cos + jnp.where(low_half, up, down) * sin_signed


def _mix_in_kernel(seq_ref, pos_ref, x_ref, mod_ref, g1_ref, w_ref, cos_ref, sin_ref,
                   gq_ref, gk_ref, hm_ref, u_ref, q_ref, kt_ref, v_ref, gate_ref,
                   *, d_ssm, d_attn, d_kv):
    del seq_ref, pos_ref
    x = x_ref[...]
    mod = mod_ref[0]
    xn = x * lax.rsqrt(jnp.mean(x * x, axis=-1, keepdims=True) + EPS) * g1_ref[...]
    h = (xn * (1.0 + mod[1:2]) + mod[0:1]).astype(BF16)

    def proj(lo, hi):
        return jnp.dot(h, w_ref[:, lo:hi], preferred_element_type=F32)

    u = proj(0, d_ssm).astype(BF16)
    n_c = u.shape[0] // SSM_CHUNK
    for t in range(SSM_CHUNK):
        u_ref[t] = u[t * n_c:(t + 1) * n_c]

    cos = cos_ref[...]
    sin_signed = sin_ref[...]
    lane = lax.broadcasted_iota(jnp.int32, cos.shape, 1)
    low_half = (lane % 32) < 16
    hm = hm_ref[...]

    def norm_rope(z, gain, scale):
        cols = []
        for j in range(z.shape[1] // LANES):
            zj = z[:, j * LANES:(j + 1) * LANES]
            msq = jnp.dot((zj * zj).astype(BF16), hm, preferred_element_type=F32)
            zn = zj * lax.rsqrt(msq + EPS) * gain
            cols.append(_rope(zn, cos, sin_signed, low_half) * scale)
        return jnp.concatenate(cols, axis=1) if len(cols) > 1 else cols[0]

    o = d_ssm
    q_ref[...] = norm_rope(proj(o, o + d_attn), gq_ref[...],
                           (HEAD_DIM ** -0.5) * LOG2E).astype(BF16)
    o += d_attn
    kt_ref[0] = norm_rope(proj(o, o + d_kv), gk_ref[...], 1.0).T.astype(BF16)
    o += d_kv
    v_ref[...] = proj(o, o + d_kv).astype(BF16)
    o += d_kv
    n_gate = gate_ref.shape[1]
    half = n_gate // 2
    gate_ref[:, :half] = _sigmoid(proj(o, o + half)).astype(BF16)
    gate_ref[:, half:] = _sigmoid(proj(o + half, o + n_gate)).astype(BF16)


def _mix_in(x, mod, g1, w_in, cos_t, sin_t, gq_t, gk_t, hm, blk_seq, blk_pos, dims):
    t, d = x.shape
    d_ssm, d_attn, d_kv = dims
    n_gate = w_in.shape[1] - d_ssm - d_attn - 2 * d_kv
    tb = TOKEN_BLOCK
    row = lambda i, s, p: (i, 0)
    const = lambda i, s, p: (0, 0)
    grid_spec = pltpu.PrefetchScalarGridSpec(
        num_scalar_prefetch=2,
        grid=(t // tb,),
        in_specs=[
            pl.BlockSpec((tb, d), row),
            pl.BlockSpec((1,) + mod.shape[1:], lambda i, s, p: (s[i], 0, 0)),
            pl.BlockSpec((1, d), const),
            pl.BlockSpec(w_in.shape, const),
            pl.BlockSpec((tb, LANES), lambda i, s, p: (p[i], 0)),
            pl.BlockSpec((tb, LANES), lambda i, s, p: (p[i], 0)),
            pl.BlockSpec((1, LANES), const),
            pl.BlockSpec((1, LANES), const),
            pl.BlockSpec((LANES, LANES), const),
        ],
        out_specs=[
            pl.BlockSpec((SSM_CHUNK, CHUNKS_PER_BLOCK, d_ssm), lambda i, s, p: (0, i, 0)),
            pl.BlockSpec((tb, d_attn), row),
            pl.BlockSpec((1, d_kv, tb), lambda i, s, p: (i, 0, 0)),
            pl.BlockSpec((tb, d_kv), row),
            pl.BlockSpec((tb, n_gate), row),
        ],
    )
    return pl.pallas_call(
        functools.partial(_mix_in_kernel, d_ssm=d_ssm, d_attn=d_attn, d_kv=d_kv),
        grid_spec=grid_spec,
        out_shape=[
            jax.ShapeDtypeStruct((SSM_CHUNK, t // SSM_CHUNK, d_ssm), BF16),
            jax.ShapeDtypeStruct((t, d_attn), BF16),
            jax.ShapeDtypeStruct((t // tb, d_kv, tb), BF16),
            jax.ShapeDtypeStruct((t, d_kv), BF16),
            jax.ShapeDtypeStruct((t, n_gate), BF16),
        ],
        compiler_params=_cparams(("arbitrary",)),
        name="mix_in",
    )(blk_seq, blk_pos, x, mod, g1, w_in, cos_t, sin_t, gq_t, gk_t, hm)


def _ssm_weights(a_re, a_im, log_dt, b_re, b_im, c_re, c_im, d_skip):
    n_dirs, g, p = a_re.shape
    hch = b_re.shape[-1]
    c = SSM_CHUNK
    nq = g * hch // LANES
    gq = g // nq
    lam = lax.complex(a_re.astype(F32), a_im.astype(F32))
    dt = jnp.exp(log_dt.astype(F32))[..., None]
    lam_dt = lam * dt
    coef = (jnp.exp(lam_dt) - 1.0) / lam
    steps = jnp.arange(c + 1, dtype=F32)
    pw = jnp.exp(lam_dt[:, None] * steps[None, :, None, None])
    bc = lax.complex(b_re.astype(F32), b_im.astype(F32)) * coef[..., None]
    cc = lax.complex(c_re.astype(F32), c_im.astype(F32))
    eye = jnp.eye(gq, dtype=F32)

    kern = jnp.einsum('rghp,rdgp,rgpk->rdghk', cc, pw[:, :c], bc).real
    dmat = jnp.eye(hch, dtype=F32)[None] * d_skip.astype(F32).reshape(g, hch)[:, :, None]
    lag0 = (kern[0, 0] + kern[1, 0] + dmat)[None]
    by_lag = jnp.concatenate([kern[1, 1:][::-1], lag0, kern[0, 1:]], 0)
    by_lag = by_lag.reshape(2 * c - 1, nq, gq, hch, hch)
    lag_blocks = jnp.einsum('dqkoh,gk->qdghko', by_lag, eye).reshape(nq, 2 * c - 1, LANES, LANES)

    win_f = pw[0, c - 1 - jnp.arange(c)][:, :, None, :] * bc[0].transpose(0, 2, 1)[None]
    win_b = pw[1, jnp.arange(c)][:, :, None, :] * bc[1].transpose(0, 2, 1)[None]

    def in_rows(w):
        parts = jnp.stack([w.real, w.imag], 0).reshape(2, c, nq, gq, hch, p)
        parts = parts.transpose(2, 0, 1, 3, 4, 5).reshape(nq, 2, c * gq * hch, p)
        return jnp.concatenate([parts, parts], -1)

    w_in_rows = jnp.stack([in_rows(win_f), in_rows(win_b)], 1)

    cout_f = cc[0][None] * pw[0, 1 + jnp.arange(c)][:, :, None, :]
    cout_b = cc[1][None] * pw[1, c - jnp.arange(c)][:, :, None, :]

    def out_cols(w):
        parts = jnp.stack([w.real, -w.imag], 0).reshape(2, c, nq, gq, hch, p)
        return parts.transpose(2, 0, 5, 1, 3, 4).reshape(nq, 2, p, c * gq * hch)

    w_out_cols = jnp.concatenate([out_cols(cout_f), out_cols(cout_b)], 1)

    sub = SUBLANES
    rows = jnp.arange(sub)
    scan_pos = jnp.stack([rows, sub - 1 - rows], 0)
    a_log = lam_dt * c
    tables = []
    d = 1
    while d < sub:
        tables.append(jnp.where((scan_pos >= d)[:, :, None, None], jnp.exp(a_log * d)[:, None], 0.0))
        d *= 2
    tables.append(jnp.exp(a_log[:, None] * (scan_pos + 1).astype(F32)[:, :, None, None]))
    a_tab = jnp.stack(tables, 1)
    a_tab = jnp.stack([a_tab.real, a_tab.imag], 2).reshape(n_dirs, len(tables), 2, sub, nq, gq * p)
    a_tab = a_tab.transpose(4, 0, 1, 2, 3, 5)
    return w_in_rows.astype(BF16), lag_blocks.astype(BF16), w_out_cols.astype(BF16), a_tab


def _group_of(index, per_group):
    return (index // per_group) % (LANES // per_group)


def _ssm_states_kernel(len_ref, u_ref, win_ref, a_ref, s_ref, w_sc, x_sc, s_sc, *, ssm_h):
    n_rows, width = x_sc.shape
    half = width // 2

    @pl.when(pl.program_id(2) == 0)
    def _():
        shape = win_ref.shape[3:]
        row_group = _group_of(lax.broadcasted_iota(jnp.int32, shape, 0), ssm_h)
        lane_half = lax.broadcasted_iota(jnp.int32, shape, 1) // (LANES // 2)
        for a in range(2):
            rows = win_ref[0, 0, a]
            for j in range(half // LANES):
                col = a * half + j * LANES
                w_sc[:, col:col + LANES] = jnp.where(row_group == 2 * j + lane_half, rows,
                                                     jnp.zeros_like(rows))

    ucat = jnp.concatenate([u_ref[t] for t in range(SSM_CHUNK)], axis=1)
    x_sc[...] = jnp.dot(ucat, w_sc[...], preferred_element_type=F32)
    sub = SUBLANES
    tile = (sub, half)
    row_id = lax.broadcasted_iota(jnp.int32, tile, 0)
    seq_rows = len_ref[pl.program_id(2)]
    n_tiles = n_rows // sub
    n_levels = a_ref.shape[2] - 1

    def cmul(c_re, c_im, z_re, z_im):
        return c_re * z_re - c_im * z_im, c_re * z_im + c_im * z_re

    def run(backward):
        def toward_later(z, dist):
            return pltpu.roll(z, sub - dist if backward else dist, axis=0)

        def body(i, carry):
            s_re, s_im = carry
            tile_i = n_tiles - 1 - i if backward else i
            row0 = pl.multiple_of(tile_i * sub, sub)
            first = (row0 + sub) if backward else row0
            keep = jnp.where(lax.rem(first, seq_rows) == 0, 0.0, 1.0).astype(F32)
            s_re = s_re * keep
            s_im = s_im * keep
            z_re = x_sc[pl.ds(row0, sub), 0:half]
            z_im = x_sc[pl.ds(row0, sub), half:width]
            for lvl in range(n_levels):
                m_re, m_im = cmul(a_ref[0, 0, lvl, 0], a_ref[0, 0, lvl, 1],
                                  toward_later(z_re, 2 ** lvl), toward_later(z_im, 2 ** lvl))
                z_re, z_im = z_re + m_re, z_im + m_im
            c_re, c_im = cmul(a_ref[0, 0, n_levels, 0], a_ref[0, 0, n_levels, 1], s_re, s_im)
            z_re, z_im = z_re + c_re, z_im + c_im
            entry = sub - 1 if backward else 0
            s_sc[pl.ds(row0, sub), 0:half] = jnp.where(row_id == entry, s_re, toward_later(z_re, 1))
            s_sc[pl.ds(row0, sub), half:width] = jnp.where(row_id == entry, s_im, toward_later(z_im, 1))
            last = sub - 1 - entry
            return (jnp.broadcast_to(z_re[last:last + 1], tile),
                    jnp.broadcast_to(z_im[last:last + 1], tile))
        zero = jnp.zeros(tile, F32)
        lax.fori_loop(0, n_tiles, body, (zero, zero))

    direction = pl.program_id(1)

    @pl.when(direction == 0)
    def _():
        run(False)

    @pl.when(direction == 1)
    def _():
        run(True)

    s_ref[...] = s_sc[...].astype(s_ref.dtype)


def _ssm_states(u_tm, w_in_rows, a_tab, seg_rows, seg_seq_rows, ssm_h):
    c, nc, d_ssm = u_tm.shape
    nq, n_dirs, _, k, two_p = w_in_rows.shape
    assert a_tab.shape[-2] == SUBLANES
    n = a_tab.shape[-1] * 2
    assert two_p == LANES and k == c * LANES
    grid_spec = pltpu.PrefetchScalarGridSpec(
        num_scalar_prefetch=1,
        grid=(nq, n_dirs, nc // seg_rows),
        in_specs=[
            pl.BlockSpec((c, seg_rows, LANES), lambda q, r, s, ln: (0, s, q)),
            pl.BlockSpec((1, 1, 2, k, two_p), lambda q, r, s, ln: (q, r, 0, 0, 0)),
            pl.BlockSpec((1, 1) + a_tab.shape[2:], lambda q, r, s, ln: (q, r, 0, 0, 0, 0)),
        ],
        out_specs=pl.BlockSpec((seg_rows, n), lambda q, r, s, ln: (s, q * n_dirs + r)),
        scratch_shapes=[pltpu.VMEM((k, n), BF16), pltpu.VMEM((seg_rows, n), F32),
                        pltpu.VMEM((seg_rows, n), F32)],
    )
    return pl.pallas_call(
        functools.partial(_ssm_states_kernel, ssm_h=ssm_h),
        grid_spec=grid_spec,
        out_shape=jax.ShapeDtypeStruct((nc, nq * n_dirs * n), BF16),
        compiler_params=_cparams(("arbitrary", "arbitrary", "arbitrary")),
        name="ssm_states",
    )(seg_seq_rows, u_tm, w_in_rows, a_tab)


def _ssm_output_kernel(u_ref, s_ref, lag_ref, wcol_ref, y_ref, wintra_sc, wout_sc, *, ssm_h):
    n_t_out = y_ref.shape[0]

    @pl.when(pl.program_id(2) == 0)
    def _():
        t_out0 = pl.program_id(1) * n_t_out
        for t_in in range(SSM_CHUNK):
            for j in range(n_t_out):
                wintra_sc[t_in * LANES:(t_in + 1) * LANES, j * LANES:(j + 1) * LANES] = (
                    lag_ref[0, t_out0 + j - t_in + SSM_CHUNK - 1])
        n_planes, p, n_cols = wcol_ref.shape[1:]
        col_group = _group_of(lax.broadcasted_iota(jnp.int32, (p, n_cols), 1), ssm_h)
        for r in range(n_planes):
            cols = wcol_ref[0, r]
            for k in range(LANES // ssm_h):
                row = (r * (LANES // ssm_h) + k) * p
                wout_sc[row:row + p, :] = jnp.where(col_group == k, cols, jnp.zeros_like(cols))

    ucat = jnp.concatenate([u_ref[t] for t in range(SSM_CHUNK)], axis=1)
    y = jnp.dot(ucat, wintra_sc[...], preferred_element_type=F32)
    y += jnp.dot(s_ref[...], wout_sc[...], preferred_element_type=F32)
    for t in range(n_t_out):
        y_ref[t] = y[:, t * LANES:(t + 1) * LANES].astype(y_ref.dtype)


def _ssm_output(u_tm, s, lag_blocks, w_out_cols, ssm_h):
    c, nc, d_ssm = u_tm.shape
    nq, n_planes, p, k = w_out_cols.shape
    n_half = 2
    n = k // n_half
    s_width = n_planes * (LANES // ssm_h) * p
    rb = min(SSM_ROW_BLOCK, nc)
    return pl.pallas_call(
        functools.partial(_ssm_output_kernel, ssm_h=ssm_h),
        grid=(nq, n_half, nc // rb),
        in_specs=[
            pl.BlockSpec((c, rb, LANES), lambda q, h, i: (0, i, q)),
            pl.BlockSpec((rb, s_width), lambda q, h, i: (i, q)),
            pl.BlockSpec((1,) + lag_blocks.shape[1:], lambda q, h, i: (q, 0, 0, 0)),
            pl.BlockSpec((1, n_planes, p, n), lambda q, h, i: (q, 0, 0, h)),
        ],
        out_specs=pl.BlockSpec((c // n_half, rb, LANES), lambda q, h, i: (h, i, q)),
        out_shape=jax.ShapeDtypeStruct((c, nc, d_ssm), BF16),
        scratch_shapes=[pltpu.VMEM((k, n), BF16), pltpu.VMEM((s_width, n), BF16)],
        compiler_params=_cparams(("arbitrary", "arbitrary", "arbitrary")),
        name="ssm_output",
    )(u_tm, s, lag_blocks, w_out_cols)


def _attn_kernel(q_ref, kt_ref, v_ref, o_ref, m_sc, l_sc, acc_sc, *, n_kv_tiles):
    tq = q_ref.shape[0]
    n_slices = q_ref.shape[1] // LANES
    q = jnp.concatenate([q_ref[:, j * LANES:(j + 1) * LANES] for j in range(n_slices)], axis=0)
    m_sc[...] = jnp.full(m_sc.shape, -jnp.inf, F32)
    l_sc[...] = jnp.zeros(l_sc.shape, F32)
    acc_sc[...] = jnp.zeros(acc_sc.shape, F32)
    step_tiles = min(KV_STEP_TILES, n_kv_tiles)
    kv_step_len = step_tiles * KV_TILE
    n_rep = kv_step_len // LANES
    lane_head = lax.broadcasted_iota(jnp.int32, acc_sc.shape, 1) // HEAD_DIM
    key_head = lax.broadcasted_iota(jnp.int32, (kt_ref.shape[1], kv_step_len), 0) // HEAD_DIM

    def kv_step(kb, carry):
        start = pl.multiple_of(kb * kv_step_len, kv_step_len)
        kt = jnp.concatenate([kt_ref[kb * step_tiles + i] for i in range(step_tiles)], axis=1)
        v = v_ref[pl.ds(start, kv_step_len), :]
        acc = acc_sc[...]
        for g in range(N_KV_HEADS):
            ktg = jnp.where(key_head == g, kt, jnp.zeros_like(kt))
            s = jnp.dot(q, ktg, preferred_element_type=F32)
            m_old = m_sc[g]
            m_new = jnp.maximum(m_old, jnp.max(s, axis=-1, keepdims=True))
            alpha = jnp.exp2(m_old - m_new)
            p = jnp.exp2(s - jnp.concatenate([m_new] * n_rep, axis=1))
            l_part = p[:, 0:LANES]
            for c in range(1, n_rep):
                l_part = l_part + p[:, c * LANES:(c + 1) * LANES]
            l_sc[g] = alpha * l_sc[g] + l_part
            m_sc[g] = m_new
            pv = jnp.dot(p.astype(BF16), v, preferred_element_type=F32)
            acc = jnp.where(lane_head == g, alpha * acc + pv, acc)
        acc_sc[...] = acc
        return carry

    lax.fori_loop(0, n_kv_tiles // step_tiles, kv_step, 0)
    inv = [1.0 / jnp.sum(l_sc[g], axis=-1, keepdims=True) for g in range(N_KV_HEADS)]
    out = acc_sc[...] * jnp.where(lane_head == 0, inv[0], inv[1])
    for j in range(n_slices):
        o_ref[:, j * LANES:(j + 1) * LANES] = out[j * tq:(j + 1) * tq].astype(o_ref.dtype)


def _attention(q, kt, v, first_token, n_seqs, seq_len):
    d_attn = q.shape[1]
    d_kv = v.shape[1]
    tq = min(Q_TILE, seq_len)
    q_tiles = seq_len // tq
    kv_tiles = seq_len // KV_TILE
    assert first_token % seq_len == 0
    seq0 = first_token // seq_len
    rows = (d_attn // LANES) * tq
    return pl.pallas_call(
        functools.partial(_attn_kernel, n_kv_tiles=kv_tiles),
        grid=(n_seqs, q_tiles),
        in_specs=[
            pl.BlockSpec((tq, d_attn), lambda i, j: ((seq0 + i) * q_tiles + j, 0)),
            pl.BlockSpec((kv_tiles, d_kv, KV_TILE), lambda i, j: (seq0 + i, 0, 0)),
            pl.BlockSpec((seq_len, d_kv), lambda i, j: (seq0 + i, 0)),
        ],
        out_specs=pl.BlockSpec((tq, d_attn), lambda i, j: (i * q_tiles + j, 0)),
        out_shape=jax.ShapeDtypeStruct((n_seqs * seq_len, d_attn), BF16),
        scratch_shapes=[pltpu.VMEM((N_KV_HEADS, rows, LANES), F32),
                        pltpu.VMEM((N_KV_HEADS, rows, LANES), F32),
                        pltpu.VMEM((rows, LANES), F32)],
        compiler_params=_cparams(("arbitrary", "arbitrary")),
        name="attention",
    )(q, kt, v)


def _mix_out_kernel(seq_ref, x_ref, y_ref, o1_ref, o2_ref, gate_ref, mod_ref, wglu_ref, bglu_ref,
                    wbs_ref, wba_ref, wout_ref, g2_ref, wr_ref, br_ref, ltri_ref,
                    x1_ref, h2_ref, route_ref, count_ref, cnt_sc,
                    *, n_groups, per_group, blocks_first):
    del seq_ref
    mod = mod_ref[0]
    y = jnp.concatenate([y_ref[t] for t in range(SSM_CHUNK)], axis=0).astype(F32)
    y = 0.5 * y * (1.0 + jnp.tanh(math.sqrt(2.0 / math.pi) * (y + 0.044715 * (y * y * y))))
    glu = jnp.dot(y.astype(BF16), wglu_ref[...], preferred_element_type=F32) + bglu_ref[...]
    ys = (y * _sigmoid(glu)).astype(BF16)
    d = x_ref.shape[1]
    gate = gate_ref[...].astype(F32)
    o = jnp.where(pl.program_id(0) < blocks_first, o1_ref[...], o2_ref[...])
    merged = gate[:, :d] * jnp.dot(ys, wbs_ref[...], preferred_element_type=F32)
    merged += gate[:, d:] * jnp.dot(o, wba_ref[...], preferred_element_type=F32)
    x1 = x_ref[...] + mod[2:3] * jnp.dot(merged.astype(BF16), wout_ref[...],
                                         preferred_element_type=F32)
    x1_ref[...] = x1
    xn = x1 * lax.rsqrt(jnp.mean(x1 * x1, axis=-1, keepdims=True) + EPS) * g2_ref[...]
    h2 = (xn * (1.0 + mod[4:5]) + mod[3:4]).astype(BF16)
    h2_ref[...] = h2

    n_exp = n_groups * per_group
    logits = jnp.dot(h2, wr_ref[...], preferred_element_type=F32) + br_ref[...]
    lane = lax.broadcasted_iota(jnp.int32, logits.shape, 1).astype(F32)
    neg = jnp.float32(-jnp.inf)
    big = jnp.float32(1 << 20)
    lc = jnp.where(lane >= n_exp, jnp.where(lane < n_exp + n_groups, logits, neg), neg)
    c_max = jnp.max(lc, axis=-1, keepdims=True)
    g_idx = jnp.min(jnp.where(lc == c_max, lane - n_exp, big), axis=-1, keepdims=True)
    pc_top = 1.0 / jnp.sum(jnp.exp(lc - c_max), axis=-1, keepdims=True)
    lf = jnp.where(lane >= g_idx * per_group,
                   jnp.where(lane < (g_idx + 1.0) * per_group, logits, neg), neg)
    f1 = jnp.max(lf, axis=-1, keepdims=True)
    e1 = jnp.min(jnp.where(lf == f1, lane, big), axis=-1, keepdims=True)
    lf2 = jnp.where(lane == e1, neg, lf)
    f2 = jnp.max(lf2, axis=-1, keepdims=True)
    e2 = jnp.min(jnp.where(lf2 == f2, lane, big), axis=-1, keepdims=True)
    r = jnp.exp(f2 - f1)
    w1 = pc_top / (1.0 + r)
    w2 = pc_top * r / (1.0 + r)

    @pl.when(pl.program_id(0) == 0)
    def _():
        cnt_sc[...] = jnp.zeros(cnt_sc.shape, F32)

    hot = jnp.where(lane == e1, 1.0, 0.0) + jnp.where(lane == e2, 1.0, 0.0)
    before = jnp.dot(ltri_ref[...], hot.astype(BF16), preferred_element_type=F32) + cnt_sc[0:1, :]
    r1 = jnp.sum(jnp.where(lane == e1, before, 0.0), axis=-1, keepdims=True)
    r2 = jnp.sum(jnp.where(lane == e2, before, 0.0), axis=-1, keepdims=True)
    cnt_sc[...] = cnt_sc[...] + jnp.sum(hot, axis=0, keepdims=True)
    count_ref[...] = cnt_sc[...]

    route = jnp.zeros(logits.shape, F32)
    for col, val in enumerate((e1, e2, w1, w2, r1, r2)):
        route = jnp.where(lane == col, val, route)
    route_ref[...] = route


def _mix_out(x, y_tm, o_first, o_second, gates, mod, w_glu, b_glu, w_bs, w_ba, w_out, g2,
             w_router, b_router, blk_seq, n_groups, per_group):
    t, d = x.shape
    tb = TOKEN_BLOCK
    blocks_first = o_first.shape[0] // tb
    row = lambda i, s: (i, 0)
    const = lambda i, s: (0, 0)
    grid_spec = pltpu.PrefetchScalarGridSpec(
        num_scalar_prefetch=1,
        grid=(t // tb,),
        in_specs=[
            pl.BlockSpec((tb, d), row),
            pl.BlockSpec((SSM_CHUNK, CHUNKS_PER_BLOCK, y_tm.shape[2]), lambda i, s: (0, i, 0)),
            pl.BlockSpec((tb, o_first.shape[1]), lambda i, s: (jnp.minimum(i, blocks_first - 1), 0)),
            pl.BlockSpec((tb, o_second.shape[1]), lambda i, s: (jnp.maximum(i - blocks_first, 0), 0)),
            pl.BlockSpec((tb, gates.shape[1]), row),
            pl.BlockSpec((1,) + mod.shape[1:], lambda i, s: (s[i], 0, 0)),
            pl.BlockSpec(w_glu.shape, const),
            pl.BlockSpec(b_glu.shape, const),
            pl.BlockSpec(w_bs.shape, const),
            pl.BlockSpec(w_ba.shape, const),
            pl.BlockSpec(w_out.shape, const),
            pl.BlockSpec(g2.shape, const),
            pl.BlockSpec(w_router.shape, const),
            pl.BlockSpec(b_router.shape, const),
            pl.BlockSpec((tb, tb), const),
        ],
        out_specs=[
            pl.BlockSpec((tb, d), row),
            pl.BlockSpec((tb, d), row),
            pl.BlockSpec((tb, LANES), row),
            pl.BlockSpec((SUBLANES, LANES), const),
        ],
        scratch_shapes=[pltpu.VMEM((SUBLANES, LANES), F32)],
    )
    strictly_lower = jnp.asarray(np.tril(np.ones((tb, tb), np.float32), -1), BF16)
    return pl.pallas_call(
        functools.partial(_mix_out_kernel, n_groups=n_groups, per_group=per_group,
                          blocks_first=blocks_first),
        grid_spec=grid_spec,
        out_shape=[
            jax.ShapeDtypeStruct((t, d), F32),
            jax.ShapeDtypeStruct((t, d), BF16),
            jax.ShapeDtypeStruct((t, LANES), F32),
            jax.ShapeDtypeStruct((SUBLANES, LANES), F32),
        ],
        compiler_params=_cparams(("arbitrary",)),
        name="mix_out",
    )(blk_seq, x, y_tm, o_first, o_second, gates, mod, w_glu, b_glu, w_bs, w_ba, w_out, g2,
      w_router, b_router, strictly_lower)


def _expert_kernel(te_ref, nt_ref, xs_ref, wg_ref, wu_ref, wd_ref, ys_ref):
    del te_ref
    live = pl.program_id(0) < nt_ref[0]

    @pl.when(live)
    def _():
        x = xs_ref[...]
        hg = jnp.dot(x, wg_ref[0].astype(BF16), preferred_element_type=F32)
        hu = jnp.dot(x, wu_ref[0].astype(BF16), preferred_element_type=F32)
        act = (hg * _sigmoid(hg) * hu).astype(BF16)
        ys_ref[...] = jnp.dot(act, wd_ref[0].astype(BF16),
                              preferred_element_type=F32).astype(ys_ref.dtype)

    @pl.when(jnp.logical_not(live))
    def _():
        ys_ref[...] = jnp.zeros(ys_ref.shape, ys_ref.dtype)


def _experts(xs, tile_expert, n_live, w_gate, w_up, w_down):
    p, d = xs.shape
    tm = EXPERT_TILE
    f = w_gate.shape[2]
    grid_spec = pltpu.PrefetchScalarGridSpec(
        num_scalar_prefetch=2,
        grid=(p // tm,),
        in_specs=[
            pl.BlockSpec((tm, d), lambda i, te, nt: (i, 0)),
            pl.BlockSpec((1, d, f), lambda i, te, nt: (te[i], 0, 0)),
            pl.BlockSpec((1, d, f), lambda i, te, nt: (te[i], 0, 0)),
            pl.BlockSpec((1, f, d), lambda i, te, nt: (te[i], 0, 0)),
        ],
        out_specs=pl.BlockSpec((tm, d), lambda i, te, nt: (i, 0)),
    )
    return pl.pallas_call(
        _expert_kernel,
        grid_spec=grid_spec,
        out_shape=jax.ShapeDtypeStruct((p, d), BF16),
        compiler_params=_cparams(("arbitrary",)),
        name="experts",
    )(tile_expert, n_live, xs, w_gate, w_up, w_down)


def _combine_kernel(seq_ref, x_ref, y1_ref, y2_ref, route_ref, mod_ref, o_ref):
    del seq_ref
    route = route_ref[...]
    w1 = route[:, 2:3]
    w2 = route[:, 3:4]
    moe = w1 * y1_ref[...].astype(F32) + w2 * y2_ref[...].astype(F32)
    o_ref[...] = x_ref[...] + mod_ref[0][5:6] * moe


def _combine(x1, y1, y2, route, mod, blk_seq):
    t, d = x1.shape
    tb = TOKEN_BLOCK
    row = lambda i, s: (i, 0)
    grid_spec = pltpu.PrefetchScalarGridSpec(
        num_scalar_prefetch=1,
        grid=(t // tb,),
        in_specs=[
            pl.BlockSpec((tb, d), row),
            pl.BlockSpec((tb, d), row),
            pl.BlockSpec((tb, d), row),
            pl.BlockSpec((tb, LANES), row),
            pl.BlockSpec((1,) + mod.shape[1:], lambda i, s: (s[i], 0, 0)),
        ],
        out_specs=pl.BlockSpec((tb, d), row),
    )
    return pl.pallas_call(
        _combine_kernel,
        grid_spec=grid_spec,
        out_shape=jax.ShapeDtypeStruct((t, d), F32),
        compiler_params=_cparams(("arbitrary",)),
        name="moe_combine",
    )(blk_seq, x1, y1, y2, route, mod)


def _dispatch_plan(route, counts, n_experts):
    t = route.shape[0]
    tm = EXPERT_TILE
    eid = route[:, :FINE_TOP_K].astype(jnp.int32).reshape(-1)
    rank = route[:, 2 * FINE_TOP_K:3 * FINE_TOP_K].astype(jnp.int32).reshape(-1)
    counts = counts[0, :n_experts].astype(jnp.int32)
    padded = ((counts + tm - 1) // tm) * tm
    pad_end = jnp.cumsum(padded)
    pad_off = pad_end - padded
    dest = pad_off[eid] + rank
    n_rows = FINE_TOP_K * t + n_experts * tm
    n_tiles = n_rows // tm
    src_tok = (jnp.arange(n_rows, dtype=jnp.int32) % t).at[dest].set(
        jnp.arange(FINE_TOP_K * t, dtype=jnp.int32) // FINE_TOP_K,
        unique_indices=True, mode='promise_in_bounds')
    tile_start = jnp.arange(n_tiles, dtype=jnp.int32) * tm
    tile_expert = jnp.minimum(jnp.sum((pad_end[None, :] <= tile_start[:, None]).astype(jnp.int32), axis=1),
                              n_experts - 1)
    n_live = (pad_end[-1] // tm).astype(jnp.int32).reshape(1)
    last_live = jnp.maximum(n_live[0] - 1, 0)
    tile_expert = jnp.where(tile_start // tm < n_live[0], tile_expert, tile_expert[last_live])
    return dest.reshape(t, FINE_TOP_K), src_tok, tile_expert, n_live


def _rope_tables(max_len):
    rows = max_len // GRID_W
    row = jnp.repeat(jnp.arange(rows, dtype=F32), GRID_W)
    col = jnp.tile(jnp.arange(GRID_W, dtype=F32), rows)
    half = HEAD_DIM // 2
    inv_freq = 1.0 / (ROPE_THETA ** (jnp.arange(0, half, 2, dtype=F32) / half))
    ang_r = row[:, None] * inv_freq[None, :]
    ang_c = col[:, None] * inv_freq[None, :]
    ang = jnp.concatenate([ang_r, ang_r, ang_c, ang_c], axis=-1)
    sign = jnp.where((jnp.arange(HEAD_DIM) % (HEAD_DIM // 2)) < HEAD_DIM // 4, -1.0, 1.0)
    reps = LANES // HEAD_DIM
    return jnp.tile(jnp.cos(ang), (1, reps)), jnp.tile(jnp.sin(ang) * sign[None, :], (1, reps))


def _to_block_order(a):
    n, d = a.shape
    return a.reshape(n // TOKEN_BLOCK, CHUNKS_PER_BLOCK, SSM_CHUNK, d).transpose(0, 2, 1, 3).reshape(n, d)


def _from_block_order(a):
    n, d = a.shape
    return a.reshape(n // TOKEN_BLOCK, SSM_CHUNK, CHUNKS_PER_BLOCK, d).transpose(0, 2, 1, 3).reshape(n, d)


def kernel(x_prompt, x_sample, c_prompt, c_sample, w_ada, b_ada, g_norm1, g_norm2, w_in, ssm_a_re, ssm_a_im, ssm_log_dt, ssm_b_re, ssm_b_im, ssm_c_re, ssm_c_im, ssm_d, w_glu, b_glu, g_q, g_k, w_branch_ssm, w_branch_attn, w_out, w_coarse, b_coarse, w_fine, b_fine, w_expert_gate, w_expert_up, w_expert_down):
    b1, l1, d = x_prompt.shape
    b2, l2, _ = x_sample.shape
    depth = w_in.shape[0]
    d_ssm = w_glu.shape[-1]
    d_attn = w_branch_attn.shape[1]
    d_kv = N_KV_HEADS * HEAD_DIM
    n_heads = d_attn // HEAD_DIM
    n_groups = w_coarse.shape[-1]
    n_experts = w_fine.shape[-1]
    per_group = n_experts // n_groups
    n_ssm_groups = ssm_b_re.shape[2]
    ssm_h = ssm_b_re.shape[-1]
    tb = TOKEN_BLOCK
    assert l1 % tb == 0 and l2 % tb == 0 and l1 % KV_TILE == 0 and l2 % KV_TILE == 0
    assert d_attn == N_KV_HEADS * (n_heads // N_KV_HEADS) * HEAD_DIM and 2 * HEAD_DIM == LANES
    assert n_ssm_groups * ssm_h == d_ssm and d_ssm % LANES == 0 and LANES % ssm_h == 0
    assert 2 * ssm_a_re.shape[-1] == LANES
    assert n_experts + n_groups <= LANES

    seq_lens = [l1] * b1 + [l2] * b2
    t = sum(seq_lens)
    x = jnp.concatenate([x_prompt.reshape(b1 * l1, d), x_sample.reshape(b2 * l2, d)], axis=0)
    x = _to_block_order(x)
    c = jnp.concatenate([c_prompt, c_sample], axis=0)
    blk_seq = np.concatenate([np.full(n // tb, s, np.int32) for s, n in enumerate(seq_lens)])
    blk_pos = np.concatenate([np.arange(n // tb, dtype=np.int32) for n in seq_lens])
    blk_seq, blk_pos = jnp.asarray(blk_seq), jnp.asarray(blk_pos)
    cos_t, sin_t = (_to_block_order(tab) for tab in _rope_tables(max(l1, l2)))

    seg_rows = max(seq_lens) // SSM_CHUNK
    seg_seq_rows = []
    filled = 0
    for n in seq_lens:
        rows = n // SSM_CHUNK
        assert rows % SUBLANES == 0 and seg_rows % rows == 0
        if filled == 0:
            seg_seq_rows.append(rows)
        assert seg_seq_rows[-1] == rows
        filled = (filled + rows) % seg_rows
    assert filled == 0
    seg_seq_rows = jnp.asarray(np.array(seg_seq_rows, np.int32))

    rep = n_heads // N_KV_HEADS
    slot_head = np.array([g * rep + j for j in range(rep) for g in range(N_KV_HEADS)])
    col_perm = (slot_head[:, None] * HEAD_DIM + np.arange(HEAD_DIM)[None, :]).reshape(-1)
    in_perm = np.concatenate([np.arange(d_ssm), d_ssm + col_perm, np.arange(d_ssm + d_attn, w_in.shape[-1])])

    mod_all = _ada_modulation(c, w_ada, b_ada)
    gq_t = jnp.tile(g_q.astype(F32), (1, LANES // HEAD_DIM))
    gk_t = jnp.tile(g_k.astype(F32), (1, LANES // HEAD_DIM))
    head_mean = jnp.asarray(np.kron(np.eye(LANES // HEAD_DIM), np.full((HEAD_DIM, HEAD_DIM), 1.0 / HEAD_DIM)), BF16)
    w_router = jnp.concatenate([w_fine, w_coarse,
                                jnp.zeros((depth, d, LANES - n_experts - n_groups), F32)], -1).astype(BF16)
    b_router = jnp.concatenate([b_fine, b_coarse,
                                jnp.zeros((depth, LANES - n_experts - n_groups), F32)], -1)

    for i in range(depth):
        mod = mod_all[i]
        u_tm, q, kt, v, gates = _mix_in(
            x, mod, g_norm1[i][None], w_in[i][:, in_perm].astype(BF16), cos_t, sin_t,
            gq_t[i][None], gk_t[i][None], head_mean, blk_seq, blk_pos, (d_ssm, d_attn, d_kv))

        w_in_rows, lag_blocks, w_out_cols, ssm_a = _ssm_weights(
            ssm_a_re[i], ssm_a_im[i], ssm_log_dt[i], ssm_b_re[i], ssm_b_im[i],
            ssm_c_re[i], ssm_c_im[i], ssm_d[i])
        states = _ssm_states(u_tm, w_in_rows, ssm_a, seg_rows, seg_seq_rows, ssm_h)
        y_tm = _ssm_output(u_tm, states, lag_blocks, w_out_cols, ssm_h)

        o_first = _attention(q, kt, v, 0, b1, l1)
        o_second = _attention(q, kt, v, b1 * l1, b2, l2)

        w_ba = w_branch_attn[i][col_perm].astype(BF16)
        x1, h2, route, counts = _mix_out(
            x, y_tm, o_first, o_second, gates, mod, w_glu[i].astype(BF16), b_glu[i][None],
            w_branch_ssm[i].astype(BF16), w_ba, w_out[i].astype(BF16), g_norm2[i][None],
            w_router[i], b_router[i][None], blk_seq, n_groups, per_group)

        dest, src_tok, tile_expert, n_live = _dispatch_plan(route, counts, n_experts)
        xs = h2.at[src_tok].get(mode='promise_in_bounds')
        ys = _experts(xs, tile_expert, n_live, w_expert_gate[i], w_expert_up[i], w_expert_down[i])
        y1 = ys.at[dest[:, 0]].get(mode='promise_in_bounds')
        y2 = ys.at[dest[:, 1]].get(mode='promise_in_bounds')
        x = _combine(x1, y1, y2, route, mod, blk_seq)

    x = _from_block_order(x)
    y_prompt = x[:b1 * l1].reshape(b1, l1, d)
    y_sample = x[b1 * l1:].reshape(b2, l2, d)
    return (y_prompt, y_sample)
```

```python
import functools
import math

import jax
import jax.numpy as jnp
import numpy as np
from jax import lax
from jax.experimental import pallas as pl
from jax.experimental.pallas import tpu as pltpu

F32 = jnp.float32
BF16 = jnp.bfloat16

EPS = 1e-6
GRID_W = 64
ROPE_THETA = 10000.0
HEAD_DIM = 64
N_KV_HEADS = 2
FINE_TOP_K = 2
LOG2E = 1.4426950408889634
LANES = 128
SUBLANES = 8
SSM_CHUNK = 16

TOKEN_BLOCK = 512
CHUNKS_PER_BLOCK = TOKEN_BLOCK // SSM_CHUNK
Q_TILE = 256
KV_TILE = TOKEN_BLOCK
KV_STEP_TILES = 4
EXPERT_TILE = 512
SSM_ROW_BLOCK = 512
VMEM_LIMIT = 56 * 1024 * 1024


def _sigmoid(x):
    return 1.0 / (1.0 + jnp.exp(-x))


def _cparams(sem):
    return pltpu.CompilerParams(dimension_semantics=sem, vmem_limit_bytes=VMEM_LIMIT)


def _ada_kernel(c_ref, w_ref, b_ref, o_ref):
    c = c_ref[...]
    ca = c * _sigmoid(c)
    o_ref[0, 0] = jnp.dot(ca, w_ref[0], preferred_element_type=F32,
                          precision=lax.Precision.HIGHEST) + b_ref[0, 0]


def _ada_modulation(c, w_ada, b_ada):
    depth, d, _ = w_ada.shape
    nb = c.shape[0]
    n_mod = w_ada.shape[2] // d
    out = pl.pallas_call(
        _ada_kernel,
        grid=(depth, n_mod),
        in_specs=[
            pl.BlockSpec((nb, d), lambda l, j: (0, 0)),
            pl.BlockSpec((1, d, d), lambda l, j: (l, 0, j)),
            pl.BlockSpec((1, 1, 1, d), lambda l, j: (l, j, 0, 0)),
        ],
        out_specs=pl.BlockSpec((1, 1, nb, d), lambda l, j: (l, j, 0, 0)),
        out_shape=jax.ShapeDtypeStruct((depth, n_mod, nb, d), F32),
        compiler_params=_cparams(("arbitrary", "arbitrary")),
        name="ada_modulation",
    )(c, w_ada, b_ada.reshape(depth, n_mod, 1, d))
    return out.transpose(0, 2, 1, 3)


def _rope(x, cos, sin_signed, low_half):
    up = pltpu.roll(x, LANES - 16, axis=1)
    down = pltpu.roll(x, 16, axis=1)
    return x * cos + jnp.where(low_half, up, down) * sin_signed


def _moe_residual(x_ref, y1_ref, y2_ref, route_ref, mod_ref):
    route = route_ref[...]
    moe = route[:, 2:3] * y1_ref[...].astype(F32) + route[:, 3:4] * y2_ref[...].astype(F32)
    return x_ref[...] + mod_ref[0][5:6] * moe


def _mix_in_kernel(seq_ref, pos_ref, *refs, d_ssm, d_attn, d_kv, pending_moe):
    del seq_ref, pos_ref
    if pending_moe:
        x_ref, y1_ref, y2_ref, route_ref, prev_mod_ref = refs[:5]
        (mod_ref, g1_ref, w_ref, cos_ref, sin_ref, gq_ref, gk_ref, hm_ref,
         u_ref, q_ref, kt_ref, v_ref, gate_ref, x_out_ref) = refs[5:]
        x = _moe_residual(x_ref, y1_ref, y2_ref, route_ref, prev_mod_ref)
        x_out_ref[...] = x
    else:
        (x_ref, mod_ref, g1_ref, w_ref, cos_ref, sin_ref, gq_ref, gk_ref, hm_ref,
         u_ref, q_ref, kt_ref, v_ref, gate_ref) = refs
        x = x_ref[...]
    mod = mod_ref[0]
    xn = x * lax.rsqrt(jnp.mean(x * x, axis=-1, keepdims=True) + EPS) * g1_ref[...]
    h = (xn * (1.0 + mod[1:2]) + mod[0:1]).astype(BF16)

    def proj(lo, hi):
        return jnp.dot(h, w_ref[:, lo:hi], preferred_element_type=F32)

    u = proj(0, d_ssm).astype(BF16)
    n_c = u.shape[0] // SSM_CHUNK
    for t in range(SSM_CHUNK):
        u_ref[t] = u[t * n_c:(t + 1) * n_c]

    cos = cos_ref[...]
    sin_signed = sin_ref[...]
    lane = lax.broadcasted_iota(jnp.int32, cos.shape, 1)
    low_half = (lane % 32) < 16
    hm = hm_ref[...]

    def norm_rope(z, gain, scale):
        cols = []
        for j in range(z.shape[1] // LANES):
            zj = z[:, j * LANES:(j + 1) * LANES]
            msq = jnp.dot((zj * zj).astype(BF16), hm, preferred_element_type=F32)
            zn = zj * lax.rsqrt(msq + EPS) * gain
            cols.append(_rope(zn, cos, sin_signed, low_half) * scale)
        return jnp.concatenate(cols, axis=1) if len(cols) > 1 else cols[0]

    o = d_ssm
    q_ref[...] = norm_rope(proj(o, o + d_attn), gq_ref[...],
                           (HEAD_DIM ** -0.5) * LOG2E).astype(BF16)
    o += d_attn
    kt_ref[0] = norm_rope(proj(o, o + d_kv), gk_ref[...], 1.0).T.astype(BF16)
    o += d_kv
    v_ref[...] = proj(o, o + d_kv).astype(BF16)
    o += d_kv
    n_gate = gate_ref.shape[1]
    half = n_gate // 2
    gate_ref[:, :half] = _sigmoid(proj(o, o + half)).astype(BF16)
    gate_ref[:, half:] = _sigmoid(proj(o + half, o + n_gate)).astype(BF16)


def _mix_in(x, pending, mod, g1, w_in, cos_t, sin_t, gq_t, gk_t, hm, blk_seq, blk_pos, dims):
    t, d = x.shape
    d_ssm, d_attn, d_kv = dims
    n_gate = w_in.shape[1] - d_ssm - d_attn - 2 * d_kv
    tb = TOKEN_BLOCK
    row = lambda i, s, p: (i, 0)
    const = lambda i, s, p: (0, 0)
    by_seq = lambda i, s, p: (s[i], 0, 0)
    pending_specs, pending_args, x_out_spec, x_out_shape = [], [], [], []
    if pending is not None:
        y1, y2, route, prev_mod = pending
        pending_specs = [pl.BlockSpec((tb, d), row), pl.BlockSpec((tb, d), row),
                         pl.BlockSpec((tb, LANES), row), pl.BlockSpec((1,) + prev_mod.shape[1:], by_seq)]
        pending_args = [y1, y2, route, prev_mod]
        x_out_spec = [pl.BlockSpec((tb, d), row)]
        x_out_shape = [jax.ShapeDtypeStruct((t, d), F32)]
    grid_spec = pltpu.PrefetchScalarGridSpec(
        num_scalar_prefetch=2,
        grid=(t // tb,),
        in_specs=[pl.BlockSpec((tb, d), row)] + pending_specs + [
            pl.BlockSpec((1,) + mod.shape[1:], by_seq),
            pl.BlockSpec((1, d), const),
            pl.BlockSpec(w_in.shape, const),
            pl.BlockSpec((tb, LANES), lambda i, s, p: (p[i], 0)),
            pl.BlockSpec((tb, LANES), lambda i, s, p: (p[i], 0)),
            pl.BlockSpec((1, LANES), const),
            pl.BlockSpec((1, LANES), const),
            pl.BlockSpec((LANES, LANES), const),
        ],
        out_specs=[
            pl.BlockSpec((SSM_CHUNK, CHUNKS_PER_BLOCK, d_ssm), lambda i, s, p: (0, i, 0)),
            pl.BlockSpec((tb, d_attn), row),
            pl.BlockSpec((1, d_kv, tb), lambda i, s, p: (i, 0, 0)),
            pl.BlockSpec((tb, d_kv), row),
            pl.BlockSpec((tb, n_gate), row),
        ] + x_out_spec,
    )
    outs = pl.pallas_call(
        functools.partial(_mix_in_kernel, d_ssm=d_ssm, d_attn=d_attn, d_kv=d_kv,
                          pending_moe=pending is not None),
        grid_spec=grid_spec,
        out_shape=[
            jax.ShapeDtypeStruct((SSM_CHUNK, t // SSM_CHUNK, d_ssm), BF16),
            jax.ShapeDtypeStruct((t, d_attn), BF16),
            jax.ShapeDtypeStruct((t // tb, d_kv, tb), BF16),
            jax.ShapeDtypeStruct((t, d_kv), BF16),
            jax.ShapeDtypeStruct((t, n_gate), BF16),
        ] + x_out_shape,
        compiler_params=_cparams(("arbitrary",)),
        name="mix_in",
    )(blk_seq, blk_pos, x, *pending_args, mod, g1, w_in, cos_t, sin_t, gq_t, gk_t, hm)
    x_now = outs[5] if pending is not None else x
    return (x_now,) + tuple(outs[:5])


def _ssm_weights(a_re, a_im, log_dt, b_re, b_im, c_re, c_im, d_skip):
    n_dirs, g, p = a_re.shape
    hch = b_re.shape[-1]
    c = SSM_CHUNK
    nq = g * hch // LANES
    gq = g // nq
    lam = lax.complex(a_re.astype(F32), a_im.astype(F32))
    dt = jnp.exp(log_dt.astype(F32))[..., None]
    lam_dt = lam * dt
    coef = (jnp.exp(lam_dt) - 1.0) / lam
    steps = jnp.arange(c + 1, dtype=F32)
    pw = jnp.exp(lam_dt[:, None] * steps[None, :, None, None])
    bc = lax.complex(b_re.astype(F32), b_im.astype(F32)) * coef[..., None]
    cc = lax.complex(c_re.astype(F32), c_im.astype(F32))
    eye = jnp.eye(gq, dtype=F32)

    kern = jnp.einsum('rghp,rdgp,rgpk->rdghk', cc, pw[:, :c], bc).real
    dmat = jnp.eye(hch, dtype=F32)[None] * d_skip.astype(F32).reshape(g, hch)[:, :, None]
    lag0 = (kern[0, 0] + kern[1, 0] + dmat)[None]
    by_lag = jnp.concatenate([kern[1, 1:][::-1], lag0, kern[0, 1:]], 0)
    by_lag = by_lag.reshape(2 * c - 1, nq, gq, hch, hch)
    lag_blocks = jnp.einsum('dqkoh,gk->qdghko', by_lag, eye).reshape(nq, 2 * c - 1, LANES, LANES)

    win_f = pw[0, c - 1 - jnp.arange(c)][:, :, None, :] * bc[0].transpose(0, 2, 1)[None]
    win_b = pw[1, jnp.arange(c)][:, :, None, :] * bc[1].transpose(0, 2, 1)[None]

    def in_rows(w):
        parts = jnp.stack([w.real, w.imag], 0).reshape(2, c, nq, gq, hch, p)
        parts = parts.transpose(2, 0, 1, 3, 4, 5).reshape(nq, 2, c * gq * hch, p)
        return jnp.concatenate([parts, parts], -1)

    w_in_rows = jnp.stack([in_rows(win_f), in_rows(win_b)], 1)

    cout_f = cc[0][None] * pw[0, 1 + jnp.arange(c)][:, :, None, :]
    cout_b = cc[1][None] * pw[1, c - jnp.arange(c)][:, :, None, :]

    def out_cols(w):
        parts = jnp.stack([w.real, -w.imag], 0).reshape(2, c, nq, gq, hch, p)
        return parts.transpose(2, 0, 5, 1, 3, 4).reshape(nq, 2, p, c * gq * hch)

    w_out_cols = jnp.concatenate([out_cols(cout_f), out_cols(cout_b)], 1)

    sub = SUBLANES
    rows = jnp.arange(sub)
    scan_pos = jnp.stack([rows, sub - 1 - rows], 0)
    a_log = lam_dt * c
    tables = []
    d = 1
    while d < sub:
        tables.append(jnp.where((scan_pos >= d)[:, :, None, None], jnp.exp(a_log * d)[:, None], 0.0))
        d *= 2
    tables.append(jnp.exp(a_log[:, None] * (scan_pos + 1).astype(F32)[:, :, None, None]))
    a_tab = jnp.stack(tables, 1)
    a_tab = jnp.stack([a_tab.real, a_tab.imag], 2).reshape(n_dirs, len(tables), 2, sub, nq, gq * p)
    a_tab = a_tab.transpose(4, 0, 1, 2, 3, 5)
    return w_in_rows.astype(BF16), lag_blocks.astype(BF16), w_out_cols.astype(BF16), a_tab


def _group_of(index, per_group):
    return (index // per_group) % (LANES // per_group)


def _ssm_states_kernel(len_ref, u_ref, win_ref, a_ref, s_ref, w_sc, x_sc, s_sc, *, ssm_h):
    n_rows, width = x_sc.shape
    half = width // 2

    @pl.when(pl.program_id(2) == 0)
    def _():
        shape = win_ref.shape[3:]
        row_group = _group_of(lax.broadcasted_iota(jnp.int32, shape, 0), ssm_h)
        lane_half = lax.broadcasted_iota(jnp.int32, shape, 1) // (LANES // 2)
        for a in range(2):
            rows = win_ref[0, 0, a]
            for j in range(half // LANES):
                col = a * half + j * LANES
                w_sc[:, col:col + LANES] = jnp.where(row_group == 2 * j + lane_half, rows,
                                                     jnp.zeros_like(rows))

    ucat = jnp.concatenate([u_ref[t] for t in range(SSM_CHUNK)], axis=1)
    x_sc[...] = jnp.dot(ucat, w_sc[...], preferred_element_type=F32)
    sub = SUBLANES
    tile = (sub, half)
    row_id = lax.broadcasted_iota(jnp.int32, tile, 0)
    seq_rows = len_ref[pl.program_id(2)]
    n_tiles = n_rows // sub
    n_levels = a_ref.shape[2] - 1

    def cmul(c_re, c_im, z_re, z_im):
        return c_re * z_re - c_im * z_im, c_re * z_im + c_im * z_re

    def run(backward):
        def toward_later(z, dist):
            return pltpu.roll(z, sub - dist if backward else dist, axis=0)

        def body(i, carry):
            s_re, s_im = carry
            tile_i = n_tiles - 1 - i if backward else i
            row0 = pl.multiple_of(tile_i * sub, sub)
            first = (row0 + sub) if backward else row0
            keep = jnp.where(lax.rem(first, seq_rows) == 0, 0.0, 1.0).astype(F32)
            s_re = s_re * keep
            s_im = s_im * keep
            z_re = x_sc[pl.ds(row0, sub), 0:half]
            z_im = x_sc[pl.ds(row0, sub), half:width]
            for lvl in range(n_levels):
                m_re, m_im = cmul(a_ref[0, 0, lvl, 0], a_ref[0, 0, lvl, 1],
                                  toward_later(z_re, 2 ** lvl), toward_later(z_im, 2 ** lvl))
                z_re, z_im = z_re + m_re, z_im + m_im
            c_re, c_im = cmul(a_ref[0, 0, n_levels, 0], a_ref[0, 0, n_levels, 1], s_re, s_im)
            z_re, z_im = z_re + c_re, z_im + c_im
            entry = sub - 1 if backward else 0
            s_sc[pl.ds(row0, sub), 0:half] = jnp.where(row_id == entry, s_re, toward_later(z_re, 1))
            s_sc[pl.ds(row0, sub), half:width] = jnp.where(row_id == entry, s_im, toward_later(z_im, 1))
            last = sub - 1 - entry
            return (jnp.broadcast_to(z_re[last:last + 1], tile),
                    jnp.broadcast_to(z_im[last:last + 1], tile))
        zero = jnp.zeros(tile, F32)
        lax.fori_loop(0, n_tiles, body, (zero, zero))

    direction = pl.program_id(1)

    @pl.when(direction == 0)
    def _():
        run(False)

    @pl.when(direction == 1)
    def _():
        run(True)

    s_ref[...] = s_sc[...].astype(s_ref.dtype)


def _ssm_states(u_tm, w_in_rows, a_tab, seg_rows, seg_seq_rows, ssm_h):
    c, nc, d_ssm = u_tm.shape
    nq, n_dirs, _, k, two_p = w_in_rows.shape
    assert a_tab.shape[-2] == SUBLANES
    n = a_tab.shape[-1] * 2
    assert two_p == LANES and k == c * LANES
    grid_spec = pltpu.PrefetchScalarGridSpec(
        num_scalar_prefetch=1,
        grid=(nq, n_dirs, nc // seg_rows),
        in_specs=[
            pl.BlockSpec((c, seg_rows, LANES), lambda q, r, s, ln: (0, s, q)),
            pl.BlockSpec((1, 1, 2, k, two_p), lambda q, r, s, ln: (q, r, 0, 0, 0)),
            pl.BlockSpec((1, 1) + a_tab.shape[2:], lambda q, r, s, ln: (q, r, 0, 0, 0, 0)),
        ],
        out_specs=pl.BlockSpec((seg_rows, n), lambda q, r, s, ln: (s, q * n_dirs + r)),
        scratch_shapes=[pltpu.VMEM((k, n), BF16), pltpu.VMEM((seg_rows, n), F32),
                        pltpu.VMEM((seg_rows, n), F32)],
    )
    return pl.pallas_call(
        functools.partial(_ssm_states_kernel, ssm_h=ssm_h),
        grid_spec=grid_spec,
        out_shape=jax.ShapeDtypeStruct((nc, nq * n_dirs * n), BF16),
        compiler_params=_cparams(("arbitrary", "arbitrary", "arbitrary")),
        name="ssm_states",
    )(seg_seq_rows, u_tm, w_in_rows, a_tab)


def _ssm_output_kernel(u_ref, s_ref, lag_ref, wcol_ref, y_ref, wintra_sc, wout_sc, *, ssm_h):
    n_t_out = y_ref.shape[0]

    @pl.when(pl.program_id(2) == 0)
    def _():
        t_out0 = pl.program_id(1) * n_t_out
        for t_in in range(SSM_CHUNK):
            for j in range(n_t_out):
                wintra_sc[t_in * LANES:(t_in + 1) * LANES, j * LANES:(j + 1) * LANES] = (
                    lag_ref[0, t_out0 + j - t_in + SSM_CHUNK - 1])
        n_planes, p, n_cols = wcol_ref.shape[1:]
        col_group = _group_of(lax.broadcasted_iota(jnp.int32, (p, n_cols), 1), ssm_h)
        for r in range(n_planes):
            cols = wcol_ref[0, r]
            for k in range(LANES // ssm_h):
                row = (r * (LANES // ssm_h) + k) * p
                wout_sc[row:row + p, :] = jnp.where(col_group == k, cols, jnp.zeros_like(cols))

    ucat = jnp.concatenate([u_ref[t] for t in range(SSM_CHUNK)], axis=1)
    y = jnp.dot(ucat, wintra_sc[...], preferred_element_type=F32)
    y += jnp.dot(s_ref[...], wout_sc[...], preferred_element_type=F32)
    for t in range(n_t_out):
        y_ref[t] = y[:, t * LANES:(t + 1) * LANES].astype(y_ref.dtype)


def _ssm_output(u_tm, s, lag_blocks, w_out_cols, ssm_h):
    c, nc, d_ssm = u_tm.shape
    nq, n_planes, p, k = w_out_cols.shape
    n_half = 2
    n = k // n_half
    s_width = n_planes * (LANES // ssm_h) * p
    rb = min(SSM_ROW_BLOCK, nc)
    return pl.pallas_call(
        functools.partial(_ssm_output_kernel, ssm_h=ssm_h),
        grid=(nq, n_half, nc // rb),
        in_specs=[
            pl.BlockSpec((c, rb, LANES), lambda q, h, i: (0, i, q)),
            pl.BlockSpec((rb, s_width), lambda q, h, i: (i, q)),
            pl.BlockSpec((1,) + lag_blocks.shape[1:], lambda q, h, i: (q, 0, 0, 0)),
            pl.BlockSpec((1, n_planes, p, n), lambda q, h, i: (q, 0, 0, h)),
        ],
        out_specs=pl.BlockSpec((c // n_half, rb, LANES), lambda q, h, i: (h, i, q)),
        out_shape=jax.ShapeDtypeStruct((c, nc, d_ssm), BF16),
        scratch_shapes=[pltpu.VMEM((k, n), BF16), pltpu.VMEM((s_width, n), BF16)],
        compiler_params=_cparams(("arbitrary", "arbitrary", "arbitrary")),
        name="ssm_output",
    )(u_tm, s, lag_blocks, w_out_cols)


def _attn_kernel(q_ref, kt_ref, v_ref, o_ref, m_sc, l_sc, acc_sc, *, n_kv_tiles):
    tq = q_ref.shape[0]
    n_slices = q_ref.shape[1] // LANES
    q = jnp.concatenate([q_ref[:, j * LANES:(j + 1) * LANES] for j in range(n_slices)], axis=0)
    m_sc[...] = jnp.full(m_sc.shape, -jnp.inf, F32)
    l_sc[...] = jnp.zeros(l_sc.shape, F32)
    acc_sc[...] = jnp.zeros(acc_sc.shape, F32)
    step_tiles = min(KV_STEP_TILES, n_kv_tiles)
    kv_step_len = step_tiles * KV_TILE
    n_rep = kv_step_len // LANES
    lane_head = lax.broadcasted_iota(jnp.int32, acc_sc.shape, 1) // HEAD_DIM
    key_head = lax.broadcasted_iota(jnp.int32, (kt_ref.shape[1], kv_step_len), 0) // HEAD_DIM

    def kv_step(kb, carry):
        start = pl.multiple_of(kb * kv_step_len, kv_step_len)
        kt = jnp.concatenate([kt_ref[kb * step_tiles + i] for i in range(step_tiles)], axis=1)
        v = v_ref[pl.ds(start, kv_step_len), :]
        acc = acc_sc[...]
        for g in range(N_KV_HEADS):
            ktg = jnp.where(key_head == g, kt, jnp.zeros_like(kt))
            s = jnp.dot(q, ktg, preferred_element_type=F32)
            m_old = m_sc[g]
            m_new = jnp.maximum(m_old, jnp.max(s, axis=-1, keepdims=True))
            alpha = jnp.exp2(m_old - m_new)
            p = jnp.exp2(s - jnp.concatenate([m_new] * n_rep, axis=1))
            l_part = p[:, 0:LANES]
            for c in range(1, n_rep):
                l_part = l_part + p[:, c * LANES:(c + 1) * LANES]
            l_sc[g] = alpha * l_sc[g] + l_part
            m_sc[g] = m_new
            pv = jnp.dot(p.astype(BF16), v, preferred_element_type=F32)
            acc = jnp.where(lane_head == g, alpha * acc + pv, acc)
        acc_sc[...] = acc
        return carry

    lax.fori_loop(0, n_kv_tiles // step_tiles, kv_step, 0)
    inv = [1.0 / jnp.sum(l_sc[g], axis=-1, keepdims=True) for g in range(N_KV_HEADS)]
    out = acc_sc[...] * jnp.where(lane_head == 0, inv[0], inv[1])
    for j in range(n_slices):
        o_ref[:, j * LANES:(j + 1) * LANES] = out[j * tq:(j + 1) * tq].astype(o_ref.dtype)


def _attention(q, kt, v, first_token, n_seqs, seq_len):
    d_attn = q.shape[1]
    d_kv = v.shape[1]
    tq = min(Q_TILE, seq_len)
    q_tiles = seq_len // tq
    kv_tiles = seq_len // KV_TILE
    assert first_token % seq_len == 0
    seq0 = first_token // seq_len
    rows = (d_attn // LANES) * tq
    return pl.pallas_call(
        functools.partial(_attn_kernel, n_kv_tiles=kv_tiles),
        grid=(n_seqs, q_tiles),
        in_specs=[
            pl.BlockSpec((tq, d_attn), lambda i, j: ((seq0 + i) * q_tiles + j, 0)),
            pl.BlockSpec((kv_tiles, d_kv, KV_TILE), lambda i, j: (seq0 + i, 0, 0)),
            pl.BlockSpec((seq_len, d_kv), lambda i, j: (seq0 + i, 0)),
        ],
        out_specs=pl.BlockSpec((tq, d_attn), lambda i, j: (i * q_tiles + j, 0)),
        out_shape=jax.ShapeDtypeStruct((n_seqs * seq_len, d_attn), BF16),
        scratch_shapes=[pltpu.VMEM((N_KV_HEADS, rows, LANES), F32),
                        pltpu.VMEM((N_KV_HEADS, rows, LANES), F32),
                        pltpu.VMEM((rows, LANES), F32)],
        compiler_params=_cparams(("arbitrary", "arbitrary")),
        name="attention",
    )(q, kt, v)


def _mix_out_kernel(seq_ref, x_ref, y_ref, o1_ref, o2_ref, gate_ref, mod_ref, wglu_ref, bglu_ref,
                    wbs_ref, wba_ref, wout_ref, g2_ref, wr_ref, br_ref, ltri_ref,
                    x1_ref, h2_ref, route_ref, count_ref, cnt_sc,
                    *, n_groups, per_group, blocks_first):
    del seq_ref
    mod = mod_ref[0]
    y = jnp.concatenate([y_ref[t] for t in range(SSM_CHUNK)], axis=0).astype(F32)
    y = 0.5 * y * (1.0 + jnp.tanh(math.sqrt(2.0 / math.pi) * (y + 0.044715 * (y * y * y))))
    glu = jnp.dot(y.astype(BF16), wglu_ref[...], preferred_element_type=F32) + bglu_ref[...]
    ys = (y * _sigmoid(glu)).astype(BF16)
    d = x_ref.shape[1]
    gate = gate_ref[...].astype(F32)
    o = jnp.where(pl.program_id(0) < blocks_first, o1_ref[...], o2_ref[...])
    merged = gate[:, :d] * jnp.dot(ys, wbs_ref[...], preferred_element_type=F32)
    merged += gate[:, d:] * jnp.dot(o, wba_ref[...], preferred_element_type=F32)
    x1 = x_ref[...] + mod[2:3] * jnp.dot(merged.astype(BF16), wout_ref[...],
                                         preferred_element_type=F32)
    x1_ref[...] = x1
    xn = x1 * lax.rsqrt(jnp.mean(x1 * x1, axis=-1, keepdims=True) + EPS) * g2_ref[...]
    h2 = (xn * (1.0 + mod[4:5]) + mod[3:4]).astype(BF16)
    h2_ref[...] = h2

    n_exp = n_groups * per_group
    logits = jnp.dot(h2, wr_ref[...], preferred_element_type=F32) + br_ref[...]
    lane = lax.broadcasted_iota(jnp.int32, logits.shape, 1).astype(F32)
    neg = jnp.float32(-jnp.inf)
    big = jnp.float32(1 << 20)
    lc = jnp.where(lane >= n_exp, jnp.where(lane < n_exp + n_groups, logits, neg), neg)
    c_max = jnp.max(lc, axis=-1, keepdims=True)
    g_idx = jnp.min(jnp.where(lc == c_max, lane - n_exp, big), axis=-1, keepdims=True)
    pc_top = 1.0 / jnp.sum(jnp.exp(lc - c_max), axis=-1, keepdims=True)
    lf = jnp.where(lane >= g_idx * per_group,
                   jnp.where(lane < (g_idx + 1.0) * per_group, logits, neg), neg)
    f1 = jnp.max(lf, axis=-1, keepdims=True)
    e1 = jnp.min(jnp.where(lf == f1, lane, big), axis=-1, keepdims=True)
    lf2 = jnp.where(lane == e1, neg, lf)
    f2 = jnp.max(lf2, axis=-1, keepdims=True)
    e2 = jnp.min(jnp.where(lf2 == f2, lane, big), axis=-1, keepdims=True)
    r = jnp.exp(f2 - f1)
    w1 = pc_top / (1.0 + r)
    w2 = pc_top * r / (1.0 + r)

    @pl.when(pl.program_id(0) == 0)
    def _():
        cnt_sc[...] = jnp.zeros(cnt_sc.shape, F32)

    hot = jnp.where(lane == e1, 1.0, 0.0) + jnp.where(lane == e2, 1.0, 0.0)
    before = jnp.dot(ltri_ref[...], hot.astype(BF16), preferred_element_type=F32) + cnt_sc[0:1, :]
    r1 = jnp.sum(jnp.where(lane == e1, before, 0.0), axis=-1, keepdims=True)
    r2 = jnp.sum(jnp.where(lane == e2, before, 0.0), axis=-1, keepdims=True)
    cnt_sc[...] = cnt_sc[...] + jnp.sum(hot, axis=0, keepdims=True)
    count_ref[...] = cnt_sc[...]

    route = jnp.zeros(logits.shape, F32)
    for col, val in enumerate((e1, e2, w1, w2, r1, r2)):
        route = jnp.where(lane == col, val, route)
    route_ref[...] = route


def _mix_out(x, y_tm, o_first, o_second, gates, mod, w_glu, b_glu, w_bs, w_ba, w_out, g2,
             w_router, b_router, blk_seq, n_groups, per_group):
    t, d = x.shape
    tb = TOKEN_BLOCK
    blocks_first = o_first.shape[0] // tb
    row = lambda i, s: (i, 0)
    const = lambda i, s: (0, 0)
    grid_spec = pltpu.PrefetchScalarGridSpec(
        num_scalar_prefetch=1,
        grid=(t // tb,),
        in_specs=[
            pl.BlockSpec((tb, d), row),
            pl.BlockSpec((SSM_CHUNK, CHUNKS_PER_BLOCK, y_tm.shape[2]), lambda i, s: (0, i, 0)),
            pl.BlockSpec((tb, o_first.shape[1]), lambda i, s: (jnp.minimum(i, blocks_first - 1), 0)),
            pl.BlockSpec((tb, o_second.shape[1]), lambda i, s: (jnp.maximum(i - blocks_first, 0), 0)),
            pl.BlockSpec((tb, gates.shape[1]), row),
            pl.BlockSpec((1,) + mod.shape[1:], lambda i, s: (s[i], 0, 0)),
            pl.BlockSpec(w_glu.shape, const),
            pl.BlockSpec(b_glu.shape, const),
            pl.BlockSpec(w_bs.shape, const),
            pl.BlockSpec(w_ba.shape, const),
            pl.BlockSpec(w_out.shape, const),
            pl.BlockSpec(g2.shape, const),
            pl.BlockSpec(w_router.shape, const),
            pl.BlockSpec(b_router.shape, const),
            pl.BlockSpec((tb, tb), const),
        ],
        out_specs=[
            pl.BlockSpec((tb, d), row),
            pl.BlockSpec((tb, d), row),
            pl.BlockSpec((tb, LANES), row),
            pl.BlockSpec((SUBLANES, LANES), const),
        ],
        scratch_shapes=[pltpu.VMEM((SUBLANES, LANES), F32)],
    )
    strictly_lower = jnp.asarray(np.tril(np.ones((tb, tb), np.float32), -1), BF16)
    return pl.pallas_call(
        functools.partial(_mix_out_kernel, n_groups=n_groups, per_group=per_group,
                          blocks_first=blocks_first),
        grid_spec=grid_spec,
        out_shape=[
            jax.ShapeDtypeStruct((t, d), F32),
            jax.ShapeDtypeStruct((t, d), BF16),
            jax.ShapeDtypeStruct((t, LANES), F32),
            jax.ShapeDtypeStruct((SUBLANES, LANES), F32),
        ],
        compiler_params=_cparams(("arbitrary",)),
        name="mix_out",
    )(blk_seq, x, y_tm, o_first, o_second, gates, mod, w_glu, b_glu, w_bs, w_ba, w_out, g2,
      w_router, b_router, strictly_lower)


def _expert_kernel(te_ref, nt_ref, xs_ref, wg_ref, wu_ref, wd_ref, ys_ref):
    del te_ref
    live = pl.program_id(0) < nt_ref[0]

    @pl.when(live)
    def _():
        x = xs_ref[...]
        hg = jnp.dot(x, wg_ref[0].astype(BF16), preferred_element_type=F32)
        hu = jnp.dot(x, wu_ref[0].astype(BF16), preferred_element_type=F32)
        act = (hg * _sigmoid(hg) * hu).astype(BF16)
        ys_ref[...] = jnp.dot(act, wd_ref[0].astype(BF16),
                              preferred_element_type=F32).astype(ys_ref.dtype)

    @pl.when(jnp.logical_not(live))
    def _():
        ys_ref[...] = jnp.zeros(ys_ref.shape, ys_ref.dtype)


def _experts(xs, tile_expert, n_live, w_gate, w_up, w_down):
    p, d = xs.shape
    tm = EXPERT_TILE
    f = w_gate.shape[2]
    grid_spec = pltpu.PrefetchScalarGridSpec(
        num_scalar_prefetch=2,
        grid=(p // tm,),
        in_specs=[
            pl.BlockSpec((tm, d), lambda i, te, nt: (i, 0)),
            pl.BlockSpec((1, d, f), lambda i, te, nt: (te[i], 0, 0)),
            pl.BlockSpec((1, d, f), lambda i, te, nt: (te[i], 0, 0)),
            pl.BlockSpec((1, f, d), lambda i, te, nt: (te[i], 0, 0)),
        ],
        out_specs=pl.BlockSpec((tm, d), lambda i, te, nt: (i, 0)),
    )
    return pl.pallas_call(
        _expert_kernel,
        grid_spec=grid_spec,
        out_shape=jax.ShapeDtypeStruct((p, d), BF16),
        compiler_params=_cparams(("arbitrary",)),
        name="experts",
    )(tile_expert, n_live, xs, w_gate, w_up, w_down)


def _combine_kernel(seq_ref, x_ref, y1_ref, y2_ref, route_ref, mod_ref, o_ref):
    del seq_ref
    o_ref[...] = _moe_residual(x_ref, y1_ref, y2_ref, route_ref, mod_ref)


def _combine(x1, y1, y2, route, mod, blk_seq):
    t, d = x1.shape
    tb = TOKEN_BLOCK
    row = lambda i, s: (i, 0)
    grid_spec = pltpu.PrefetchScalarGridSpec(
        num_scalar_prefetch=1,
        grid=(t // tb,),
        in_specs=[
            pl.BlockSpec((tb, d), row),
            pl.BlockSpec((tb, d), row),
            pl.BlockSpec((tb, d), row),
            pl.BlockSpec((tb, LANES), row),
            pl.BlockSpec((1,) + mod.shape[1:], lambda i, s: (s[i], 0, 0)),
        ],
        out_specs=pl.BlockSpec((tb, d), row),
    )
    return pl.pallas_call(
        _combine_kernel,
        grid_spec=grid_spec,
        out_shape=jax.ShapeDtypeStruct((t, d), F32),
        compiler_params=_cparams(("arbitrary",)),
        name="moe_combine",
    )(blk_seq, x1, y1, y2, route, mod)


def _dispatch_plan(route, counts, n_experts):
    t = route.shape[0]
    tm = EXPERT_TILE
    eid = route[:, :FINE_TOP_K].astype(jnp.int32).reshape(-1)
    rank = route[:, 2 * FINE_TOP_K:3 * FINE_TOP_K].astype(jnp.int32).reshape(-1)
    counts = counts[0, :n_experts].astype(jnp.int32)
    padded = ((counts + tm - 1) // tm) * tm
    pad_end = jnp.cumsum(padded)
    pad_off = pad_end - padded
    dest = pad_off[eid] + rank
    n_rows = FINE_TOP_K * t + n_experts * tm
    n_tiles = n_rows // tm
    src_tok = (jnp.arange(n_rows, dtype=jnp.int32) % t).at[dest].set(
        jnp.arange(FINE_TOP_K * t, dtype=jnp.int32) // FINE_TOP_K,
        unique_indices=True, mode='promise_in_bounds')
    tile_start = jnp.arange(n_tiles, dtype=jnp.int32) * tm
    tile_expert = jnp.minimum(jnp.sum((pad_end[None, :] <= tile_start[:, None]).astype(jnp.int32), axis=1),
                              n_experts - 1)
    n_live = (pad_end[-1] // tm).astype(jnp.int32).reshape(1)
    last_live = jnp.maximum(n_live[0] - 1, 0)
    tile_expert = jnp.where(tile_start // tm < n_live[0], tile_expert, tile_expert[last_live])
    return dest.reshape(t, FINE_TOP_K), src_tok, tile_expert, n_live


def _rope_tables(max_len):
    rows = max_len // GRID_W
    row = jnp.repeat(jnp.arange(rows, dtype=F32), GRID_W)
    col = jnp.tile(jnp.arange(GRID_W, dtype=F32), rows)
    half = HEAD_DIM // 2
    inv_freq = 1.0 / (ROPE_THETA ** (jnp.arange(0, half, 2, dtype=F32) / half))
    ang_r = row[:, None] * inv_freq[None, :]
    ang_c = col[:, None] * inv_freq[None, :]
    ang = jnp.concatenate([ang_r, ang_r, ang_c, ang_c], axis=-1)
    sign = jnp.where((jnp.arange(HEAD_DIM) % (HEAD_DIM // 2)) < HEAD_DIM // 4, -1.0, 1.0)
    reps = LANES // HEAD_DIM
    return jnp.tile(jnp.cos(ang), (1, reps)), jnp.tile(jnp.sin(ang) * sign[None, :], (1, reps))


def _to_block_order(a):
    n, d = a.shape
    return a.reshape(n // TOKEN_BLOCK, CHUNKS_PER_BLOCK, SSM_CHUNK, d).transpose(0, 2, 1, 3).reshape(n, d)


def _from_block_order(a):
    n, d = a.shape
    return a.reshape(n // TOKEN_BLOCK, SSM_CHUNK, CHUNKS_PER_BLOCK, d).transpose(0, 2, 1, 3).reshape(n, d)


def kernel(x_prompt, x_sample, c_prompt, c_sample, w_ada, b_ada, g_norm1, g_norm2, w_in, ssm_a_re, ssm_a_im, ssm_log_dt, ssm_b_re, ssm_b_im, ssm_c_re, ssm_c_im, ssm_d, w_glu, b_glu, g_q, g_k, w_branch_ssm, w_branch_attn, w_out, w_coarse, b_coarse, w_fine, b_fine, w_expert_gate, w_expert_up, w_expert_down):
    b1, l1, d = x_prompt.shape
    b2, l2, _ = x_sample.shape
    depth = w_in.shape[0]
    d_ssm = w_glu.shape[-1]
    d_attn = w_branch_attn.shape[1]
    d_kv = N_KV_HEADS * HEAD_DIM
    n_heads = d_attn // HEAD_DIM
    n_groups = w_coarse.shape[-1]
    n_experts = w_fine.shape[-1]
    per_group = n_experts // n_groups
    n_ssm_groups = ssm_b_re.shape[2]
    ssm_h = ssm_b_re.shape[-1]
    tb = TOKEN_BLOCK
    assert l1 % tb == 0 and l2 % tb == 0 and l1 % KV_TILE == 0 and l2 % KV_TILE == 0
    assert d_attn == N_KV_HEADS * (n_heads // N_KV_HEADS) * HEAD_DIM and 2 * HEAD_DIM == LANES
    assert n_ssm_groups * ssm_h == d_ssm and d_ssm % LANES == 0 and LANES % ssm_h == 0
    assert 2 * ssm_a_re.shape[-1] == LANES
    assert n_experts + n_groups <= LANES

    seq_lens = [l1] * b1 + [l2] * b2
    t = sum(seq_lens)
    x = jnp.concatenate([x_prompt.reshape(b1 * l1, d), x_sample.reshape(b2 * l2, d)], axis=0)
    x = _to_block_order(x)
    c = jnp.concatenate([c_prompt, c_sample], axis=0)
    blk_seq = np.concatenate([np.full(n // tb, s, np.int32) for s, n in enumerate(seq_lens)])
    blk_pos = np.concatenate([np.arange(n // tb, dtype=np.int32) for n in seq_lens])
    blk_seq, blk_pos = jnp.asarray(blk_seq), jnp.asarray(blk_pos)
    cos_t, sin_t = (_to_block_order(tab) for tab in _rope_tables(max(l1, l2)))

    seg_rows = max(seq_lens) // SSM_CHUNK
    seg_seq_rows = []
    filled = 0
    for n in seq_lens:
        rows = n // SSM_CHUNK
        assert rows % SUBLANES == 0 and seg_rows % rows == 0
        if filled == 0:
            seg_seq_rows.append(rows)
        assert seg_seq_rows[-1] == rows
        filled = (filled + rows) % seg_rows
    assert filled == 0
    seg_seq_rows = jnp.asarray(np.array(seg_seq_rows, np.int32))

    rep = n_heads // N_KV_HEADS
    slot_head = np.array([g * rep + j for j in range(rep) for g in range(N_KV_HEADS)])
    col_perm = (slot_head[:, None] * HEAD_DIM + np.arange(HEAD_DIM)[None, :]).reshape(-1)
    in_perm = np.concatenate([np.arange(d_ssm), d_ssm + col_perm, np.arange(d_ssm + d_attn, w_in.shape[-1])])

    mod_all = _ada_modulation(c, w_ada, b_ada)
    gq_t = jnp.tile(g_q.astype(F32), (1, LANES // HEAD_DIM))
    gk_t = jnp.tile(g_k.astype(F32), (1, LANES // HEAD_DIM))
    head_mean = jnp.asarray(np.kron(np.eye(LANES // HEAD_DIM), np.full((HEAD_DIM, HEAD_DIM), 1.0 / HEAD_DIM)), BF16)
    w_router = jnp.concatenate([w_fine, w_coarse,
                                jnp.zeros((depth, d, LANES - n_experts - n_groups), F32)], -1).astype(BF16)
    b_router = jnp.concatenate([b_fine, b_coarse,
                                jnp.zeros((depth, LANES - n_experts - n_groups), F32)], -1)

    pending = None
    for i in range(depth):
        mod = mod_all[i]
        x, u_tm, q, kt, v, gates = _mix_in(
            x, pending, mod, g_norm1[i][None], w_in[i][:, in_perm].astype(BF16), cos_t, sin_t,
            gq_t[i][None], gk_t[i][None], head_mean, blk_seq, blk_pos, (d_ssm, d_attn, d_kv))

        w_in_rows, lag_blocks, w_out_cols, ssm_a = _ssm_weights(
            ssm_a_re[i], ssm_a_im[i], ssm_log_dt[i], ssm_b_re[i], ssm_b_im[i],
            ssm_c_re[i], ssm_c_im[i], ssm_d[i])
        states = _ssm_states(u_tm, w_in_rows, ssm_a, seg_rows, seg_seq_rows, ssm_h)
        y_tm = _ssm_output(u_tm, states, lag_blocks, w_out_cols, ssm_h)

        o_first = _attention(q, kt, v, 0, b1, l1)
        o_second = _attention(q, kt, v, b1 * l1, b2, l2)

        w_ba = w_branch_attn[i][col_perm].astype(BF16)
        x, h2, route, counts = _mix_out(
            x, y_tm, o_first, o_second, gates, mod, w_glu[i].astype(BF16), b_glu[i][None],
            w_branch_ssm[i].astype(BF16), w_ba, w_out[i].astype(BF16), g_norm2[i][None],
            w_router[i], b_router[i][None], blk_seq, n_groups, per_group)

        dest, src_tok, tile_expert, n_live = _dispatch_plan(route, counts, n_experts)
        xs = h2.at[src_tok].get(mode='promise_in_bounds')
        ys = _experts(xs, tile_expert, n_live, w_expert_gate[i], w_expert_up[i], w_expert_down[i])
        y1 = ys.at[dest[:, 0]].get(mode='promise_in_bounds')
        y2 = ys.at[dest[:, 1]].get(mode='promise_in_bounds')
        pending = (y1, y2, route, mod)

    x = _combine(x, *pending, blk_seq)
    x = _from_block_order(x)
    y_prompt = x[:b1 * l1].reshape(b1, l1, d)
    y_sample = x[b1 * l1:].reshape(b2, l2, d)
    return (y_prompt, y_sample)
```

```python
import functools
import math

import jax
import jax.numpy as jnp
import numpy as np
from jax import lax
from jax.experimental import pallas as pl
from jax.experimental.pallas import tpu as pltpu

F32 = jnp.float32
BF16 = jnp.bfloat16

EPS = 1e-6
GRID_W = 64
ROPE_THETA = 10000.0
HEAD_DIM = 64
N_KV_HEADS = 2
FINE_TOP_K = 2
LOG2E = 1.4426950408889634
LANES = 128
SUBLANES = 8
SSM_CHUNK = 16

TOKEN_BLOCK = 512
CHUNKS_PER_BLOCK = TOKEN_BLOCK // SSM_CHUNK
Q_TILE = 256
KV_TILE = TOKEN_BLOCK
KV_STEP_TILES = 4
EXPERT_TILE = 1024
SSM_ROW_BLOCK = 512
VMEM_LIMIT = 56 * 1024 * 1024


def _sigmoid(x):
    return 1.0 / (1.0 + jnp.exp(-x))


def _cparams(sem):
    return pltpu.CompilerParams(dimension_semantics=sem, vmem_limit_bytes=VMEM_LIMIT)


def _ada_kernel(c_ref, w_ref, b_ref, o_ref):
    c = c_ref[...]
    ca = c * _sigmoid(c)
    o_ref[0, 0] = jnp.dot(ca, w_ref[0], preferred_element_type=F32,
                          precision=lax.Precision.HIGHEST) + b_ref[0, 0]


def _ada_modulation(c, w_ada, b_ada):
    depth, d, _ = w_ada.shape
    nb = c.shape[0]
    n_mod = w_ada.shape[2] // d
    out = pl.pallas_call(
        _ada_kernel,
        grid=(depth, n_mod),
        in_specs=[
            pl.BlockSpec((nb, d), lambda l, j: (0, 0)),
            pl.BlockSpec((1, d, d), lambda l, j: (l, 0, j)),
            pl.BlockSpec((1, 1, 1, d), lambda l, j: (l, j, 0, 0)),
        ],
        out_specs=pl.BlockSpec((1, 1, nb, d), lambda l, j: (l, j, 0, 0)),
        out_shape=jax.ShapeDtypeStruct((depth, n_mod, nb, d), F32),
        compiler_params=_cparams(("arbitrary", "arbitrary")),
        name="ada_modulation",
    )(c, w_ada, b_ada.reshape(depth, n_mod, 1, d))
    return out.transpose(0, 2, 1, 3)


def _rope(x, cos, sin_signed, low_half):
    up = pltpu.roll(x, LANES - 16, axis=1)
    down = pltpu.roll(x, 16, axis=1)
    return x * cos + jnp.where(low_half, up, down) * sin_signed


def _moe_residual(x_ref, y1_ref, y2_ref, route_ref, mod_ref):
    route = route_ref[...]
    moe = route[:, 2:3] * y1_ref[...].astype(F32) + route[:, 3:4] * y2_ref[...].astype(F32)
    return x_ref[...] + mod_ref[0][5:6] * moe


def _mix_in_kernel(seq_ref, pos_ref, *refs, d_ssm, d_attn, d_kv, pending_moe):
    del seq_ref, pos_ref
    if pending_moe:
        x_ref, y1_ref, y2_ref, route_ref, prev_mod_ref = refs[:5]
        (mod_ref, g1_ref, w_ref, cos_ref, sin_ref, gq_ref, gk_ref, hm_ref,
         u_ref, q_ref, kt_ref, v_ref, gate_ref, x_out_ref) = refs[5:]
        x = _moe_residual(x_ref, y1_ref, y2_ref, route_ref, prev_mod_ref)
        x_out_ref[...] = x
    else:
        (x_ref, mod_ref, g1_ref, w_ref, cos_ref, sin_ref, gq_ref, gk_ref, hm_ref,
         u_ref, q_ref, kt_ref, v_ref, gate_ref) = refs
        x = x_ref[...]
    mod = mod_ref[0]
    xn = x * lax.rsqrt(jnp.mean(x * x, axis=-1, keepdims=True) + EPS) * g1_ref[...]
    h = (xn * (1.0 + mod[1:2]) + mod[0:1]).astype(BF16)

    def proj(lo, hi):
        return jnp.dot(h, w_ref[:, lo:hi], preferred_element_type=F32)

    u = proj(0, d_ssm).astype(BF16)
    n_c = u.shape[0] // SSM_CHUNK
    for t in range(SSM_CHUNK):
        u_ref[t] = u[t * n_c:(t + 1) * n_c]

    cos = cos_ref[...]
    sin_signed = sin_ref[...]
    lane = lax.broadcasted_iota(jnp.int32, cos.shape, 1)
    low_half = (lane % 32) < 16
    hm = hm_ref[...]

    def norm_rope(z, gain, scale):
        cols = []
        for j in range(z.shape[1] // LANES):
            zj = z[:, j * LANES:(j + 1) * LANES]
            msq = jnp.dot((zj * zj).astype(BF16), hm, preferred_element_type=F32)
            zn = zj * lax.rsqrt(msq + EPS) * gain
            cols.append(_rope(zn, cos, sin_signed, low_half) * scale)
        return jnp.concatenate(cols, axis=1) if len(cols) > 1 else cols[0]

    o = d_ssm
    q_ref[...] = norm_rope(proj(o, o + d_attn), gq_ref[...],
                           (HEAD_DIM ** -0.5) * LOG2E).astype(BF16)
    o += d_attn
    kt_ref[0] = norm_rope(proj(o, o + d_kv), gk_ref[...], 1.0).T.astype(BF16)
    o += d_kv
    v_ref[...] = proj(o, o + d_kv).astype(BF16)
    o += d_kv
    n_gate = gate_ref.shape[1]
    half = n_gate // 2
    gate_ref[:, :half] = _sigmoid(proj(o, o + half)).astype(BF16)
    gate_ref[:, half:] = _sigmoid(proj(o + half, o + n_gate)).astype(BF16)


def _mix_in(x, pending, mod, g1, w_in, cos_t, sin_t, gq_t, gk_t, hm, blk_seq, blk_pos, dims):
    t, d = x.shape
    d_ssm, d_attn, d_kv = dims
    n_gate = w_in.shape[1] - d_ssm - d_attn - 2 * d_kv
    tb = TOKEN_BLOCK
    row = lambda i, s, p: (i, 0)
    const = lambda i, s, p: (0, 0)
    by_seq = lambda i, s, p: (s[i], 0, 0)
    pending_specs, pending_args, x_out_spec, x_out_shape = [], [], [], []
    if pending is not None:
        y1, y2, route, prev_mod = pending
        pending_specs = [pl.BlockSpec((tb, d), row), pl.BlockSpec((tb, d), row),
                         pl.BlockSpec((tb, LANES), row), pl.BlockSpec((1,) + prev_mod.shape[1:], by_seq)]
        pending_args = [y1, y2, route, prev_mod]
        x_out_spec = [pl.BlockSpec((tb, d), row)]
        x_out_shape = [jax.ShapeDtypeStruct((t, d), F32)]
    grid_spec = pltpu.PrefetchScalarGridSpec(
        num_scalar_prefetch=2,
        grid=(t // tb,),
        in_specs=[pl.BlockSpec((tb, d), row)] + pending_specs + [
            pl.BlockSpec((1,) + mod.shape[1:], by_seq),
            pl.BlockSpec((1, d), const),
            pl.BlockSpec(w_in.shape, const),
            pl.BlockSpec((tb, LANES), lambda i, s, p: (p[i], 0)),
            pl.BlockSpec((tb, LANES), lambda i, s, p: (p[i], 0)),
            pl.BlockSpec((1, LANES), const),
            pl.BlockSpec((1, LANES), const),
            pl.BlockSpec((LANES, LANES), const),
        ],
        out_specs=[
            pl.BlockSpec((SSM_CHUNK, CHUNKS_PER_BLOCK, d_ssm), lambda i, s, p: (0, i, 0)),
            pl.BlockSpec((tb, d_attn), row),
            pl.BlockSpec((1, d_kv, tb), lambda i, s, p: (i, 0, 0)),
            pl.BlockSpec((tb, d_kv), row),
            pl.BlockSpec((tb, n_gate), row),
        ] + x_out_spec,
    )
    outs = pl.pallas_call(
        functools.partial(_mix_in_kernel, d_ssm=d_ssm, d_attn=d_attn, d_kv=d_kv,
                          pending_moe=pending is not None),
        grid_spec=grid_spec,
        out_shape=[
            jax.ShapeDtypeStruct((SSM_CHUNK, t // SSM_CHUNK, d_ssm), BF16),
            jax.ShapeDtypeStruct((t, d_attn), BF16),
            jax.ShapeDtypeStruct((t // tb, d_kv, tb), BF16),
            jax.ShapeDtypeStruct((t, d_kv), BF16),
            jax.ShapeDtypeStruct((t, n_gate), BF16),
        ] + x_out_shape,
        compiler_params=_cparams(("arbitrary",)),
        name="mix_in",
    )(blk_seq, blk_pos, x, *pending_args, mod, g1, w_in, cos_t, sin_t, gq_t, gk_t, hm)
    x_now = outs[5] if pending is not None else x
    return (x_now,) + tuple(outs[:5])


def _ssm_weights(a_re, a_im, log_dt, b_re, b_im, c_re, c_im, d_skip):
    n_dirs, g, p = a_re.shape
    hch = b_re.shape[-1]
    c = SSM_CHUNK
    nq = g * hch // LANES
    gq = g // nq
    lam = lax.complex(a_re.astype(F32), a_im.astype(F32))
    dt = jnp.exp(log_dt.astype(F32))[..., None]
    lam_dt = lam * dt
    coef = (jnp.exp(lam_dt) - 1.0) / lam
    steps = jnp.arange(c + 1, dtype=F32)
    pw = jnp.exp(lam_dt[:, None] * steps[None, :, None, None])
    bc = lax.complex(b_re.astype(F32), b_im.astype(F32)) * coef[..., None]
    cc = lax.complex(c_re.astype(F32), c_im.astype(F32))
    eye = jnp.eye(gq, dtype=F32)

    kern = jnp.einsum('rghp,rdgp,rgpk->rdghk', cc, pw[:, :c], bc).real
    dmat = jnp.eye(hch, dtype=F32)[None] * d_skip.astype(F32).reshape(g, hch)[:, :, None]
    lag0 = (kern[0, 0] + kern[1, 0] + dmat)[None]
    by_lag = jnp.concatenate([kern[1, 1:][::-1], lag0, kern[0, 1:]], 0)
    by_lag = by_lag.reshape(2 * c - 1, nq, gq, hch, hch)
    lag_blocks = jnp.einsum('dqkoh,gk->qdghko', by_lag, eye).reshape(nq, 2 * c - 1, LANES, LANES)

    win_f = pw[0, c - 1 - jnp.arange(c)][:, :, None, :] * bc[0].transpose(0, 2, 1)[None]
    win_b = pw[1, jnp.arange(c)][:, :, None, :] * bc[1].transpose(0, 2, 1)[None]

    def in_rows(w):
        parts = jnp.stack([w.real, w.imag], 0).reshape(2, c, nq, gq, hch, p)
        parts = parts.transpose(2, 0, 1, 3, 4, 5).reshape(nq, 2, c * gq * hch, p)
        return jnp.concatenate([parts, parts], -1)

    w_in_rows = jnp.stack([in_rows(win_f), in_rows(win_b)], 1)

    cout_f = cc[0][None] * pw[0, 1 + jnp.arange(c)][:, :, None, :]
    cout_b = cc[1][None] * pw[1, c - jnp.arange(c)][:, :, None, :]

    def out_cols(w):
        parts = jnp.stack([w.real, -w.imag], 0).reshape(2, c, nq, gq, hch, p)
        return parts.transpose(2, 0, 5, 1, 3, 4).reshape(nq, 2, p, c * gq * hch)

    w_out_cols = jnp.concatenate([out_cols(cout_f), out_cols(cout_b)], 1)

    sub = SUBLANES
    rows = jnp.arange(sub)
    scan_pos = jnp.stack([rows, sub - 1 - rows], 0)
    a_log = lam_dt * c
    tables = []
    d = 1
    while d < sub:
        tables.append(jnp.where((scan_pos >= d)[:, :, None, None], jnp.exp(a_log * d)[:, None], 0.0))
        d *= 2
    tables.append(jnp.exp(a_log[:, None] * (scan_pos + 1).astype(F32)[:, :, None, None]))
    a_tab = jnp.stack(tables, 1)
    a_tab = jnp.stack([a_tab.real, a_tab.imag], 2).reshape(n_dirs, len(tables), 2, sub, nq, gq * p)
    a_tab = a_tab.transpose(4, 0, 1, 2, 3, 5)
    return w_in_rows.astype(BF16), lag_blocks.astype(BF16), w_out_cols.astype(BF16), a_tab


def _group_of(index, per_group):
    return (index // per_group) % (LANES // per_group)


def _ssm_states_kernel(len_ref, u_ref, win_ref, a_ref, s_ref, w_sc, x_sc, s_sc, *, ssm_h):
    n_rows, width = x_sc.shape
    half = width // 2

    @pl.when(pl.program_id(2) == 0)
    def _():
        shape = win_ref.shape[3:]
        row_group = _group_of(lax.broadcasted_iota(jnp.int32, shape, 0), ssm_h)
        lane_half = lax.broadcasted_iota(jnp.int32, shape, 1) // (LANES // 2)
        for a in range(2):
            rows = win_ref[0, 0, a]
            for j in range(half // LANES):
                col = a * half + j * LANES
                w_sc[:, col:col + LANES] = jnp.where(row_group == 2 * j + lane_half, rows,
                                                     jnp.zeros_like(rows))

    ucat = jnp.concatenate([u_ref[t] for t in range(SSM_CHUNK)], axis=1)
    x_sc[...] = jnp.dot(ucat, w_sc[...], preferred_element_type=F32)
    sub = SUBLANES
    tile = (sub, half)
    row_id = lax.broadcasted_iota(jnp.int32, tile, 0)
    seq_rows = len_ref[pl.program_id(2)]
    n_tiles = n_rows // sub
    n_levels = a_ref.shape[2] - 1

    def cmul(c_re, c_im, z_re, z_im):
        return c_re * z_re - c_im * z_im, c_re * z_im + c_im * z_re

    def run(backward):
        def toward_later(z, dist):
            return pltpu.roll(z, sub - dist if backward else dist, axis=0)

        def body(i, carry):
            s_re, s_im = carry
            tile_i = n_tiles - 1 - i if backward else i
            row0 = pl.multiple_of(tile_i * sub, sub)
            first = (row0 + sub) if backward else row0
            keep = jnp.where(lax.rem(first, seq_rows) == 0, 0.0, 1.0).astype(F32)
            s_re = s_re * keep
            s_im = s_im * keep
            z_re = x_sc[pl.ds(row0, sub), 0:half]
            z_im = x_sc[pl.ds(row0, sub), half:width]
            for lvl in range(n_levels):
                m_re, m_im = cmul(a_ref[0, 0, lvl, 0], a_ref[0, 0, lvl, 1],
                                  toward_later(z_re, 2 ** lvl), toward_later(z_im, 2 ** lvl))
                z_re, z_im = z_re + m_re, z_im + m_im
            c_re, c_im = cmul(a_ref[0, 0, n_levels, 0], a_ref[0, 0, n_levels, 1], s_re, s_im)
            z_re, z_im = z_re + c_re, z_im + c_im
            entry = sub - 1 if backward else 0
            s_sc[pl.ds(row0, sub), 0:half] = jnp.where(row_id == entry, s_re, toward_later(z_re, 1))
            s_sc[pl.ds(row0, sub), half:width] = jnp.where(row_id == entry, s_im, toward_later(z_im, 1))
            last = sub - 1 - entry
            return (jnp.broadcast_to(z_re[last:last + 1], tile),
                    jnp.broadcast_to(z_im[last:last + 1], tile))
        zero = jnp.zeros(tile, F32)
        lax.fori_loop(0, n_tiles, body, (zero, zero))

    direction = pl.program_id(1)

    @pl.when(direction == 0)
    def _():
        run(False)

    @pl.when(direction == 1)
    def _():
        run(True)

    s_ref[...] = s_sc[...].astype(s_ref.dtype)


def _ssm_states(u_tm, w_in_rows, a_tab, seg_rows, seg_seq_rows, ssm_h):
    c, nc, d_ssm = u_tm.shape
    nq, n_dirs, _, k, two_p = w_in_rows.shape
    assert a_tab.shape[-2] == SUBLANES
    n = a_tab.shape[-1] * 2
    assert two_p == LANES and k == c * LANES
    grid_spec = pltpu.PrefetchScalarGridSpec(
        num_scalar_prefetch=1,
        grid=(nq, n_dirs, nc // seg_rows),
        in_specs=[
            pl.BlockSpec((c, seg_rows, LANES), lambda q, r, s, ln: (0, s, q)),
            pl.BlockSpec((1, 1, 2, k, two_p), lambda q, r, s, ln: (q, r, 0, 0, 0)),
            pl.BlockSpec((1, 1) + a_tab.shape[2:], lambda q, r, s, ln: (q, r, 0, 0, 0, 0)),
        ],
        out_specs=pl.BlockSpec((seg_rows, n), lambda q, r, s, ln: (s, q * n_dirs + r)),
        scratch_shapes=[pltpu.VMEM((k, n), BF16), pltpu.VMEM((seg_rows, n), F32),
                        pltpu.VMEM((seg_rows, n), F32)],
    )
    return pl.pallas_call(
        functools.partial(_ssm_states_kernel, ssm_h=ssm_h),
        grid_spec=grid_spec,
        out_shape=jax.ShapeDtypeStruct((nc, nq * n_dirs * n), BF16),
        compiler_params=_cparams(("arbitrary", "arbitrary", "arbitrary")),
        name="ssm_states",
    )(seg_seq_rows, u_tm, w_in_rows, a_tab)


def _ssm_output_kernel(u_ref, s_ref, lag_ref, wcol_ref, y_ref, wintra_sc, wout_sc, *, ssm_h):
    n_t_out = y_ref.shape[0]

    @pl.when(pl.program_id(2) == 0)
    def _():
        t_out0 = pl.program_id(1) * n_t_out
        for t_in in range(SSM_CHUNK):
            for j in range(n_t_out):
                wintra_sc[t_in * LANES:(t_in + 1) * LANES, j * LANES:(j + 1) * LANES] = (
                    lag_ref[0, t_out0 + j - t_in + SSM_CHUNK - 1])
        n_planes, p, n_cols = wcol_ref.shape[1:]
        col_group = _group_of(lax.broadcasted_iota(jnp.int32, (p, n_cols), 1), ssm_h)
        for r in range(n_planes):
            cols = wcol_ref[0, r]
            for k in range(LANES // ssm_h):
                row = (r * (LANES // ssm_h) + k) * p
                wout_sc[row:row + p, :] = jnp.where(col_group == k, cols, jnp.zeros_like(cols))

    ucat = jnp.concatenate([u_ref[t] for t in range(SSM_CHUNK)], axis=1)
    y = jnp.dot(ucat, wintra_sc[...], preferred_element_type=F32)
    y += jnp.dot(s_ref[...], wout_sc[...], preferred_element_type=F32)
    for t in range(n_t_out):
        y_ref[t] = y[:, t * LANES:(t + 1) * LANES].astype(y_ref.dtype)


def _ssm_output(u_tm, s, lag_blocks, w_out_cols, ssm_h):
    c, nc, d_ssm = u_tm.shape
    nq, n_planes, p, k = w_out_cols.shape
    n_half = 2
    n = k // n_half
    s_width = n_planes * (LANES // ssm_h) * p
    rb = min(SSM_ROW_BLOCK, nc)
    return pl.pallas_call(
        functools.partial(_ssm_output_kernel, ssm_h=ssm_h),
        grid=(nq, n_half, nc // rb),
        in_specs=[
            pl.BlockSpec((c, rb, LANES), lambda q, h, i: (0, i, q)),
            pl.BlockSpec((rb, s_width), lambda q, h, i: (i, q)),
            pl.BlockSpec((1,) + lag_blocks.shape[1:], lambda q, h, i: (q, 0, 0, 0)),
            pl.BlockSpec((1, n_planes, p, n), lambda q, h, i: (q, 0, 0, h)),
        ],
        out_specs=pl.BlockSpec((c // n_half, rb, LANES), lambda q, h, i: (h, i, q)),
        out_shape=jax.ShapeDtypeStruct((c, nc, d_ssm), BF16),
        scratch_shapes=[pltpu.VMEM((k, n), BF16), pltpu.VMEM((s_width, n), BF16)],
        compiler_params=_cparams(("arbitrary", "arbitrary", "arbitrary")),
        name="ssm_output",
    )(u_tm, s, lag_blocks, w_out_cols)


def _attn_kernel(q_ref, kt_ref, v_ref, o_ref, m_sc, l_sc, acc_sc, *, n_kv_tiles):
    tq = q_ref.shape[0]
    n_slices = q_ref.shape[1] // LANES
    q = jnp.concatenate([q_ref[:, j * LANES:(j + 1) * LANES] for j in range(n_slices)], axis=0)
    m_sc[...] = jnp.full(m_sc.shape, -jnp.inf, F32)
    l_sc[...] = jnp.zeros(l_sc.shape, F32)
    acc_sc[...] = jnp.zeros(acc_sc.shape, F32)
    step_tiles = min(KV_STEP_TILES, n_kv_tiles)
    kv_step_len = step_tiles * KV_TILE
    n_rep = kv_step_len // LANES
    lane_head = lax.broadcasted_iota(jnp.int32, acc_sc.shape, 1) // HEAD_DIM
    key_head = lax.broadcasted_iota(jnp.int32, (kt_ref.shape[1], kv_step_len), 0) // HEAD_DIM

    def kv_step(kb, carry):
        start = pl.multiple_of(kb * kv_step_len, kv_step_len)
        kt = jnp.concatenate([kt_ref[kb * step_tiles + i] for i in range(step_tiles)], axis=1)
        v = v_ref[pl.ds(start, kv_step_len), :]
        acc = acc_sc[...]
        for g in range(N_KV_HEADS):
            ktg = jnp.where(key_head == g, kt, jnp.zeros_like(kt))
            s = jnp.dot(q, ktg, preferred_element_type=F32)
            m_old = m_sc[g]
            m_new = jnp.maximum(m_old, jnp.max(s, axis=-1, keepdims=True))
            alpha = jnp.exp2(m_old - m_new)
            p = jnp.exp2(s - jnp.concatenate([m_new] * n_rep, axis=1))
            l_part = p[:, 0:LANES]
            for c in range(1, n_rep):
                l_part = l_part + p[:, c * LANES:(c + 1) * LANES]
            l_sc[g] = alpha * l_sc[g] + l_part
            m_sc[g] = m_new
            pv = jnp.dot(p.astype(BF16), v, preferred_element_type=F32)
            acc = jnp.where(lane_head == g, alpha * acc + pv, acc)
        acc_sc[...] = acc
        return carry

    lax.fori_loop(0, n_kv_tiles // step_tiles, kv_step, 0)
    inv = [1.0 / jnp.sum(l_sc[g], axis=-1, keepdims=True) for g in range(N_KV_HEADS)]
    out = acc_sc[...] * jnp.where(lane_head == 0, inv[0], inv[1])
    for j in range(n_slices):
        o_ref[:, j * LANES:(j + 1) * LANES] = out[j * tq:(j + 1) * tq].astype(o_ref.dtype)


def _attention(q, kt, v, first_token, n_seqs, seq_len):
    d_attn = q.shape[1]
    d_kv = v.shape[1]
    tq = min(Q_TILE, seq_len)
    q_tiles = seq_len // tq
    kv_tiles = seq_len // KV_TILE
    assert first_token % seq_len == 0
    seq0 = first_token // seq_len
    rows = (d_attn // LANES) * tq
    return pl.pallas_call(
        functools.partial(_attn_kernel, n_kv_tiles=kv_tiles),
        grid=(n_seqs, q_tiles),
        in_specs=[
            pl.BlockSpec((tq, d_attn), lambda i, j: ((seq0 + i) * q_tiles + j, 0)),
            pl.BlockSpec((kv_tiles, d_kv, KV_TILE), lambda i, j: (seq0 + i, 0, 0)),
            pl.BlockSpec((seq_len, d_kv), lambda i, j: (seq0 + i, 0)),
        ],
        out_specs=pl.BlockSpec((tq, d_attn), lambda i, j: (i * q_tiles + j, 0)),
        out_shape=jax.ShapeDtypeStruct((n_seqs * seq_len, d_attn), BF16),
        scratch_shapes=[pltpu.VMEM((N_KV_HEADS, rows, LANES), F32),
                        pltpu.VMEM((N_KV_HEADS, rows, LANES), F32),
                        pltpu.VMEM((rows, LANES), F32)],
        compiler_params=_cparams(("arbitrary", "arbitrary")),
        name="attention",
    )(q, kt, v)


def _mix_out_kernel(seq_ref, x_ref, y_ref, o1_ref, o2_ref, gate_ref, mod_ref, wglu_ref, bglu_ref,
                    wbs_ref, wba_ref, wout_ref, g2_ref, wr_ref, br_ref, ltri_ref,
                    x1_ref, h2_ref, route_ref, count_ref, cnt_sc,
                    *, n_groups, per_group, blocks_first):
    del seq_ref
    mod = mod_ref[0]
    y = jnp.concatenate([y_ref[t] for t in range(SSM_CHUNK)], axis=0).astype(F32)
    y = 0.5 * y * (1.0 + jnp.tanh(math.sqrt(2.0 / math.pi) * (y + 0.044715 * (y * y * y))))
    glu = jnp.dot(y.astype(BF16), wglu_ref[...], preferred_element_type=F32) + bglu_ref[...]
    ys = (y * _sigmoid(glu)).astype(BF16)
    d = x_ref.shape[1]
    gate = gate_ref[...].astype(F32)
    o = jnp.where(pl.program_id(0) < blocks_first, o1_ref[...], o2_ref[...])
    merged = gate[:, :d] * jnp.dot(ys, wbs_ref[...], preferred_element_type=F32)
    merged += gate[:, d:] * jnp.dot(o, wba_ref[...], preferred_element_type=F32)
    x1 = x_ref[...] + mod[2:3] * jnp.dot(merged.astype(BF16), wout_ref[...],
                                         preferred_element_type=F32)
    x1_ref[...] = x1
    xn = x1 * lax.rsqrt(jnp.mean(x1 * x1, axis=-1, keepdims=True) + EPS) * g2_ref[...]
    h2 = (xn * (1.0 + mod[4:5]) + mod[3:4]).astype(BF16)
    h2_ref[...] = h2

    n_exp = n_groups * per_group
    logits = jnp.dot(h2, wr_ref[...], preferred_element_type=F32) + br_ref[...]
    lane = lax.broadcasted_iota(jnp.int32, logits.shape, 1).astype(F32)
    neg = jnp.float32(-jnp.inf)
    big = jnp.float32(1 << 20)
    lc = jnp.where(lane >= n_exp, jnp.where(lane < n_exp + n_groups, logits, neg), neg)
    c_max = jnp.max(lc, axis=-1, keepdims=True)
    g_idx = jnp.min(jnp.where(lc == c_max, lane - n_exp, big), axis=-1, keepdims=True)
    pc_top = 1.0 / jnp.sum(jnp.exp(lc - c_max), axis=-1, keepdims=True)
    lf = jnp.where(lane >= g_idx * per_group,
                   jnp.where(lane < (g_idx + 1.0) * per_group, logits, neg), neg)
    f1 = jnp.max(lf, axis=-1, keepdims=True)
    e1 = jnp.min(jnp.where(lf == f1, lane, big), axis=-1, keepdims=True)
    lf2 = jnp.where(lane == e1, neg, lf)
    f2 = jnp.max(lf2, axis=-1, keepdims=True)
    e2 = jnp.min(jnp.where(lf2 == f2, lane, big), axis=-1, keepdims=True)
    r = jnp.exp(f2 - f1)
    w1 = pc_top / (1.0 + r)
    w2 = pc_top * r / (1.0 + r)

    @pl.when(pl.program_id(0) == 0)
    def _():
        cnt_sc[...] = jnp.zeros(cnt_sc.shape, F32)

    hot = jnp.where(lane == e1, 1.0, 0.0) + jnp.where(lane == e2, 1.0, 0.0)
    before = jnp.dot(ltri_ref[...], hot.astype(BF16), preferred_element_type=F32) + cnt_sc[0:1, :]
    r1 = jnp.sum(jnp.where(lane == e1, before, 0.0), axis=-1, keepdims=True)
    r2 = jnp.sum(jnp.where(lane == e2, before, 0.0), axis=-1, keepdims=True)
    cnt_sc[...] = cnt_sc[...] + jnp.sum(hot, axis=0, keepdims=True)
    count_ref[...] = cnt_sc[...]

    route = jnp.zeros(logits.shape, F32)
    for col, val in enumerate((e1, e2, w1, w2, r1, r2)):
        route = jnp.where(lane == col, val, route)
    route_ref[...] = route


def _mix_out(x, y_tm, o_first, o_second, gates, mod, w_glu, b_glu, w_bs, w_ba, w_out, g2,
             w_router, b_router, blk_seq, n_groups, per_group):
    t, d = x.shape
    tb = TOKEN_BLOCK
    blocks_first = o_first.shape[0] // tb
    row = lambda i, s: (i, 0)
    const = lambda i, s: (0, 0)
    grid_spec = pltpu.PrefetchScalarGridSpec(
        num_scalar_prefetch=1,
        grid=(t // tb,),
        in_specs=[
            pl.BlockSpec((tb, d), row),
            pl.BlockSpec((SSM_CHUNK, CHUNKS_PER_BLOCK, y_tm.shape[2]), lambda i, s: (0, i, 0)),
            pl.BlockSpec((tb, o_first.shape[1]), lambda i, s: (jnp.minimum(i, blocks_first - 1), 0)),
            pl.BlockSpec((tb, o_second.shape[1]), lambda i, s: (jnp.maximum(i - blocks_first, 0), 0)),
            pl.BlockSpec((tb, gates.shape[1]), row),
            pl.BlockSpec((1,) + mod.shape[1:], lambda i, s: (s[i], 0, 0)),
            pl.BlockSpec(w_glu.shape, const),
            pl.BlockSpec(b_glu.shape, const),
            pl.BlockSpec(w_bs.shape, const),
            pl.BlockSpec(w_ba.shape, const),
            pl.BlockSpec(w_out.shape, const),
            pl.BlockSpec(g2.shape, const),
            pl.BlockSpec(w_router.shape, const),
            pl.BlockSpec(b_router.shape, const),
            pl.BlockSpec((tb, tb), const),
        ],
        out_specs=[
            pl.BlockSpec((tb, d), row),
            pl.BlockSpec((tb, d), row),
            pl.BlockSpec((tb, LANES), row),
            pl.BlockSpec((SUBLANES, LANES), const),
        ],
        scratch_shapes=[pltpu.VMEM((SUBLANES, LANES), F32)],
    )
    strictly_lower = jnp.asarray(np.tril(np.ones((tb, tb), np.float32), -1), BF16)
    return pl.pallas_call(
        functools.partial(_mix_out_kernel, n_groups=n_groups, per_group=per_group,
                          blocks_first=blocks_first),
        grid_spec=grid_spec,
        out_shape=[
            jax.ShapeDtypeStruct((t, d), F32),
            jax.ShapeDtypeStruct((t, d), BF16),
            jax.ShapeDtypeStruct((t, LANES), F32),
            jax.ShapeDtypeStruct((SUBLANES, LANES), F32),
        ],
        compiler_params=_cparams(("arbitrary",)),
        name="mix_out",
    )(blk_seq, x, y_tm, o_first, o_second, gates, mod, w_glu, b_glu, w_bs, w_ba, w_out, g2,
      w_router, b_router, strictly_lower)


def _expert_kernel(te_ref, nt_ref, xs_ref, wg_ref, wu_ref, wd_ref, ys_ref, wg_sc, wu_sc, wd_sc):
    i = pl.program_id(0)
    live = i < nt_ref[0]
    new_expert = jnp.logical_or(i == 0, te_ref[i] != te_ref[jnp.maximum(i - 1, 0)])

    @pl.when(jnp.logical_and(live, new_expert))
    def _():
        wg_sc[...] = wg_ref[0].astype(BF16)
        wu_sc[...] = wu_ref[0].astype(BF16)
        wd_sc[...] = wd_ref[0].astype(BF16)

    @pl.when(live)
    def _():
        x = xs_ref[...]
        hg = jnp.dot(x, wg_sc[...], preferred_element_type=F32)
        hu = jnp.dot(x, wu_sc[...], preferred_element_type=F32)
        act = (hg * _sigmoid(hg) * hu).astype(BF16)
        ys_ref[...] = jnp.dot(act, wd_sc[...], preferred_element_type=F32).astype(ys_ref.dtype)

    @pl.when(jnp.logical_not(live))
    def _():
        ys_ref[...] = jnp.zeros(ys_ref.shape, ys_ref.dtype)


def _experts(xs, tile_expert, n_live, w_gate, w_up, w_down):
    p, d = xs.shape
    tm = EXPERT_TILE
    f = w_gate.shape[2]
    grid_spec = pltpu.PrefetchScalarGridSpec(
        num_scalar_prefetch=2,
        grid=(p // tm,),
        in_specs=[
            pl.BlockSpec((tm, d), lambda i, te, nt: (i, 0)),
            pl.BlockSpec((1, d, f), lambda i, te, nt: (te[i], 0, 0)),
            pl.BlockSpec((1, d, f), lambda i, te, nt: (te[i], 0, 0)),
            pl.BlockSpec((1, f, d), lambda i, te, nt: (te[i], 0, 0)),
        ],
        out_specs=pl.BlockSpec((tm, d), lambda i, te, nt: (i, 0)),
        scratch_shapes=[pltpu.VMEM((d, f), BF16), pltpu.VMEM((d, f), BF16), pltpu.VMEM((f, d), BF16)],
    )
    return pl.pallas_call(
        _expert_kernel,
        grid_spec=grid_spec,
        out_shape=jax.ShapeDtypeStruct((p, d), BF16),
        compiler_params=_cparams(("arbitrary",)),
        name="experts",
    )(tile_expert, n_live, xs, w_gate, w_up, w_down)


def _combine_kernel(seq_ref, x_ref, y1_ref, y2_ref, route_ref, mod_ref, o_ref):
    del seq_ref
    o_ref[...] = _moe_residual(x_ref, y1_ref, y2_ref, route_ref, mod_ref)


def _combine(x1, y1, y2, route, mod, blk_seq):
    t, d = x1.shape
    tb = TOKEN_BLOCK
    row = lambda i, s: (i, 0)
    grid_spec = pltpu.PrefetchScalarGridSpec(
        num_scalar_prefetch=1,
        grid=(t // tb,),
        in_specs=[
            pl.BlockSpec((tb, d), row),
            pl.BlockSpec((tb, d), row),
            pl.BlockSpec((tb, d), row),
            pl.BlockSpec((tb, LANES), row),
            pl.BlockSpec((1,) + mod.shape[1:], lambda i, s: (s[i], 0, 0)),
        ],
        out_specs=pl.BlockSpec((tb, d), row),
    )
    return pl.pallas_call(
        _combine_kernel,
        grid_spec=grid_spec,
        out_shape=jax.ShapeDtypeStruct((t, d), F32),
        compiler_params=_cparams(("arbitrary",)),
        name="moe_combine",
    )(blk_seq, x1, y1, y2, route, mod)


def _dispatch_plan(route, counts, n_experts):
    t = route.shape[0]
    tm = EXPERT_TILE
    eid = route[:, :FINE_TOP_K].astype(jnp.int32).reshape(-1)
    rank = route[:, 2 * FINE_TOP_K:3 * FINE_TOP_K].astype(jnp.int32).reshape(-1)
    counts = counts[0, :n_experts].astype(jnp.int32)
    padded = ((counts + tm - 1) // tm) * tm
    pad_end = jnp.cumsum(padded)
    pad_off = pad_end - padded
    dest = pad_off[eid] + rank
    n_rows = FINE_TOP_K * t + n_experts * tm
    n_tiles = n_rows // tm
    src_tok = (jnp.arange(n_rows, dtype=jnp.int32) % t).at[dest].set(
        jnp.arange(FINE_TOP_K * t, dtype=jnp.int32) // FINE_TOP_K,
        unique_indices=True, mode='promise_in_bounds')
    tile_start = jnp.arange(n_tiles, dtype=jnp.int32) * tm
    tile_expert = jnp.minimum(jnp.sum((pad_end[None, :] <= tile_start[:, None]).astype(jnp.int32), axis=1),
                              n_experts - 1)
    n_live = (pad_end[-1] // tm).astype(jnp.int32).reshape(1)
    last_live = jnp.maximum(n_live[0] - 1, 0)
    tile_expert = jnp.where(tile_start // tm < n_live[0], tile_expert, tile_expert[last_live])
    return dest.reshape(t, FINE_TOP_K), src_tok, tile_expert, n_live


def _rope_tables(max_len):
    rows = max_len // GRID_W
    row = jnp.repeat(jnp.arange(rows, dtype=F32), GRID_W)
    col = jnp.tile(jnp.arange(GRID_W, dtype=F32), rows)
    half = HEAD_DIM // 2
    inv_freq = 1.0 / (ROPE_THETA ** (jnp.arange(0, half, 2, dtype=F32) / half))
    ang_r = row[:, None] * inv_freq[None, :]
    ang_c = col[:, None] * inv_freq[None, :]
    ang = jnp.concatenate([ang_r, ang_r, ang_c, ang_c], axis=-1)
    sign = jnp.where((jnp.arange(HEAD_DIM) % (HEAD_DIM // 2)) < HEAD_DIM // 4, -1.0, 1.0)
    reps = LANES // HEAD_DIM
    return jnp.tile(jnp.cos(ang), (1, reps)), jnp.tile(jnp.sin(ang) * sign[None, :], (1, reps))


def _to_block_order(a):
    n, d = a.shape
    return a.reshape(n // TOKEN_BLOCK, CHUNKS_PER_BLOCK, SSM_CHUNK, d).transpose(0, 2, 1, 3).reshape(n, d)


def _from_block_order(a):
    n, d = a.shape
    return a.reshape(n // TOKEN_BLOCK, SSM_CHUNK, CHUNKS_PER_BLOCK, d).transpose(0, 2, 1, 3).reshape(n, d)


def kernel(x_prompt, x_sample, c_prompt, c_sample, w_ada, b_ada, g_norm1, g_norm2, w_in, ssm_a_re, ssm_a_im, ssm_log_dt, ssm_b_re, ssm_b_im, ssm_c_re, ssm_c_im, ssm_d, w_glu, b_glu, g_q, g_k, w_branch_ssm, w_branch_attn, w_out, w_coarse, b_coarse, w_fine, b_fine, w_expert_gate, w_expert_up, w_expert_down):
    b1, l1, d = x_prompt.shape
    b2, l2, _ = x_sample.shape
    depth = w_in.shape[0]
    d_ssm = w_glu.shape[-1]
    d_attn = w_branch_attn.shape[1]
    d_kv = N_KV_HEADS * HEAD_DIM
    n_heads = d_attn // HEAD_DIM
    n_groups = w_coarse.shape[-1]
    n_experts = w_fine.shape[-1]
    per_group = n_experts // n_groups
    n_ssm_groups = ssm_b_re.shape[2]
    ssm_h = ssm_b_re.shape[-1]
    tb = TOKEN_BLOCK
    assert l1 % tb == 0 and l2 % tb == 0 and l1 % KV_TILE == 0 and l2 % KV_TILE == 0
    assert d_attn == N_KV_HEADS * (n_heads // N_KV_HEADS) * HEAD_DIM and 2 * HEAD_DIM == LANES
    assert n_ssm_groups * ssm_h == d_ssm and d_ssm % LANES == 0 and LANES % ssm_h == 0
    assert 2 * ssm_a_re.shape[-1] == LANES
    assert n_experts + n_groups <= LANES

    seq_lens = [l1] * b1 + [l2] * b2
    t = sum(seq_lens)
    x = jnp.concatenate([x_prompt.reshape(b1 * l1, d), x_sample.reshape(b2 * l2, d)], axis=0)
    x = _to_block_order(x)
    c = jnp.concatenate([c_prompt, c_sample], axis=0)
    blk_seq = np.concatenate([np.full(n // tb, s, np.int32) for s, n in enumerate(seq_lens)])
    blk_pos = np.concatenate([np.arange(n // tb, dtype=np.int32) for n in seq_lens])
    blk_seq, blk_pos = jnp.asarray(blk_seq), jnp.asarray(blk_pos)
    cos_t, sin_t = (_to_block_order(tab) for tab in _rope_tables(max(l1, l2)))

    seg_rows = max(seq_lens) // SSM_CHUNK
    seg_seq_rows = []
    filled = 0
    for n in seq_lens:
        rows = n // SSM_CHUNK
        assert rows % SUBLANES == 0 and seg_rows % rows == 0
        if filled == 0:
            seg_seq_rows.append(rows)
        assert seg_seq_rows[-1] == rows
        filled = (filled + rows) % seg_rows
    assert filled == 0
    seg_seq_rows = jnp.asarray(np.array(seg_seq_rows, np.int32))

    rep = n_heads // N_KV_HEADS
    slot_head = np.array([g * rep + j for j in range(rep) for g in range(N_KV_HEADS)])
    col_perm = (slot_head[:, None] * HEAD_DIM + np.arange(HEAD_DIM)[None, :]).reshape(-1)
    in_perm = np.concatenate([np.arange(d_ssm), d_ssm + col_perm, np.arange(d_ssm + d_attn, w_in.shape[-1])])

    mod_all = _ada_modulation(c, w_ada, b_ada)
    gq_t = jnp.tile(g_q.astype(F32), (1, LANES // HEAD_DIM))
    gk_t = jnp.tile(g_k.astype(F32), (1, LANES // HEAD_DIM))
    head_mean = jnp.asarray(np.kron(np.eye(LANES // HEAD_DIM), np.full((HEAD_DIM, HEAD_DIM), 1.0 / HEAD_DIM)), BF16)
    w_router = jnp.concatenate([w_fine, w_coarse,
                                jnp.zeros((depth, d, LANES - n_experts - n_groups), F32)], -1).astype(BF16)
    b_router = jnp.concatenate([b_fine, b_coarse,
                                jnp.zeros((depth, LANES - n_experts - n_groups), F32)], -1)

    pending = None
    for i in range(depth):
        mod = mod_all[i]
        x, u_tm, q, kt, v, gates = _mix_in(
            x, pending, mod, g_norm1[i][None], w_in[i][:, in_perm].astype(BF16), cos_t, sin_t,
            gq_t[i][None], gk_t[i][None], head_mean, blk_seq, blk_pos, (d_ssm, d_attn, d_kv))

        w_in_rows, lag_blocks, w_out_cols, ssm_a = _ssm_weights(
            ssm_a_re[i], ssm_a_im[i], ssm_log_dt[i], ssm_b_re[i], ssm_b_im[i],
            ssm_c_re[i], ssm_c_im[i], ssm_d[i])
        states = _ssm_states(u_tm, w_in_rows, ssm_a, seg_rows, seg_seq_rows, ssm_h)
        y_tm = _ssm_output(u_tm, states, lag_blocks, w_out_cols, ssm_h)

        o_first = _attention(q, kt, v, 0, b1, l1)
        o_second = _attention(q, kt, v, b1 * l1, b2, l2)

        w_ba = w_branch_attn[i][col_perm].astype(BF16)
        x, h2, route, counts = _mix_out(
            x, y_tm, o_first, o_second, gates, mod, w_glu[i].astype(BF16), b_glu[i][None],
            w_branch_ssm[i].astype(BF16), w_ba, w_out[i].astype(BF16), g_norm2[i][None],
            w_router[i], b_router[i][None], blk_seq, n_groups, per_group)

        dest, src_tok, tile_expert, n_live = _dispatch_plan(route, counts, n_experts)
        xs = h2.at[src_tok].get(mode='promise_in_bounds')
        ys = _experts(xs, tile_expert, n_live, w_expert_gate[i], w_expert_up[i], w_expert_down[i])
        y1 = ys.at[dest[:, 0]].get(mode='promise_in_bounds')
        y2 = ys.at[dest[:, 1]].get(mode='promise_in_bounds')
        pending = (y1, y2, route, mod)

    x = _combine(x, *pending, blk_seq)
    x = _from_block_order(x)
    y_prompt = x[:b1 * l1].reshape(b1, l1, d)
    y_sample = x[b1 * l1:].reshape(b2, l2, d)
    return (y_prompt, y_sample)
```

```python
import functools
import math

import jax
import jax.numpy as jnp
import numpy as np
from jax import lax
from jax.experimental import pallas as pl
from jax.experimental.pallas import tpu as pltpu

F32 = jnp.float32
BF16 = jnp.bfloat16

EPS = 1e-6
GRID_W = 64
ROPE_THETA = 10000.0
HEAD_DIM = 64
N_KV_HEADS = 2
FINE_TOP_K = 2
LOG2E = 1.4426950408889634
LANES = 128
SUBLANES = 8
SSM_CHUNK = 16

TOKEN_BLOCK = 512
CHUNKS_PER_BLOCK = TOKEN_BLOCK // SSM_CHUNK
Q_TILE = 256
KV_TILE = TOKEN_BLOCK
KV_STEP_TILES = 4
EXPERT_TILE = 512
SSM_ROW_BLOCK = 512
VMEM_LIMIT = 56 * 1024 * 1024


def _sigmoid(x):
    return 1.0 / (1.0 + jnp.exp(-x))


def _cparams(sem):
    return pltpu.CompilerParams(dimension_semantics=sem, vmem_limit_bytes=VMEM_LIMIT)


def _ada_kernel(c_ref, w_ref, b_ref, o_ref):
    c = c_ref[...]
    ca = c * _sigmoid(c)
    o_ref[0, 0] = jnp.dot(ca, w_ref[0], preferred_element_type=F32,
                          precision=lax.Precision.HIGHEST) + b_ref[0, 0]


def _ada_modulation(c, w_ada, b_ada):
    depth, d, _ = w_ada.shape
    nb = c.shape[0]
    n_mod = w_ada.shape[2] // d
    out = pl.pallas_call(
        _ada_kernel,
        grid=(depth, n_mod),
        in_specs=[
            pl.BlockSpec((nb, d), lambda l, j: (0, 0)),
            pl.BlockSpec((1, d, d), lambda l, j: (l, 0, j)),
            pl.BlockSpec((1, 1, 1, d), lambda l, j: (l, j, 0, 0)),
        ],
        out_specs=pl.BlockSpec((1, 1, nb, d), lambda l, j: (l, j, 0, 0)),
        out_shape=jax.ShapeDtypeStruct((depth, n_mod, nb, d), F32),
        compiler_params=_cparams(("arbitrary", "arbitrary")),
        name="ada_modulation",
    )(c, w_ada, b_ada.reshape(depth, n_mod, 1, d))
    return out.transpose(0, 2, 1, 3)


def _rope(x, cos, sin_signed, low_half):
    up = pltpu.roll(x, LANES - 16, axis=1)
    down = pltpu.roll(x, 16, axis=1)
    return x * cos + jnp.where(low_half, up, down) * sin_signed


def _moe_residual(x_ref, y1_ref, y2_ref, route_ref, mod_ref):
    route = route_ref[...]
    moe = route[:, 2:3] * y1_ref[...].astype(F32) + route[:, 3:4] * y2_ref[...].astype(F32)
    return x_ref[...] + mod_ref[0][5:6] * moe


def _mix_in_kernel(seq_ref, pos_ref, *refs, d_ssm, d_attn, d_kv, pending_moe):
    del seq_ref, pos_ref
    if pending_moe:
        x_ref, y1_ref, y2_ref, route_ref, prev_mod_ref = refs[:5]
        (mod_ref, g1_ref, w_ref, cos_ref, sin_ref, gq_ref, gk_ref, hm_ref,
         u_ref, q_ref, kt_ref, v_ref, gate_ref, x_out_ref) = refs[5:]
        x = _moe_residual(x_ref, y1_ref, y2_ref, route_ref, prev_mod_ref)
        x_out_ref[...] = x
    else:
        (x_ref, mod_ref, g1_ref, w_ref, cos_ref, sin_ref, gq_ref, gk_ref, hm_ref,
         u_ref, q_ref, kt_ref, v_ref, gate_ref) = refs
        x = x_ref[...]
    mod = mod_ref[0]
    xn = x * lax.rsqrt(jnp.mean(x * x, axis=-1, keepdims=True) + EPS) * g1_ref[...]
    h = (xn * (1.0 + mod[1:2]) + mod[0:1]).astype(BF16)

    def proj(lo, hi):
        return jnp.dot(h, w_ref[:, lo:hi], preferred_element_type=F32)

    u = proj(0, d_ssm).astype(BF16)
    n_c = u.shape[0] // SSM_CHUNK
    for t in range(SSM_CHUNK):
        u_ref[t] = u[t * n_c:(t + 1) * n_c]

    cos = cos_ref[...]
    sin_signed = sin_ref[...]
    lane = lax.broadcasted_iota(jnp.int32, cos.shape, 1)
    low_half = (lane % 32) < 16
    hm = hm_ref[...]

    def norm_rope(z, gain, scale):
        cols = []
        for j in range(z.shape[1] // LANES):
            zj = z[:, j * LANES:(j + 1) * LANES]
            msq = jnp.dot((zj * zj).astype(BF16), hm, preferred_element_type=F32)
            zn = zj * lax.rsqrt(msq + EPS) * gain
            cols.append(_rope(zn, cos, sin_signed, low_half) * scale)
        return jnp.concatenate(cols, axis=1) if len(cols) > 1 else cols[0]

    o = d_ssm
    q_ref[...] = norm_rope(proj(o, o + d_attn), gq_ref[...],
                           (HEAD_DIM ** -0.5) * LOG2E).astype(BF16)
    o += d_attn
    kt_ref[0] = norm_rope(proj(o, o + d_kv), gk_ref[...], 1.0).T.astype(BF16)
    o += d_kv
    v_ref[...] = proj(o, o + d_kv).astype(BF16)
    o += d_kv
    n_gate = gate_ref.shape[1]
    half = n_gate // 2
    gate_ref[:, :half] = _sigmoid(proj(o, o + half)).astype(BF16)
    gate_ref[:, half:] = _sigmoid(proj(o + half, o + n_gate)).astype(BF16)


def _mix_in(x, pending, mod, g1, w_in, cos_t, sin_t, gq_t, gk_t, hm, blk_seq, blk_pos, dims):
    t, d = x.shape
    d_ssm, d_attn, d_kv = dims
    n_gate = w_in.shape[1] - d_ssm - d_attn - 2 * d_kv
    tb = TOKEN_BLOCK
    row = lambda i, s, p: (i, 0)
    const = lambda i, s, p: (0, 0)
    by_seq = lambda i, s, p: (s[i], 0, 0)
    pending_specs, pending_args, x_out_spec, x_out_shape = [], [], [], []
    if pending is not None:
        y1, y2, route, prev_mod = pending
        pending_specs = [pl.BlockSpec((tb, d), row), pl.BlockSpec((tb, d), row),
                         pl.BlockSpec((tb, LANES), row), pl.BlockSpec((1,) + prev_mod.shape[1:], by_seq)]
        pending_args = [y1, y2, route, prev_mod]
        x_out_spec = [pl.BlockSpec((tb, d), row)]
        x_out_shape = [jax.ShapeDtypeStruct((t, d), F32)]
    grid_spec = pltpu.PrefetchScalarGridSpec(
        num_scalar_prefetch=2,
        grid=(t // tb,),
        in_specs=[pl.BlockSpec((tb, d), row)] + pending_specs + [
            pl.BlockSpec((1,) + mod.shape[1:], by_seq),
            pl.BlockSpec((1, d), const),
            pl.BlockSpec(w_in.shape, const),
            pl.BlockSpec((tb, LANES), lambda i, s, p: (p[i], 0)),
            pl.BlockSpec((tb, LANES), lambda i, s, p: (p[i], 0)),
            pl.BlockSpec((1, LANES), const),
            pl.BlockSpec((1, LANES), const),
            pl.BlockSpec((LANES, LANES), const),
        ],
        out_specs=[
            pl.BlockSpec((SSM_CHUNK, CHUNKS_PER_BLOCK, d_ssm), lambda i, s, p: (0, i, 0)),
            pl.BlockSpec((tb, d_attn), row),
            pl.BlockSpec((1, d_kv, tb), lambda i, s, p: (i, 0, 0)),
            pl.BlockSpec((tb, d_kv), row),
            pl.BlockSpec((tb, n_gate), row),
        ] + x_out_spec,
    )
    outs = pl.pallas_call(
        functools.partial(_mix_in_kernel, d_ssm=d_ssm, d_attn=d_attn, d_kv=d_kv,
                          pending_moe=pending is not None),
        grid_spec=grid_spec,
        out_shape=[
            jax.ShapeDtypeStruct((SSM_CHUNK, t // SSM_CHUNK, d_ssm), BF16),
            jax.ShapeDtypeStruct((t, d_attn), BF16),
            jax.ShapeDtypeStruct((t // tb, d_kv, tb), BF16),
            jax.ShapeDtypeStruct((t, d_kv), BF16),
            jax.ShapeDtypeStruct((t, n_gate), BF16),
        ] + x_out_shape,
        compiler_params=_cparams(("arbitrary",)),
        name="mix_in",
    )(blk_seq, blk_pos, x, *pending_args, mod, g1, w_in, cos_t, sin_t, gq_t, gk_t, hm)
    x_now = outs[5] if pending is not None else x
    return (x_now,) + tuple(outs[:5])


def _ssm_weights(a_re, a_im, log_dt, b_re, b_im, c_re, c_im, d_skip):
    n_dirs, g, p = a_re.shape
    hch = b_re.shape[-1]
    c = SSM_CHUNK
    nq = g * hch // LANES
    gq = g // nq
    lam = lax.complex(a_re.astype(F32), a_im.astype(F32))
    dt = jnp.exp(log_dt.astype(F32))[..., None]
    lam_dt = lam * dt
    coef = (jnp.exp(lam_dt) - 1.0) / lam
    steps = jnp.arange(c + 1, dtype=F32)
    pw = jnp.exp(lam_dt[:, None] * steps[None, :, None, None])
    bc = lax.complex(b_re.astype(F32), b_im.astype(F32)) * coef[..., None]
    cc = lax.complex(c_re.astype(F32), c_im.astype(F32))
    eye = jnp.eye(gq, dtype=F32)

    kern = jnp.einsum('rghp,rdgp,rgpk->rdghk', cc, pw[:, :c], bc).real
    dmat = jnp.eye(hch, dtype=F32)[None] * d_skip.astype(F32).reshape(g, hch)[:, :, None]
    lag0 = (kern[0, 0] + kern[1, 0] + dmat)[None]
    by_lag = jnp.concatenate([kern[1, 1:][::-1], lag0, kern[0, 1:]], 0)
    by_lag = by_lag.reshape(2 * c - 1, nq, gq, hch, hch)
    lag_blocks = jnp.einsum('dqkoh,gk->qdghko', by_lag, eye).reshape(nq, 2 * c - 1, LANES, LANES)

    win_f = pw[0, c - 1 - jnp.arange(c)][:, :, None, :] * bc[0].transpose(0, 2, 1)[None]
    win_b = pw[1, jnp.arange(c)][:, :, None, :] * bc[1].transpose(0, 2, 1)[None]

    def in_rows(w):
        parts = jnp.stack([w.real, w.imag], 0).reshape(2, c, nq, gq, hch, p)
        parts = parts.transpose(2, 0, 1, 3, 4, 5).reshape(nq, 2, c * gq * hch, p)
        return jnp.concatenate([parts, parts], -1)

    w_in_rows = jnp.stack([in_rows(win_f), in_rows(win_b)], 1)

    cout_f = cc[0][None] * pw[0, 1 + jnp.arange(c)][:, :, None, :]
    cout_b = cc[1][None] * pw[1, c - jnp.arange(c)][:, :, None, :]

    def out_cols(w):
        parts = jnp.stack([w.real, -w.imag], 0).reshape(2, c, nq, gq, hch, p)
        return parts.transpose(2, 0, 5, 1, 3, 4).reshape(nq, 2, p, c * gq * hch)

    w_out_cols = jnp.concatenate([out_cols(cout_f), out_cols(cout_b)], 1)

    sub = SUBLANES
    rows = jnp.arange(sub)
    scan_pos = jnp.stack([rows, sub - 1 - rows], 0)
    a_log = lam_dt * c
    tables = []
    d = 1
    while d < sub:
        tables.append(jnp.where((scan_pos >= d)[:, :, None, None], jnp.exp(a_log * d)[:, None], 0.0))
        d *= 2
    tables.append(jnp.exp(a_log[:, None] * (scan_pos + 1).astype(F32)[:, :, None, None]))
    a_tab = jnp.stack(tables, 1)
    a_tab = jnp.stack([a_tab.real, a_tab.imag], 2).reshape(n_dirs, len(tables), 2, sub, nq, gq * p)
    a_tab = a_tab.transpose(4, 0, 1, 2, 3, 5)
    return w_in_rows.astype(BF16), lag_blocks.astype(BF16), w_out_cols.astype(BF16), a_tab


def _group_of(index, per_group):
    return (index // per_group) % (LANES // per_group)


def _ssm_states_kernel(len_ref, u_ref, win_ref, a_ref, s_ref, w_sc, x_sc, s_sc, *, ssm_h):
    n_rows, width = x_sc.shape
    half = width // 2

    @pl.when(pl.program_id(2) == 0)
    def _():
        shape = win_ref.shape[3:]
        row_group = _group_of(lax.broadcasted_iota(jnp.int32, shape, 0), ssm_h)
        lane_half = lax.broadcasted_iota(jnp.int32, shape, 1) // (LANES // 2)
        for a in range(2):
            rows = win_ref[0, 0, a]
            for j in range(half // LANES):
                col = a * half + j * LANES
                w_sc[:, col:col + LANES] = jnp.where(row_group == 2 * j + lane_half, rows,
                                                     jnp.zeros_like(rows))

    ucat = jnp.concatenate([u_ref[t] for t in range(SSM_CHUNK)], axis=1)
    x_sc[...] = jnp.dot(ucat, w_sc[...], preferred_element_type=F32)
    sub = SUBLANES
    tile = (sub, half)
    row_id = lax.broadcasted_iota(jnp.int32, tile, 0)
    seq_rows = len_ref[pl.program_id(2)]
    n_tiles = n_rows // sub
    n_levels = a_ref.shape[2] - 1

    def cmul(c_re, c_im, z_re, z_im):
        return c_re * z_re - c_im * z_im, c_re * z_im + c_im * z_re

    def run(backward):
        def toward_later(z, dist):
            return pltpu.roll(z, sub - dist if backward else dist, axis=0)

        def body(i, carry):
            s_re, s_im = carry
            tile_i = n_tiles - 1 - i if backward else i
            row0 = pl.multiple_of(tile_i * sub, sub)
            first = (row0 + sub) if backward else row0
            keep = jnp.where(lax.rem(first, seq_rows) == 0, 0.0, 1.0).astype(F32)
            s_re = s_re * keep
            s_im = s_im * keep
            z_re = x_sc[pl.ds(row0, sub), 0:half]
            z_im = x_sc[pl.ds(row0, sub), half:width]
            for lvl in range(n_levels):
                m_re, m_im = cmul(a_ref[0, 0, lvl, 0], a_ref[0, 0, lvl, 1],
                                  toward_later(z_re, 2 ** lvl), toward_later(z_im, 2 ** lvl))
                z_re, z_im = z_re + m_re, z_im + m_im
            c_re, c_im = cmul(a_ref[0, 0, n_levels, 0], a_ref[0, 0, n_levels, 1], s_re, s_im)
            z_re, z_im = z_re + c_re, z_im + c_im
            entry = sub - 1 if backward else 0
            s_sc[pl.ds(row0, sub), 0:half] = jnp.where(row_id == entry, s_re, toward_later(z_re, 1))
            s_sc[pl.ds(row0, sub), half:width] = jnp.where(row_id == entry, s_im, toward_later(z_im, 1))
            last = sub - 1 - entry
            return (jnp.broadcast_to(z_re[last:last + 1], tile),
                    jnp.broadcast_to(z_im[last:last + 1], tile))
        zero = jnp.zeros(tile, F32)
        lax.fori_loop(0, n_tiles, body, (zero, zero))

    direction = pl.program_id(1)

    @pl.when(direction == 0)
    def _():
        run(False)

    @pl.when(direction == 1)
    def _():
        run(True)

    s_ref[...] = s_sc[...].astype(s_ref.dtype)


def _ssm_states(u_tm, w_in_rows, a_tab, seg_rows, seg_seq_rows, ssm_h):
    c, nc, d_ssm = u_tm.shape
    nq, n_dirs, _, k, two_p = w_in_rows.shape
    assert a_tab.shape[-2] == SUBLANES
    n = a_tab.shape[-1] * 2
    assert two_p == LANES and k == c * LANES
    grid_spec = pltpu.PrefetchScalarGridSpec(
        num_scalar_prefetch=1,
        grid=(nq, n_dirs, nc // seg_rows),
        in_specs=[
            pl.BlockSpec((c, seg_rows, LANES), lambda q, r, s, ln: (0, s, q)),
            pl.BlockSpec((1, 1, 2, k, two_p), lambda q, r, s, ln: (q, r, 0, 0, 0)),
            pl.BlockSpec((1, 1) + a_tab.shape[2:], lambda q, r, s, ln: (q, r, 0, 0, 0, 0)),
        ],
        out_specs=pl.BlockSpec((seg_rows, n), lambda q, r, s, ln: (s, q * n_dirs + r)),
        scratch_shapes=[pltpu.VMEM((k, n), BF16), pltpu.VMEM((seg_rows, n), F32),
                        pltpu.VMEM((seg_rows, n), F32)],
    )
    return pl.pallas_call(
        functools.partial(_ssm_states_kernel, ssm_h=ssm_h),
        grid_spec=grid_spec,
        out_shape=jax.ShapeDtypeStruct((nc, nq * n_dirs * n), BF16),
        compiler_params=_cparams(("arbitrary", "arbitrary", "arbitrary")),
        name="ssm_states",
    )(seg_seq_rows, u_tm, w_in_rows, a_tab)


def _ssm_output_kernel(u_ref, s_ref, lag_ref, wcol_ref, y_ref, wintra_sc, wout_sc, *, ssm_h):
    n_t_out = y_ref.shape[0]

    @pl.when(pl.program_id(2) == 0)
    def _():
        t_out0 = pl.program_id(1) * n_t_out
        for t_in in range(SSM_CHUNK):
            for j in range(n_t_out):
                wintra_sc[t_in * LANES:(t_in + 1) * LANES, j * LANES:(j + 1) * LANES] = (
                    lag_ref[0, t_out0 + j - t_in + SSM_CHUNK - 1])
        n_planes, p, n_cols = wcol_ref.shape[1:]
        col_group = _group_of(lax.broadcasted_iota(jnp.int32, (p, n_cols), 1), ssm_h)
        for r in range(n_planes):
            cols = wcol_ref[0, r]
            for k in range(LANES // ssm_h):
                row = (r * (LANES // ssm_h) + k) * p
                wout_sc[row:row + p, :] = jnp.where(col_group == k, cols, jnp.zeros_like(cols))

    ucat = jnp.concatenate([u_ref[t] for t in range(SSM_CHUNK)], axis=1)
    y = jnp.dot(ucat, wintra_sc[...], preferred_element_type=F32)
    y += jnp.dot(s_ref[...], wout_sc[...], preferred_element_type=F32)
    for t in range(n_t_out):
        y_ref[t] = y[:, t * LANES:(t + 1) * LANES].astype(y_ref.dtype)


def _ssm_output(u_tm, s, lag_blocks, w_out_cols, ssm_h):
    c, nc, d_ssm = u_tm.shape
    nq, n_planes, p, k = w_out_cols.shape
    n_half = 2
    n = k // n_half
    s_width = n_planes * (LANES // ssm_h) * p
    rb = min(SSM_ROW_BLOCK, nc)
    return pl.pallas_call(
        functools.partial(_ssm_output_kernel, ssm_h=ssm_h),
        grid=(nq, n_half, nc // rb),
        in_specs=[
            pl.BlockSpec((c, rb, LANES), lambda q, h, i: (0, i, q)),
            pl.BlockSpec((rb, s_width), lambda q, h, i: (i, q)),
            pl.BlockSpec((1,) + lag_blocks.shape[1:], lambda q, h, i: (q, 0, 0, 0)),
            pl.BlockSpec((1, n_planes, p, n), lambda q, h, i: (q, 0, 0, h)),
        ],
        out_specs=pl.BlockSpec((c // n_half, rb, LANES), lambda q, h, i: (h, i, q)),
        out_shape=jax.ShapeDtypeStruct((c, nc, d_ssm), BF16),
        scratch_shapes=[pltpu.VMEM((k, n), BF16), pltpu.VMEM((s_width, n), BF16)],
        compiler_params=_cparams(("arbitrary", "arbitrary", "arbitrary")),
        name="ssm_output",
    )(u_tm, s, lag_blocks, w_out_cols)


def _attn_kernel(q_ref, kt_ref, v_ref, o_ref, m_sc, l_sc, acc_sc, *, n_kv_tiles):
    tq = q_ref.shape[0]
    n_slices = q_ref.shape[1] // LANES
    q = jnp.concatenate([q_ref[:, j * LANES:(j + 1) * LANES] for j in range(n_slices)], axis=0)
    m_sc[...] = jnp.full(m_sc.shape, -jnp.inf, F32)
    l_sc[...] = jnp.zeros(l_sc.shape, F32)
    acc_sc[...] = jnp.zeros(acc_sc.shape, F32)
    step_tiles = min(KV_STEP_TILES, n_kv_tiles)
    kv_step_len = step_tiles * KV_TILE
    n_rep = kv_step_len // LANES
    lane_head = lax.broadcasted_iota(jnp.int32, acc_sc.shape, 1) // HEAD_DIM
    key_head = lax.broadcasted_iota(jnp.int32, (kt_ref.shape[1], kv_step_len), 0) // HEAD_DIM

    def kv_step(kb, carry):
        start = pl.multiple_of(kb * kv_step_len, kv_step_len)
        kt = jnp.concatenate([kt_ref[kb * step_tiles + i] for i in range(step_tiles)], axis=1)
        v = v_ref[pl.ds(start, kv_step_len), :]
        acc = acc_sc[...]
        for g in range(N_KV_HEADS):
            ktg = jnp.where(key_head == g, kt, jnp.zeros_like(kt))
            s = jnp.dot(q, ktg, preferred_element_type=F32)
            m_old = m_sc[g]
            m_new = jnp.maximum(m_old, jnp.max(s, axis=-1, keepdims=True))
            alpha = jnp.exp2(m_old - m_new)
            p = jnp.exp2(s - jnp.concatenate([m_new] * n_rep, axis=1))
            l_part = p[:, 0:LANES]
            for c in range(1, n_rep):
                l_part = l_part + p[:, c * LANES:(c + 1) * LANES]
            l_sc[g] = alpha * l_sc[g] + l_part
            m_sc[g] = m_new
            pv = jnp.dot(p.astype(BF16), v, preferred_element_type=F32)
            acc = jnp.where(lane_head == g, alpha * acc + pv, acc)
        acc_sc[...] = acc
        return carry

    lax.fori_loop(0, n_kv_tiles // step_tiles, kv_step, 0)
    inv = [1.0 / jnp.sum(l_sc[g], axis=-1, keepdims=True) for g in range(N_KV_HEADS)]
    out = acc_sc[...] * jnp.where(lane_head == 0, inv[0], inv[1])
    for j in range(n_slices):
        o_ref[:, j * LANES:(j + 1) * LANES] = out[j * tq:(j + 1) * tq].astype(o_ref.dtype)


def _attention(q, kt, v, first_token, n_seqs, seq_len):
    d_attn = q.shape[1]
    d_kv = v.shape[1]
    tq = min(Q_TILE, seq_len)
    q_tiles = seq_len // tq
    kv_tiles = seq_len // KV_TILE
    assert first_token % seq_len == 0
    seq0 = first_token // seq_len
    rows = (d_attn // LANES) * tq
    return pl.pallas_call(
        functools.partial(_attn_kernel, n_kv_tiles=kv_tiles),
        grid=(n_seqs, q_tiles),
        in_specs=[
            pl.BlockSpec((tq, d_attn), lambda i, j: ((seq0 + i) * q_tiles + j, 0)),
            pl.BlockSpec((kv_tiles, d_kv, KV_TILE), lambda i, j: (seq0 + i, 0, 0)),
            pl.BlockSpec((seq_len, d_kv), lambda i, j: (seq0 + i, 0)),
        ],
        out_specs=pl.BlockSpec((tq, d_attn), lambda i, j: (i * q_tiles + j, 0)),
        out_shape=jax.ShapeDtypeStruct((n_seqs * seq_len, d_attn), BF16),
        scratch_shapes=[pltpu.VMEM((N_KV_HEADS, rows, LANES), F32),
                        pltpu.VMEM((N_KV_HEADS, rows, LANES), F32),
                        pltpu.VMEM((rows, LANES), F32)],
        compiler_params=_cparams(("arbitrary", "arbitrary")),
        name="attention",
    )(q, kt, v)


def _mix_out_kernel(seq_ref, x_ref, y_ref, o1_ref, o2_ref, gate_ref, mod_ref, wglu_ref, bglu_ref,
                    wbs_ref, wba_ref, wout_ref, g2_ref, wr_ref, br_ref, ltri_ref,
                    x1_ref, h2_ref, route_ref, count_ref, cnt_sc,
                    *, n_groups, per_group, blocks_first):
    del seq_ref
    mod = mod_ref[0]
    y = jnp.concatenate([y_ref[t] for t in range(SSM_CHUNK)], axis=0).astype(F32)
    y = 0.5 * y * (1.0 + jnp.tanh(math.sqrt(2.0 / math.pi) * (y + 0.044715 * (y * y * y))))
    glu = jnp.dot(y.astype(BF16), wglu_ref[...], preferred_element_type=F32) + bglu_ref[...]
    ys = (y * _sigmoid(glu)).astype(BF16)
    d = x_ref.shape[1]
    gate = gate_ref[...].astype(F32)
    o = jnp.where(pl.program_id(0) < blocks_first, o1_ref[...], o2_ref[...])
    merged = gate[:, :d] * jnp.dot(ys, wbs_ref[...], preferred_element_type=F32)
    merged += gate[:, d:] * jnp.dot(o, wba_ref[...], preferred_element_type=F32)
    x1 = x_ref[...] + mod[2:3] * jnp.dot(merged.astype(BF16), wout_ref[...],
                                         preferred_element_type=F32)
    x1_ref[...] = x1
    xn = x1 * lax.rsqrt(jnp.mean(x1 * x1, axis=-1, keepdims=True) + EPS) * g2_ref[...]
    h2 = (xn * (1.0 + mod[4:5]) + mod[3:4]).astype(BF16)
    h2_ref[...] = h2

    n_exp = n_groups * per_group
    logits = jnp.dot(h2, wr_ref[...], preferred_element_type=F32) + br_ref[...]
    lane = lax.broadcasted_iota(jnp.int32, logits.shape, 1).astype(F32)
    neg = jnp.float32(-jnp.inf)
    big = jnp.float32(1 << 20)
    lc = jnp.where(lane >= n_exp, jnp.where(lane < n_exp + n_groups, logits, neg), neg)
    c_max = jnp.max(lc, axis=-1, keepdims=True)
    g_idx = jnp.min(jnp.where(lc == c_max, lane - n_exp, big), axis=-1, keepdims=True)
    pc_top = 1.0 / jnp.sum(jnp.exp(lc - c_max), axis=-1, keepdims=True)
    lf = jnp.where(lane >= g_idx * per_group,
                   jnp.where(lane < (g_idx + 1.0) * per_group, logits, neg), neg)
    f1 = jnp.max(lf, axis=-1, keepdims=True)
    e1 = jnp.min(jnp.where(lf == f1, lane, big), axis=-1, keepdims=True)
    lf2 = jnp.where(lane == e1, neg, lf)
    f2 = jnp.max(lf2, axis=-1, keepdims=True)
    e2 = jnp.min(jnp.where(lf2 == f2, lane, big), axis=-1, keepdims=True)
    r = jnp.exp(f2 - f1)
    w1 = pc_top / (1.0 + r)
    w2 = pc_top * r / (1.0 + r)

    @pl.when(pl.program_id(0) == 0)
    def _():
        cnt_sc[...] = jnp.zeros(cnt_sc.shape, F32)

    hot = jnp.where(lane == e1, 1.0, 0.0) + jnp.where(lane == e2, 1.0, 0.0)
    before = jnp.dot(ltri_ref[...], hot.astype(BF16), preferred_element_type=F32) + cnt_sc[0:1, :]
    r1 = jnp.sum(jnp.where(lane == e1, before, 0.0), axis=-1, keepdims=True)
    r2 = jnp.sum(jnp.where(lane == e2, before, 0.0), axis=-1, keepdims=True)
    cnt_sc[...] = cnt_sc[...] + jnp.sum(hot, axis=0, keepdims=True)
    count_ref[...] = cnt_sc[...]

    route = jnp.zeros(logits.shape, F32)
    for col, val in enumerate((e1, e2, w1, w2, r1, r2)):
        route = jnp.where(lane == col, val, route)
    route_ref[...] = route


def _mix_out(x, y_tm, o_first, o_second, gates, mod, w_glu, b_glu, w_bs, w_ba, w_out, g2,
             w_router, b_router, blk_seq, n_groups, per_group):
    t, d = x.shape
    tb = TOKEN_BLOCK
    blocks_first = o_first.shape[0] // tb
    row = lambda i, s: (i, 0)
    const = lambda i, s: (0, 0)
    grid_spec = pltpu.PrefetchScalarGridSpec(
        num_scalar_prefetch=1,
        grid=(t // tb,),
        in_specs=[
            pl.BlockSpec((tb, d), row),
            pl.BlockSpec((SSM_CHUNK, CHUNKS_PER_BLOCK, y_tm.shape[2]), lambda i, s: (0, i, 0)),
            pl.BlockSpec((tb, o_first.shape[1]), lambda i, s: (jnp.minimum(i, blocks_first - 1), 0)),
            pl.BlockSpec((tb, o_second.shape[1]), lambda i, s: (jnp.maximum(i - blocks_first, 0), 0)),
            pl.BlockSpec((tb, gates.shape[1]), row),
            pl.BlockSpec((1,) + mod.shape[1:], lambda i, s: (s[i], 0, 0)),
            pl.BlockSpec(w_glu.shape, const),
            pl.BlockSpec(b_glu.shape, const),
            pl.BlockSpec(w_bs.shape, const),
            pl.BlockSpec(w_ba.shape, const),
            pl.BlockSpec(w_out.shape, const),
            pl.BlockSpec(g2.shape, const),
            pl.BlockSpec(w_router.shape, const),
            pl.BlockSpec(b_router.shape, const),
            pl.BlockSpec((tb, tb), const),
        ],
        out_specs=[
            pl.BlockSpec((tb, d), row),
            pl.BlockSpec((tb, d), row),
            pl.BlockSpec((tb, LANES), row),
            pl.BlockSpec((SUBLANES, LANES), const),
        ],
        scratch_shapes=[pltpu.VMEM((SUBLANES, LANES), F32)],
    )
    strictly_lower = jnp.asarray(np.tril(np.ones((tb, tb), np.float32), -1), BF16)
    return pl.pallas_call(
        functools.partial(_mix_out_kernel, n_groups=n_groups, per_group=per_group,
                          blocks_first=blocks_first),
        grid_spec=grid_spec,
        out_shape=[
            jax.ShapeDtypeStruct((t, d), F32),
            jax.ShapeDtypeStruct((t, d), BF16),
            jax.ShapeDtypeStruct((t, LANES), F32),
            jax.ShapeDtypeStruct((SUBLANES, LANES), F32),
        ],
        compiler_params=_cparams(("arbitrary",)),
        name="mix_out",
    )(blk_seq, x, y_tm, o_first, o_second, gates, mod, w_glu, b_glu, w_bs, w_ba, w_out, g2,
      w_router, b_router, strictly_lower)


def _expert_kernel(te_ref, nt_ref, xs_ref, wg_ref, wu_ref, wd_ref, ys_ref, wg_sc, wu_sc, wd_sc):
    i = pl.program_id(0)
    live = i < nt_ref[0]
    new_expert = jnp.logical_or(i == 0, te_ref[i] != te_ref[jnp.maximum(i - 1, 0)])

    @pl.when(jnp.logical_and(live, new_expert))
    def _():
        wg_sc[...] = wg_ref[0].astype(BF16)
        wu_sc[...] = wu_ref[0].astype(BF16)
        wd_sc[...] = wd_ref[0].astype(BF16)

    @pl.when(live)
    def _():
        x = xs_ref[...]
        hg = jnp.dot(x, wg_sc[...], preferred_element_type=F32)
        hu = jnp.dot(x, wu_sc[...], preferred_element_type=F32)
        act = (hg * _sigmoid(hg) * hu).astype(BF16)
        ys_ref[...] = jnp.dot(act, wd_sc[...], preferred_element_type=F32).astype(ys_ref.dtype)

    @pl.when(jnp.logical_not(live))
    def _():
        ys_ref[...] = jnp.zeros(ys_ref.shape, ys_ref.dtype)


def _experts(xs, tile_expert, n_live, w_gate, w_up, w_down):
    p, d = xs.shape
    tm = EXPERT_TILE
    f = w_gate.shape[2]
    grid_spec = pltpu.PrefetchScalarGridSpec(
        num_scalar_prefetch=2,
        grid=(p // tm,),
        in_specs=[
            pl.BlockSpec((tm, d), lambda i, te, nt: (i, 0)),
            pl.BlockSpec((1, d, f), lambda i, te, nt: (te[i], 0, 0)),
            pl.BlockSpec((1, d, f), lambda i, te, nt: (te[i], 0, 0)),
            pl.BlockSpec((1, f, d), lambda i, te, nt: (te[i], 0, 0)),
        ],
        out_specs=pl.BlockSpec((tm, d), lambda i, te, nt: (i, 0)),
        scratch_shapes=[pltpu.VMEM((d, f), BF16), pltpu.VMEM((d, f), BF16), pltpu.VMEM((f, d), BF16)],
    )
    return pl.pallas_call(
        _expert_kernel,
        grid_spec=grid_spec,
        out_shape=jax.ShapeDtypeStruct((p, d), BF16),
        compiler_params=_cparams(("arbitrary",)),
        name="experts",
    )(tile_expert, n_live, xs, w_gate, w_up, w_down)


def _combine_kernel(seq_ref, x_ref, y1_ref, y2_ref, route_ref, mod_ref, o_ref):
    del seq_ref
    o_ref[...] = _moe_residual(x_ref, y1_ref, y2_ref, route_ref, mod_ref)


def _combine(x1, y1, y2, route, mod, blk_seq):
    t, d = x1.shape
    tb = TOKEN_BLOCK
    row = lambda i, s: (i, 0)
    grid_spec = pltpu.PrefetchScalarGridSpec(
        num_scalar_prefetch=1,
        grid=(t // tb,),
        in_specs=[
            pl.BlockSpec((tb, d), row),
            pl.BlockSpec((tb, d), row),
            pl.BlockSpec((tb, d), row),
            pl.BlockSpec((tb, LANES), row),
            pl.BlockSpec((1,) + mod.shape[1:], lambda i, s: (s[i], 0, 0)),
        ],
        out_specs=pl.BlockSpec((tb, d), row),
    )
    return pl.pallas_call(
        _combine_kernel,
        grid_spec=grid_spec,
        out_shape=jax.ShapeDtypeStruct((t, d), F32),
        compiler_params=_cparams(("arbitrary",)),
        name="moe_combine",
    )(blk_seq, x1, y1, y2, route, mod)


def _dispatch_plan(route, counts, n_experts):
    t = route.shape[0]
    tm = EXPERT_TILE
    eid = route[:, :FINE_TOP_K].astype(jnp.int32).reshape(-1)
    rank = route[:, 2 * FINE_TOP_K:3 * FINE_TOP_K].astype(jnp.int32).reshape(-1)
    counts = counts[0, :n_experts].astype(jnp.int32)
    padded = ((counts + tm - 1) // tm) * tm
    pad_end = jnp.cumsum(padded)
    pad_off = pad_end - padded
    dest = pad_off[eid] + rank
    n_rows = FINE_TOP_K * t + n_experts * tm
    n_tiles = n_rows // tm
    src_tok = (jnp.arange(n_rows, dtype=jnp.int32) % t).at[dest].set(
        jnp.arange(FINE_TOP_K * t, dtype=jnp.int32) // FINE_TOP_K,
        unique_indices=True, mode='promise_in_bounds')
    tile_start = jnp.arange(n_tiles, dtype=jnp.int32) * tm
    tile_expert = jnp.minimum(jnp.sum((pad_end[None, :] <= tile_start[:, None]).astype(jnp.int32), axis=1),
                              n_experts - 1)
    n_live = (pad_end[-1] // tm).astype(jnp.int32).reshape(1)
    last_live = jnp.maximum(n_live[0] - 1, 0)
    tile_expert = jnp.where(tile_start // tm < n_live[0], tile_expert, tile_expert[last_live])
    return dest.reshape(t, FINE_TOP_K), src_tok, tile_expert, n_live


def _rope_tables(max_len):
    rows = max_len // GRID_W
    row = jnp.repeat(jnp.arange(rows, dtype=F32), GRID_W)
    col = jnp.tile(jnp.arange(GRID_W, dtype=F32), rows)
    half = HEAD_DIM // 2
    inv_freq = 1.0 / (ROPE_THETA ** (jnp.arange(0, half, 2, dtype=F32) / half))
    ang_r = row[:, None] * inv_freq[None, :]
    ang_c = col[:, None] * inv_freq[None, :]
    ang = jnp.concatenate([ang_r, ang_r, ang_c, ang_c], axis=-1)
    sign = jnp.where((jnp.arange(HEAD_DIM) % (HEAD_DIM // 2)) < HEAD_DIM // 4, -1.0, 1.0)
    reps = LANES // HEAD_DIM
    return jnp.tile(jnp.cos(ang), (1, reps)), jnp.tile(jnp.sin(ang) * sign[None, :], (1, reps))


def _to_block_order(a):
    n, d = a.shape
    return a.reshape(n // TOKEN_BLOCK, CHUNKS_PER_BLOCK, SSM_CHUNK, d).transpose(0, 2, 1, 3).reshape(n, d)


def _from_block_order(a):
    n, d = a.shape
    return a.reshape(n // TOKEN_BLOCK, SSM_CHUNK, CHUNKS_PER_BLOCK, d).transpose(0, 2, 1, 3).reshape(n, d)


def kernel(x_prompt, x_sample, c_prompt, c_sample, w_ada, b_ada, g_norm1, g_norm2, w_in, ssm_a_re, ssm_a_im, ssm_log_dt, ssm_b_re, ssm_b_im, ssm_c_re, ssm_c_im, ssm_d, w_glu, b_glu, g_q, g_k, w_branch_ssm, w_branch_attn, w_out, w_coarse, b_coarse, w_fine, b_fine, w_expert_gate, w_expert_up, w_expert_down):
    b1, l1, d = x_prompt.shape
    b2, l2, _ = x_sample.shape
    depth = w_in.shape[0]
    d_ssm = w_glu.shape[-1]
    d_attn = w_branch_attn.shape[1]
    d_kv = N_KV_HEADS * HEAD_DIM
    n_heads = d_attn // HEAD_DIM
    n_groups = w_coarse.shape[-1]
    n_experts = w_fine.shape[-1]
    per_group = n_experts // n_groups
    n_ssm_groups = ssm_b_re.shape[2]
    ssm_h = ssm_b_re.shape[-1]
    tb = TOKEN_BLOCK
    assert l1 % tb == 0 and l2 % tb == 0 and l1 % KV_TILE == 0 and l2 % KV_TILE == 0
    assert d_attn == N_KV_HEADS * (n_heads // N_KV_HEADS) * HEAD_DIM and 2 * HEAD_DIM == LANES
    assert n_ssm_groups * ssm_h == d_ssm and d_ssm % LANES == 0 and LANES % ssm_h == 0
    assert 2 * ssm_a_re.shape[-1] == LANES
    assert n_experts + n_groups <= LANES

    seq_lens = [l1] * b1 + [l2] * b2
    t = sum(seq_lens)
    x = jnp.concatenate([x_prompt.reshape(b1 * l1, d), x_sample.reshape(b2 * l2, d)], axis=0)
    x = _to_block_order(x)
    c = jnp.concatenate([c_prompt, c_sample], axis=0)
    blk_seq = np.concatenate([np.full(n // tb, s, np.int32) for s, n in enumerate(seq_lens)])
    blk_pos = np.concatenate([np.arange(n // tb, dtype=np.int32) for n in seq_lens])
    blk_seq, blk_pos = jnp.asarray(blk_seq), jnp.asarray(blk_pos)
    cos_t, sin_t = (_to_block_order(tab) for tab in _rope_tables(max(l1, l2)))

    seg_rows = max(seq_lens) // SSM_CHUNK
    seg_seq_rows = []
    filled = 0
    for n in seq_lens:
        rows = n // SSM_CHUNK
        assert rows % SUBLANES == 0 and seg_rows % rows == 0
        if filled == 0:
            seg_seq_rows.append(rows)
        assert seg_seq_rows[-1] == rows
        filled = (filled + rows) % seg_rows
    assert filled == 0
    seg_seq_rows = jnp.asarray(np.array(seg_seq_rows, np.int32))

    rep = n_heads // N_KV_HEADS
    slot_head = np.array([g * rep + j for j in range(rep) for g in range(N_KV_HEADS)])
    col_perm = (slot_head[:, None] * HEAD_DIM + np.arange(HEAD_DIM)[None, :]).reshape(-1)
    in_perm = np.concatenate([np.arange(d_ssm), d_ssm + col_perm, np.arange(d_ssm + d_attn, w_in.shape[-1])])

    mod_all = _ada_modulation(c, w_ada, b_ada)
    gq_t = jnp.tile(g_q.astype(F32), (1, LANES // HEAD_DIM))
    gk_t = jnp.tile(g_k.astype(F32), (1, LANES // HEAD_DIM))
    head_mean = jnp.asarray(np.kron(np.eye(LANES // HEAD_DIM), np.full((HEAD_DIM, HEAD_DIM), 1.0 / HEAD_DIM)), BF16)
    w_router = jnp.concatenate([w_fine, w_coarse,
                                jnp.zeros((depth, d, LANES - n_experts - n_groups), F32)], -1).astype(BF16)
    b_router = jnp.concatenate([b_fine, b_coarse,
                                jnp.zeros((depth, LANES - n_experts - n_groups), F32)], -1)

    pending = None
    for i in range(depth):
        mod = mod_all[i]
        x, u_tm, q, kt, v, gates = _mix_in(
            x, pending, mod, g_norm1[i][None], w_in[i][:, in_perm].astype(BF16), cos_t, sin_t,
            gq_t[i][None], gk_t[i][None], head_mean, blk_seq, blk_pos, (d_ssm, d_attn, d_kv))

        w_in_rows, lag_blocks, w_out_cols, ssm_a = _ssm_weights(
            ssm_a_re[i], ssm_a_im[i], ssm_log_dt[i], ssm_b_re[i], ssm_b_im[i],
            ssm_c_re[i], ssm_c_im[i], ssm_d[i])
        states = _ssm_states(u_tm, w_in_rows, ssm_a, seg_rows, seg_seq_rows, ssm_h)
        y_tm = _ssm_output(u_tm, states, lag_blocks, w_out_cols, ssm_h)

        o_first = _attention(q, kt, v, 0, b1, l1)
        o_second = _attention(q, kt, v, b1 * l1, b2, l2)

        w_ba = w_branch_attn[i][col_perm].astype(BF16)
        x, h2, route, counts = _mix_out(
            x, y_tm, o_first, o_second, gates, mod, w_glu[i].astype(BF16), b_glu[i][None],
            w_branch_ssm[i].astype(BF16), w_ba, w_out[i].astype(BF16), g_norm2[i][None],
            w_router[i], b_router[i][None], blk_seq, n_groups, per_group)

        dest, src_tok, tile_expert, n_live = _dispatch_plan(route, counts, n_experts)
        xs = h2.at[src_tok].get(mode='promise_in_bounds')
        ys = _experts(xs, tile_expert, n_live, w_expert_gate[i], w_expert_up[i], w_expert_down[i])
        y1 = ys.at[dest[:, 0]].get(mode='promise_in_bounds')
        y2 = ys.at[dest[:, 1]].get(mode='promise_in_bounds')
        pending = (y1, y2, route, mod)

    x = _combine(x, *pending, blk_seq)
    x = _from_block_order(x)
    y_prompt = x[:b1 * l1].reshape(b1, l1, d)
    y_sample = x[b1 * l1:].reshape(b2, l2, d)
    return (y_prompt, y_sample)
```

```python
import functools
import math

import jax
import jax.numpy as jnp
import numpy as np
from jax import lax
from jax.experimental import pallas as pl
from jax.experimental.pallas import tpu as pltpu

F32 = jnp.float32
BF16 = jnp.bfloat16

EPS = 1e-6
GRID_W = 64
ROPE_THETA = 10000.0
HEAD_DIM = 64
N_KV_HEADS = 2
FINE_TOP_K = 2
LOG2E = 1.4426950408889634
LANES = 128
SUBLANES = 8
SSM_CHUNK = 16

TOKEN_BLOCK = 512
CHUNKS_PER_BLOCK = TOKEN_BLOCK // SSM_CHUNK
Q_TILE = 256
KV_TILE = TOKEN_BLOCK
KV_STEP_TILES = 8
EXPERT_TILE = 1024
SSM_ROW_BLOCK = 512
VMEM_LIMIT = 56 * 1024 * 1024


def _sigmoid(x):
    return 1.0 / (1.0 + jnp.exp(-x))


def _cparams(sem):
    return pltpu.CompilerParams(dimension_semantics=sem, vmem_limit_bytes=VMEM_LIMIT)


def _ada_kernel(c_ref, w_ref, b_ref, o_ref):
    c = c_ref[...]
    ca = c * _sigmoid(c)
    o_ref[0, 0] = jnp.dot(ca, w_ref[0], preferred_element_type=F32,
                          precision=lax.Precision.HIGHEST) + b_ref[0, 0]


def _ada_modulation(c, w_ada, b_ada):
    depth, d, _ = w_ada.shape
    nb = c.shape[0]
    n_mod = w_ada.shape[2] // d
    out = pl.pallas_call(
        _ada_kernel,
        grid=(depth, n_mod),
        in_specs=[
            pl.BlockSpec((nb, d), lambda l, j: (0, 0)),
            pl.BlockSpec((1, d, d), lambda l, j: (l, 0, j)),
            pl.BlockSpec((1, 1, 1, d), lambda l, j: (l, j, 0, 0)),
        ],
        out_specs=pl.BlockSpec((1, 1, nb, d), lambda l, j: (l, j, 0, 0)),
        out_shape=jax.ShapeDtypeStruct((depth, n_mod, nb, d), F32),
        compiler_params=_cparams(("arbitrary", "arbitrary")),
        name="ada_modulation",
    )(c, w_ada, b_ada.reshape(depth, n_mod, 1, d))
    return out.transpose(0, 2, 1, 3)


def _rope(x, cos, sin_signed, low_half):
    up = pltpu.roll(x, LANES - 16, axis=1)
    down = pltpu.roll(x, 16, axis=1)
    return x * cos + jnp.where(low_half, up, down) * sin_signed


def _moe_residual(x_ref, y1_ref, y2_ref, route_ref, mod_ref):
    route = route_ref[...]
    moe = route[:, 2:3] * y1_ref[...].astype(F32) + route[:, 3:4] * y2_ref[...].astype(F32)
    return x_ref[...] + mod_ref[0][5:6] * moe


def _mix_in_kernel(seq_ref, pos_ref, *refs, d_ssm, d_attn, d_kv, pending_moe):
    del seq_ref, pos_ref
    if pending_moe:
        x_ref, y1_ref, y2_ref, route_ref, prev_mod_ref = refs[:5]
        (mod_ref, g1_ref, w_ref, cos_ref, sin_ref, gq_ref, gk_ref, hm_ref,
         u_ref, q_ref, kt_ref, v_ref, gate_ref, x_out_ref) = refs[5:]
        x = _moe_residual(x_ref, y1_ref, y2_ref, route_ref, prev_mod_ref)
        x_out_ref[...] = x
    else:
        (x_ref, mod_ref, g1_ref, w_ref, cos_ref, sin_ref, gq_ref, gk_ref, hm_ref,
         u_ref, q_ref, kt_ref, v_ref, gate_ref) = refs
        x = x_ref[...]
    mod = mod_ref[0]
    xn = x * lax.rsqrt(jnp.mean(x * x, axis=-1, keepdims=True) + EPS) * g1_ref[...]
    h = (xn * (1.0 + mod[1:2]) + mod[0:1]).astype(BF16)

    def proj(lo, hi):
        return jnp.dot(h, w_ref[:, lo:hi], preferred_element_type=F32)

    u = proj(0, d_ssm).astype(BF16)
    n_c = u.shape[0] // SSM_CHUNK
    for t in range(SSM_CHUNK):
        u_ref[t] = u[t * n_c:(t + 1) * n_c]

    cos = cos_ref[...]
    sin_signed = sin_ref[...]
    lane = lax.broadcasted_iota(jnp.int32, cos.shape, 1)
    low_half = (lane % 32) < 16
    hm = hm_ref[...]

    def norm_rope(z, gain, scale):
        cols = []
        for j in range(z.shape[1] // LANES):
            zj = z[:, j * LANES:(j + 1) * LANES]
            msq = jnp.dot((zj * zj).astype(BF16), hm, preferred_element_type=F32)
            zn = zj * lax.rsqrt(msq + EPS) * gain
            cols.append(_rope(zn, cos, sin_signed, low_half) * scale)
        return jnp.concatenate(cols, axis=1) if len(cols) > 1 else cols[0]

    o = d_ssm
    q_ref[...] = norm_rope(proj(o, o + d_attn), gq_ref[...],
                           (HEAD_DIM ** -0.5) * LOG2E).astype(BF16)
    o += d_attn
    kt_ref[0] = norm_rope(proj(o, o + d_kv), gk_ref[...], 1.0).T.astype(BF16)
    o += d_kv
    v_ref[...] = proj(o, o + d_kv).astype(BF16)
    o += d_kv
    n_gate = gate_ref.shape[1]
    half = n_gate // 2
    gate_ref[:, :half] = _sigmoid(proj(o, o + half)).astype(BF16)
    gate_ref[:, half:] = _sigmoid(proj(o + half, o + n_gate)).astype(BF16)


def _mix_in(x, pending, mod, g1, w_in, cos_t, sin_t, gq_t, gk_t, hm, blk_seq, blk_pos, dims):
    t, d = x.shape
    d_ssm, d_attn, d_kv = dims
    n_gate = w_in.shape[1] - d_ssm - d_attn - 2 * d_kv
    tb = TOKEN_BLOCK
    row = lambda i, s, p: (i, 0)
    const = lambda i, s, p: (0, 0)
    by_seq = lambda i, s, p: (s[i], 0, 0)
    pending_specs, pending_args, x_out_spec, x_out_shape = [], [], [], []
    if pending is not None:
        y1, y2, route, prev_mod = pending
        pending_specs = [pl.BlockSpec((tb, d), row), pl.BlockSpec((tb, d), row),
                         pl.BlockSpec((tb, LANES), row), pl.BlockSpec((1,) + prev_mod.shape[1:], by_seq)]
        pending_args = [y1, y2, route, prev_mod]
        x_out_spec = [pl.BlockSpec((tb, d), row)]
        x_out_shape = [jax.ShapeDtypeStruct((t, d), F32)]
    grid_spec = pltpu.PrefetchScalarGridSpec(
        num_scalar_prefetch=2,
        grid=(t // tb,),
        in_specs=[pl.BlockSpec((tb, d), row)] + pending_specs + [
            pl.BlockSpec((1,) + mod.shape[1:], by_seq),
            pl.BlockSpec((1, d), const),
            pl.BlockSpec(w_in.shape, const),
            pl.BlockSpec((tb, LANES), lambda i, s, p: (p[i], 0)),
            pl.BlockSpec((tb, LANES), lambda i, s, p: (p[i], 0)),
            pl.BlockSpec((1, LANES), const),
            pl.BlockSpec((1, LANES), const),
            pl.BlockSpec((LANES, LANES), const),
        ],
        out_specs=[
            pl.BlockSpec((SSM_CHUNK, CHUNKS_PER_BLOCK, d_ssm), lambda i, s, p: (0, i, 0)),
            pl.BlockSpec((tb, d_attn), row),
            pl.BlockSpec((1, d_kv, tb), lambda i, s, p: (i, 0, 0)),
            pl.BlockSpec((tb, d_kv), row),
            pl.BlockSpec((tb, n_gate), row),
        ] + x_out_spec,
    )
    outs = pl.pallas_call(
        functools.partial(_mix_in_kernel, d_ssm=d_ssm, d_attn=d_attn, d_kv=d_kv,
                          pending_moe=pending is not None),
        grid_spec=grid_spec,
        out_shape=[
            jax.ShapeDtypeStruct((SSM_CHUNK, t // SSM_CHUNK, d_ssm), BF16),
            jax.ShapeDtypeStruct((t, d_attn), BF16),
            jax.ShapeDtypeStruct((t // tb, d_kv, tb), BF16),
            jax.ShapeDtypeStruct((t, d_kv), BF16),
            jax.ShapeDtypeStruct((t, n_gate), BF16),
        ] + x_out_shape,
        compiler_params=_cparams(("arbitrary",)),
        name="mix_in",
    )(blk_seq, blk_pos, x, *pending_args, mod, g1, w_in, cos_t, sin_t, gq_t, gk_t, hm)
    x_now = outs[5] if pending is not None else x
    return (x_now,) + tuple(outs[:5])


def _ssm_weights(a_re, a_im, log_dt, b_re, b_im, c_re, c_im, d_skip):
    n_dirs, g, p = a_re.shape
    hch = b_re.shape[-1]
    c = SSM_CHUNK
    nq = g * hch // LANES
    gq = g // nq
    lam = lax.complex(a_re.astype(F32), a_im.astype(F32))
    dt = jnp.exp(log_dt.astype(F32))[..., None]
    lam_dt = lam * dt
    coef = (jnp.exp(lam_dt) - 1.0) / lam
    steps = jnp.arange(c + 1, dtype=F32)
    pw = jnp.exp(lam_dt[:, None] * steps[None, :, None, None])
    bc = lax.complex(b_re.astype(F32), b_im.astype(F32)) * coef[..., None]
    cc = lax.complex(c_re.astype(F32), c_im.astype(F32))
    eye = jnp.eye(gq, dtype=F32)

    kern = jnp.einsum('rghp,rdgp,rgpk->rdghk', cc, pw[:, :c], bc).real
    dmat = jnp.eye(hch, dtype=F32)[None] * d_skip.astype(F32).reshape(g, hch)[:, :, None]
    lag0 = (kern[0, 0] + kern[1, 0] + dmat)[None]
    by_lag = jnp.concatenate([kern[1, 1:][::-1], lag0, kern[0, 1:]], 0)
    by_lag = by_lag.reshape(2 * c - 1, nq, gq, hch, hch)
    lag_blocks = jnp.einsum('dqkoh,gk->qdghko', by_lag, eye).reshape(nq, 2 * c - 1, LANES, LANES)

    win_f = pw[0, c - 1 - jnp.arange(c)][:, :, None, :] * bc[0].transpose(0, 2, 1)[None]
    win_b = pw[1, jnp.arange(c)][:, :, None, :] * bc[1].transpose(0, 2, 1)[None]

    def in_rows(w):
        parts = jnp.stack([w.real, w.imag], 0).reshape(2, c, nq, gq, hch, p)
        parts = parts.transpose(2, 0, 1, 3, 4, 5).reshape(nq, 2, c * gq * hch, p)
        return jnp.concatenate([parts, parts], -1)

    w_in_rows = jnp.stack([in_rows(win_f), in_rows(win_b)], 1)

    cout_f = cc[0][None] * pw[0, 1 + jnp.arange(c)][:, :, None, :]
    cout_b = cc[1][None] * pw[1, c - jnp.arange(c)][:, :, None, :]

    def out_cols(w):
        parts = jnp.stack([w.real, -w.imag], 0).reshape(2, c, nq, gq, hch, p)
        return parts.transpose(2, 0, 5, 1, 3, 4).reshape(nq, 2, p, c * gq * hch)

    w_out_cols = jnp.concatenate([out_cols(cout_f), out_cols(cout_b)], 1)

    sub = SUBLANES
    rows = jnp.arange(sub)
    scan_pos = jnp.stack([rows, sub - 1 - rows], 0)
    a_log = lam_dt * c
    tables = []
    d = 1
    while d < sub:
        tables.append(jnp.where((scan_pos >= d)[:, :, None, None], jnp.exp(a_log * d)[:, None], 0.0))
        d *= 2
    tables.append(jnp.exp(a_log[:, None] * (scan_pos + 1).astype(F32)[:, :, None, None]))
    a_tab = jnp.stack(tables, 1)
    a_tab = jnp.stack([a_tab.real, a_tab.imag], 2).reshape(n_dirs, len(tables), 2, sub, nq, gq * p)
    a_tab = a_tab.transpose(4, 0, 1, 2, 3, 5)
    return w_in_rows.astype(BF16), lag_blocks.astype(BF16), w_out_cols.astype(BF16), a_tab


def _group_of(index, per_group):
    return (index // per_group) % (LANES // per_group)


def _ssm_states_kernel(len_ref, u_ref, win_ref, a_ref, s_ref, w_sc, x_sc, s_sc, *, ssm_h):
    n_rows, width = x_sc.shape
    half = width // 2

    @pl.when(pl.program_id(2) == 0)
    def _():
        shape = win_ref.shape[3:]
        row_group = _group_of(lax.broadcasted_iota(jnp.int32, shape, 0), ssm_h)
        lane_half = lax.broadcasted_iota(jnp.int32, shape, 1) // (LANES // 2)
        for a in range(2):
            rows = win_ref[0, 0, a]
            for j in range(half // LANES):
                col = a * half + j * LANES
                w_sc[:, col:col + LANES] = jnp.where(row_group == 2 * j + lane_half, rows,
                                                     jnp.zeros_like(rows))

    ucat = jnp.concatenate([u_ref[t] for t in range(SSM_CHUNK)], axis=1)
    x_sc[...] = jnp.dot(ucat, w_sc[...], preferred_element_type=F32)
    sub = SUBLANES
    tile = (sub, half)
    row_id = lax.broadcasted_iota(jnp.int32, tile, 0)
    seq_rows = len_ref[pl.program_id(2)]
    n_tiles = n_rows // sub
    n_levels = a_ref.shape[2] - 1

    def cmul(c_re, c_im, z_re, z_im):
        return c_re * z_re - c_im * z_im, c_re * z_im + c_im * z_re

    def run(backward):
        def toward_later(z, dist):
            return pltpu.roll(z, sub - dist if backward else dist, axis=0)

        def body(i, carry):
            s_re, s_im = carry
            tile_i = n_tiles - 1 - i if backward else i
            row0 = pl.multiple_of(tile_i * sub, sub)
            first = (row0 + sub) if backward else row0
            keep = jnp.where(lax.rem(first, seq_rows) == 0, 0.0, 1.0).astype(F32)
            s_re = s_re * keep
            s_im = s_im * keep
            z_re = x_sc[pl.ds(row0, sub), 0:half]
            z_im = x_sc[pl.ds(row0, sub), half:width]
            for lvl in range(n_levels):
                m_re, m_im = cmul(a_ref[0, 0, lvl, 0], a_ref[0, 0, lvl, 1],
                                  toward_later(z_re, 2 ** lvl), toward_later(z_im, 2 ** lvl))
                z_re, z_im = z_re + m_re, z_im + m_im
            c_re, c_im = cmul(a_ref[0, 0, n_levels, 0], a_ref[0, 0, n_levels, 1], s_re, s_im)
            z_re, z_im = z_re + c_re, z_im + c_im
            entry = sub - 1 if backward else 0
            s_sc[pl.ds(row0, sub), 0:half] = jnp.where(row_id == entry, s_re, toward_later(z_re, 1))
            s_sc[pl.ds(row0, sub), half:width] = jnp.where(row_id == entry, s_im, toward_later(z_im, 1))
            last = sub - 1 - entry
            return (jnp.broadcast_to(z_re[last:last + 1], tile),
                    jnp.broadcast_to(z_im[last:last + 1], tile))
        zero = jnp.zeros(tile, F32)
        lax.fori_loop(0, n_tiles, body, (zero, zero))

    direction = pl.program_id(1)

    @pl.when(direction == 0)
    def _():
        run(False)

    @pl.when(direction == 1)
    def _():
        run(True)

    s_ref[...] = s_sc[...].astype(s_ref.dtype)


def _ssm_states(u_tm, w_in_rows, a_tab, seg_rows, seg_seq_rows, ssm_h):
    c, nc, d_ssm = u_tm.shape
    nq, n_dirs, _, k, two_p = w_in_rows.shape
    assert a_tab.shape[-2] == SUBLANES
    n = a_tab.shape[-1] * 2
    assert two_p == LANES and k == c * LANES
    grid_spec = pltpu.PrefetchScalarGridSpec(
        num_scalar_prefetch=1,
        grid=(nq, n_dirs, nc // seg_rows),
        in_specs=[
            pl.BlockSpec((c, seg_rows, LANES), lambda q, r, s, ln: (0, s, q)),
            pl.BlockSpec((1, 1, 2, k, two_p), lambda q, r, s, ln: (q, r, 0, 0, 0)),
            pl.BlockSpec((1, 1) + a_tab.shape[2:], lambda q, r, s, ln: (q, r, 0, 0, 0, 0)),
        ],
        out_specs=pl.BlockSpec((seg_rows, n), lambda q, r, s, ln: (s, q * n_dirs + r)),
        scratch_shapes=[pltpu.VMEM((k, n), BF16), pltpu.VMEM((seg_rows, n), F32),
                        pltpu.VMEM((seg_rows, n), F32)],
    )
    return pl.pallas_call(
        functools.partial(_ssm_states_kernel, ssm_h=ssm_h),
        grid_spec=grid_spec,
        out_shape=jax.ShapeDtypeStruct((nc, nq * n_dirs * n), BF16),
        compiler_params=_cparams(("arbitrary", "arbitrary", "arbitrary")),
        name="ssm_states",
    )(seg_seq_rows, u_tm, w_in_rows, a_tab)


def _ssm_output_kernel(u_ref, s_ref, lag_ref, wcol_ref, y_ref, wintra_sc, wout_sc, *, ssm_h):
    n_t_out = y_ref.shape[0]

    @pl.when(pl.program_id(2) == 0)
    def _():
        t_out0 = pl.program_id(1) * n_t_out
        for t_in in range(SSM_CHUNK):
            for j in range(n_t_out):
                wintra_sc[t_in * LANES:(t_in + 1) * LANES, j * LANES:(j + 1) * LANES] = (
                    lag_ref[0, t_out0 + j - t_in + SSM_CHUNK - 1])
        n_planes, p, n_cols = wcol_ref.shape[1:]
        col_group = _group_of(lax.broadcasted_iota(jnp.int32, (p, n_cols), 1), ssm_h)
        for r in range(n_planes):
            cols = wcol_ref[0, r]
            for k in range(LANES // ssm_h):
                row = (r * (LANES // ssm_h) + k) * p
                wout_sc[row:row + p, :] = jnp.where(col_group == k, cols, jnp.zeros_like(cols))

    ucat = jnp.concatenate([u_ref[t] for t in range(SSM_CHUNK)], axis=1)
    y = jnp.dot(ucat, wintra_sc[...], preferred_element_type=F32)
    y += jnp.dot(s_ref[...], wout_sc[...], preferred_element_type=F32)
    for t in range(n_t_out):
        y_ref[t] = y[:, t * LANES:(t + 1) * LANES].astype(y_ref.dtype)


def _ssm_output(u_tm, s, lag_blocks, w_out_cols, ssm_h):
    c, nc, d_ssm = u_tm.shape
    nq, n_planes, p, k = w_out_cols.shape
    n_half = 2
    n = k // n_half
    s_width = n_planes * (LANES // ssm_h) * p
    rb = min(SSM_ROW_BLOCK, nc)
    return pl.pallas_call(
        functools.partial(_ssm_output_kernel, ssm_h=ssm_h),
        grid=(nq, n_half, nc // rb),
        in_specs=[
            pl.BlockSpec((c, rb, LANES), lambda q, h, i: (0, i, q)),
            pl.BlockSpec((rb, s_width), lambda q, h, i: (i, q)),
            pl.BlockSpec((1,) + lag_blocks.shape[1:], lambda q, h, i: (q, 0, 0, 0)),
            pl.BlockSpec((1, n_planes, p, n), lambda q, h, i: (q, 0, 0, h)),
        ],
        out_specs=pl.BlockSpec((c // n_half, rb, LANES), lambda q, h, i: (h, i, q)),
        out_shape=jax.ShapeDtypeStruct((c, nc, d_ssm), BF16),
        scratch_shapes=[pltpu.VMEM((k, n), BF16), pltpu.VMEM((s_width, n), BF16)],
        compiler_params=_cparams(("arbitrary", "arbitrary", "arbitrary")),
        name="ssm_output",
    )(u_tm, s, lag_blocks, w_out_cols)


def _attn_kernel(q_ref, kt_ref, v_ref, o_ref, m_sc, l_sc, acc_sc, *, n_kv_tiles):
    tq = q_ref.shape[0]
    n_slices = q_ref.shape[1] // LANES
    q = jnp.concatenate([q_ref[:, j * LANES:(j + 1) * LANES] for j in range(n_slices)], axis=0)
    m_sc[...] = jnp.full(m_sc.shape, -jnp.inf, F32)
    l_sc[...] = jnp.zeros(l_sc.shape, F32)
    acc_sc[...] = jnp.zeros(acc_sc.shape, F32)
    step_tiles = min(KV_STEP_TILES, n_kv_tiles)
    kv_step_len = step_tiles * KV_TILE
    n_rep = kv_step_len // LANES
    lane_head = lax.broadcasted_iota(jnp.int32, acc_sc.shape, 1) // HEAD_DIM
    key_head = lax.broadcasted_iota(jnp.int32, (kt_ref.shape[1], kv_step_len), 0) // HEAD_DIM

    def kv_step(kb, carry):
        start = pl.multiple_of(kb * kv_step_len, kv_step_len)
        kt = jnp.concatenate([kt_ref[kb * step_tiles + i] for i in range(step_tiles)], axis=1)
        v = v_ref[pl.ds(start, kv_step_len), :]
        acc = acc_sc[...]
        for g in range(N_KV_HEADS):
            ktg = jnp.where(key_head == g, kt, jnp.zeros_like(kt))
            s = jnp.dot(q, ktg, preferred_element_type=F32)
            m_old = m_sc[g]
            m_new = jnp.maximum(m_old, jnp.max(s, axis=-1, keepdims=True))
            alpha = jnp.exp2(m_old - m_new)
            p = jnp.exp2(s - jnp.concatenate([m_new] * n_rep, axis=1))
            l_part = p[:, 0:LANES]
            for c in range(1, n_rep):
                l_part = l_part + p[:, c * LANES:(c + 1) * LANES]
            l_sc[g] = alpha * l_sc[g] + l_part
            m_sc[g] = m_new
            pv = jnp.dot(p.astype(BF16), v, preferred_element_type=F32)
            acc = jnp.where(lane_head == g, alpha * acc + pv, acc)
        acc_sc[...] = acc
        return carry

    lax.fori_loop(0, n_kv_tiles // step_tiles, kv_step, 0)
    inv = [1.0 / jnp.sum(l_sc[g], axis=-1, keepdims=True) for g in range(N_KV_HEADS)]
    out = acc_sc[...] * jnp.where(lane_head == 0, inv[0], inv[1])
    for j in range(n_slices):
        o_ref[:, j * LANES:(j + 1) * LANES] = out[j * tq:(j + 1) * tq].astype(o_ref.dtype)


def _attention(q, kt, v, first_token, n_seqs, seq_len):
    d_attn = q.shape[1]
    d_kv = v.shape[1]
    tq = min(Q_TILE, seq_len)
    q_tiles = seq_len // tq
    kv_tiles = seq_len // KV_TILE
    assert first_token % seq_len == 0
    seq0 = first_token // seq_len
    rows = (d_attn // LANES) * tq
    return pl.pallas_call(
        functools.partial(_attn_kernel, n_kv_tiles=kv_tiles),
        grid=(n_seqs, q_tiles),
        in_specs=[
            pl.BlockSpec((tq, d_attn), lambda i, j: ((seq0 + i) * q_tiles + j, 0)),
            pl.BlockSpec((kv_tiles, d_kv, KV_TILE), lambda i, j: (seq0 + i, 0, 0)),
            pl.BlockSpec((seq_len, d_kv), lambda i, j: (seq0 + i, 0)),
        ],
        out_specs=pl.BlockSpec((tq, d_attn), lambda i, j: (i * q_tiles + j, 0)),
        out_shape=jax.ShapeDtypeStruct((n_seqs * seq_len, d_attn), BF16),
        scratch_shapes=[pltpu.VMEM((N_KV_HEADS, rows, LANES), F32),
                        pltpu.VMEM((N_KV_HEADS, rows, LANES), F32),
                        pltpu.VMEM((rows, LANES), F32)],
        compiler_params=_cparams(("arbitrary", "arbitrary")),
        name="attention",
    )(q, kt, v)


def _mix_out_kernel(seq_ref, x_ref, y_ref, o1_ref, o2_ref, gate_ref, mod_ref, wglu_ref, bglu_ref,
                    wbs_ref, wba_ref, wout_ref, g2_ref, wr_ref, br_ref, ltri_ref,
                    x1_ref, h2_ref, route_ref, count_ref, cnt_sc,
                    *, n_groups, per_group, blocks_first):
    del seq_ref
    mod = mod_ref[0]
    y = jnp.concatenate([y_ref[t] for t in range(SSM_CHUNK)], axis=0).astype(F32)
    y = 0.5 * y * (1.0 + jnp.tanh(math.sqrt(2.0 / math.pi) * (y + 0.044715 * (y * y * y))))
    glu = jnp.dot(y.astype(BF16), wglu_ref[...], preferred_element_type=F32) + bglu_ref[...]
    ys = (y * _sigmoid(glu)).astype(BF16)
    d = x_ref.shape[1]
    gate = gate_ref[...].astype(F32)
    o = jnp.where(pl.program_id(0) < blocks_first, o1_ref[...], o2_ref[...])
    merged = gate[:, :d] * jnp.dot(ys, wbs_ref[...], preferred_element_type=F32)
    merged += gate[:, d:] * jnp.dot(o, wba_ref[...], preferred_element_type=F32)
    x1 = x_ref[...] + mod[2:3] * jnp.dot(merged.astype(BF16), wout_ref[...],
                                         preferred_element_type=F32)
    x1_ref[...] = x1
    xn = x1 * lax.rsqrt(jnp.mean(x1 * x1, axis=-1, keepdims=True) + EPS) * g2_ref[...]
    h2 = (xn * (1.0 + mod[4:5]) + mod[3:4]).astype(BF16)
    h2_ref[...] = h2

    n_exp = n_groups * per_group
    logits = jnp.dot(h2, wr_ref[...], preferred_element_type=F32) + br_ref[...]
    lane = lax.broadcasted_iota(jnp.int32, logits.shape, 1).astype(F32)
    neg = jnp.float32(-jnp.inf)
    big = jnp.float32(1 << 20)
    lc = jnp.where(lane >= n_exp, jnp.where(lane < n_exp + n_groups, logits, neg), neg)
    c_max = jnp.max(lc, axis=-1, keepdims=True)
    g_idx = jnp.min(jnp.where(lc == c_max, lane - n_exp, big), axis=-1, keepdims=True)
    pc_top = 1.0 / jnp.sum(jnp.exp(lc - c_max), axis=-1, keepdims=True)
    lf = jnp.where(lane >= g_idx * per_group,
                   jnp.where(lane < (g_idx + 1.0) * per_group, logits, neg), neg)
    f1 = jnp.max(lf, axis=-1, keepdims=True)
    e1 = jnp.min(jnp.where(lf == f1, lane, big), axis=-1, keepdims=True)
    lf2 = jnp.where(lane == e1, neg, lf)
    f2 = jnp.max(lf2, axis=-1, keepdims=True)
    e2 = jnp.min(jnp.where(lf2 == f2, lane, big), axis=-1, keepdims=True)
    r = jnp.exp(f2 - f1)
    w1 = pc_top / (1.0 + r)
    w2 = pc_top * r / (1.0 + r)

    @pl.when(pl.program_id(0) == 0)
    def _():
        cnt_sc[...] = jnp.zeros(cnt_sc.shape, F32)

    hot = jnp.where(lane == e1, 1.0, 0.0) + jnp.where(lane == e2, 1.0, 0.0)
    before = jnp.dot(ltri_ref[...], hot.astype(BF16), preferred_element_type=F32) + cnt_sc[0:1, :]
    r1 = jnp.sum(jnp.where(lane == e1, before, 0.0), axis=-1, keepdims=True)
    r2 = jnp.sum(jnp.where(lane == e2, before, 0.0), axis=-1, keepdims=True)
    cnt_sc[...] = cnt_sc[...] + jnp.sum(hot, axis=0, keepdims=True)
    count_ref[...] = cnt_sc[...]

    route = jnp.zeros(logits.shape, F32)
    for col, val in enumerate((e1, e2, w1, w2, r1, r2)):
        route = jnp.where(lane == col, val, route)
    route_ref[...] = route


def _mix_out(x, y_tm, o_first, o_second, gates, mod, w_glu, b_glu, w_bs, w_ba, w_out, g2,
             w_router, b_router, blk_seq, n_groups, per_group):
    t, d = x.shape
    tb = TOKEN_BLOCK
    blocks_first = o_first.shape[0] // tb
    row = lambda i, s: (i, 0)
    const = lambda i, s: (0, 0)
    grid_spec = pltpu.PrefetchScalarGridSpec(
        num_scalar_prefetch=1,
        grid=(t // tb,),
        in_specs=[
            pl.BlockSpec((tb, d), row),
            pl.BlockSpec((SSM_CHUNK, CHUNKS_PER_BLOCK, y_tm.shape[2]), lambda i, s: (0, i, 0)),
            pl.BlockSpec((tb, o_first.shape[1]), lambda i, s: (jnp.minimum(i, blocks_first - 1), 0)),
            pl.BlockSpec((tb, o_second.shape[1]), lambda i, s: (jnp.maximum(i - blocks_first, 0), 0)),
            pl.BlockSpec((tb, gates.shape[1]), row),
            pl.BlockSpec((1,) + mod.shape[1:], lambda i, s: (s[i], 0, 0)),
            pl.BlockSpec(w_glu.shape, const),
            pl.BlockSpec(b_glu.shape, const),
            pl.BlockSpec(w_bs.shape, const),
            pl.BlockSpec(w_ba.shape, const),
            pl.BlockSpec(w_out.shape, const),
            pl.BlockSpec(g2.shape, const),
            pl.BlockSpec(w_router.shape, const),
            pl.BlockSpec(b_router.shape, const),
            pl.BlockSpec((tb, tb), const),
        ],
        out_specs=[
            pl.BlockSpec((tb, d), row),
            pl.BlockSpec((tb, d), row),
            pl.BlockSpec((tb, LANES), row),
            pl.BlockSpec((SUBLANES, LANES), const),
        ],
        scratch_shapes=[pltpu.VMEM((SUBLANES, LANES), F32)],
    )
    strictly_lower = jnp.asarray(np.tril(np.ones((tb, tb), np.float32), -1), BF16)
    return pl.pallas_call(
        functools.partial(_mix_out_kernel, n_groups=n_groups, per_group=per_group,
                          blocks_first=blocks_first),
        grid_spec=grid_spec,
        out_shape=[
            jax.ShapeDtypeStruct((t, d), F32),
            jax.ShapeDtypeStruct((t, d), BF16),
            jax.ShapeDtypeStruct((t, LANES), F32),
            jax.ShapeDtypeStruct((SUBLANES, LANES), F32),
        ],
        compiler_params=_cparams(("arbitrary",)),
        name="mix_out",
    )(blk_seq, x, y_tm, o_first, o_second, gates, mod, w_glu, b_glu, w_bs, w_ba, w_out, g2,
      w_router, b_router, strictly_lower)


def _expert_kernel(te_ref, nt_ref, xs_ref, wg_ref, wu_ref, wd_ref, ys_ref, wg_sc, wu_sc, wd_sc):
    i = pl.program_id(0)
    live = i < nt_ref[0]
    new_expert = jnp.logical_or(i == 0, te_ref[i] != te_ref[jnp.maximum(i - 1, 0)])

    @pl.when(jnp.logical_and(live, new_expert))
    def _():
        wg_sc[...] = wg_ref[0].astype(BF16)
        wu_sc[...] = wu_ref[0].astype(BF16)
        wd_sc[...] = wd_ref[0].astype(BF16)

    @pl.when(live)
    def _():
        x = xs_ref[...]
        hg = jnp.dot(x, wg_sc[...], preferred_element_type=F32)
        hu = jnp.dot(x, wu_sc[...], preferred_element_type=F32)
        act = (hg * _sigmoid(hg) * hu).astype(BF16)
        ys_ref[...] = jnp.dot(act, wd_sc[...], preferred_element_type=F32).astype(ys_ref.dtype)

    @pl.when(jnp.logical_not(live))
    def _():
        ys_ref[...] = jnp.zeros(ys_ref.shape, ys_ref.dtype)


def _experts(xs, tile_expert, n_live, w_gate, w_up, w_down):
    p, d = xs.shape
    tm = EXPERT_TILE
    f = w_gate.shape[2]
    grid_spec = pltpu.PrefetchScalarGridSpec(
        num_scalar_prefetch=2,
        grid=(p // tm,),
        in_specs=[
            pl.BlockSpec((tm, d), lambda i, te, nt: (i, 0)),
            pl.BlockSpec((1, d, f), lambda i, te, nt: (te[i], 0, 0)),
            pl.BlockSpec((1, d, f), lambda i, te, nt: (te[i], 0, 0)),
            pl.BlockSpec((1, f, d), lambda i, te, nt: (te[i], 0, 0)),
        ],
        out_specs=pl.BlockSpec((tm, d), lambda i, te, nt: (i, 0)),
        scratch_shapes=[pltpu.VMEM((d, f), BF16), pltpu.VMEM((d, f), BF16), pltpu.VMEM((f, d), BF16)],
    )
    return pl.pallas_call(
        _expert_kernel,
        grid_spec=grid_spec,
        out_shape=jax.ShapeDtypeStruct((p, d), BF16),
        compiler_params=_cparams(("arbitrary",)),
        name="experts",
    )(tile_expert, n_live, xs, w_gate, w_up, w_down)


def _combine_kernel(seq_ref, x_ref, y1_ref, y2_ref, route_ref, mod_ref, o_ref):
    del seq_ref
    o_ref[...] = _moe_residual(x_ref, y1_ref, y2_ref, route_ref, mod_ref)


def _combine(x1, y1, y2, route, mod, blk_seq):
    t, d = x1.shape
    tb = TOKEN_BLOCK
    row = lambda i, s: (i, 0)
    grid_spec = pltpu.PrefetchScalarGridSpec(
        num_scalar_prefetch=1,
        grid=(t // tb,),
        in_specs=[
            pl.BlockSpec((tb, d), row),
            pl.BlockSpec((tb, d), row),
            pl.BlockSpec((tb, d), row),
            pl.BlockSpec((tb, LANES), row),
            pl.BlockSpec((1,) + mod.shape[1:], lambda i, s: (s[i], 0, 0)),
        ],
        out_specs=pl.BlockSpec((tb, d), row),
    )
    return pl.pallas_call(
        _combine_kernel,
        grid_spec=grid_spec,
        out_shape=jax.ShapeDtypeStruct((t, d), F32),
        compiler_params=_cparams(("arbitrary",)),
        name="moe_combine",
    )(blk_seq, x1, y1, y2, route, mod)


def _dispatch_plan(route, counts, n_experts):
    t = route.shape[0]
    tm = EXPERT_TILE
    eid = route[:, :FINE_TOP_K].astype(jnp.int32).reshape(-1)
    rank = route[:, 2 * FINE_TOP_K:3 * FINE_TOP_K].astype(jnp.int32).reshape(-1)
    counts = counts[0, :n_experts].astype(jnp.int32)
    padded = ((counts + tm - 1) // tm) * tm
    pad_end = jnp.cumsum(padded)
    pad_off = pad_end - padded
    dest = pad_off[eid] + rank
    n_rows = FINE_TOP_K * t + n_experts * tm
    n_tiles = n_rows // tm
    src_tok = (jnp.arange(n_rows, dtype=jnp.int32) % t).at[dest].set(
        jnp.arange(FINE_TOP_K * t, dtype=jnp.int32) // FINE_TOP_K,
        unique_indices=True, mode='promise_in_bounds')
    tile_start = jnp.arange(n_tiles, dtype=jnp.int32) * tm
    tile_expert = jnp.minimum(jnp.sum((pad_end[None, :] <= tile_start[:, None]).astype(jnp.int32), axis=1),
                              n_experts - 1)
    n_live = (pad_end[-1] // tm).astype(jnp.int32).reshape(1)
    last_live = jnp.maximum(n_live[0] - 1, 0)
    tile_expert = jnp.where(tile_start // tm < n_live[0], tile_expert, tile_expert[last_live])
    return dest.reshape(t, FINE_TOP_K), src_tok, tile_expert, n_live


def _rope_tables(max_len):
    rows = max_len // GRID_W
    row = jnp.repeat(jnp.arange(rows, dtype=F32), GRID_W)
    col = jnp.tile(jnp.arange(GRID_W, dtype=F32), rows)
    half = HEAD_DIM // 2
    inv_freq = 1.0 / (ROPE_THETA ** (jnp.arange(0, half, 2, dtype=F32) / half))
    ang_r = row[:, None] * inv_freq[None, :]
    ang_c = col[:, None] * inv_freq[None, :]
    ang = jnp.concatenate([ang_r, ang_r, ang_c, ang_c], axis=-1)
    sign = jnp.where((jnp.arange(HEAD_DIM) % (HEAD_DIM // 2)) < HEAD_DIM // 4, -1.0, 1.0)
    reps = LANES // HEAD_DIM
    return jnp.tile(jnp.cos(ang), (1, reps)), jnp.tile(jnp.sin(ang) * sign[None, :], (1, reps))


def _to_block_order(a):
    n, d = a.shape
    return a.reshape(n // TOKEN_BLOCK, CHUNKS_PER_BLOCK, SSM_CHUNK, d).transpose(0, 2, 1, 3).reshape(n, d)


def _from_block_order(a):
    n, d = a.shape
    return a.reshape(n // TOKEN_BLOCK, SSM_CHUNK, CHUNKS_PER_BLOCK, d).transpose(0, 2, 1, 3).reshape(n, d)


def kernel(x_prompt, x_sample, c_prompt, c_sample, w_ada, b_ada, g_norm1, g_norm2, w_in, ssm_a_re, ssm_a_im, ssm_log_dt, ssm_b_re, ssm_b_im, ssm_c_re, ssm_c_im, ssm_d, w_glu, b_glu, g_q, g_k, w_branch_ssm, w_branch_attn, w_out, w_coarse, b_coarse, w_fine, b_fine, w_expert_gate, w_expert_up, w_expert_down):
    b1, l1, d = x_prompt.shape
    b2, l2, _ = x_sample.shape
    depth = w_in.shape[0]
    d_ssm = w_glu.shape[-1]
    d_attn = w_branch_attn.shape[1]
    d_kv = N_KV_HEADS * HEAD_DIM
    n_heads = d_attn // HEAD_DIM
    n_groups = w_coarse.shape[-1]
    n_experts = w_fine.shape[-1]
    per_group = n_experts // n_groups
    n_ssm_groups = ssm_b_re.shape[2]
    ssm_h = ssm_b_re.shape[-1]
    tb = TOKEN_BLOCK
    assert l1 % tb == 0 and l2 % tb == 0 and l1 % KV_TILE == 0 and l2 % KV_TILE == 0
    assert d_attn == N_KV_HEADS * (n_heads // N_KV_HEADS) * HEAD_DIM and 2 * HEAD_DIM == LANES
    assert n_ssm_groups * ssm_h == d_ssm and d_ssm % LANES == 0 and LANES % ssm_h == 0
    assert 2 * ssm_a_re.shape[-1] == LANES
    assert n_experts + n_groups <= LANES

    seq_lens = [l1] * b1 + [l2] * b2
    t = sum(seq_lens)
    x = jnp.concatenate([x_prompt.reshape(b1 * l1, d), x_sample.reshape(b2 * l2, d)], axis=0)
    x = _to_block_order(x)
    c = jnp.concatenate([c_prompt, c_sample], axis=0)
    blk_seq = np.concatenate([np.full(n // tb, s, np.int32) for s, n in enumerate(seq_lens)])
    blk_pos = np.concatenate([np.arange(n // tb, dtype=np.int32) for n in seq_lens])
    blk_seq, blk_pos = jnp.asarray(blk_seq), jnp.asarray(blk_pos)
    cos_t, sin_t = (_to_block_order(tab) for tab in _rope_tables(max(l1, l2)))

    seg_rows = max(seq_lens) // SSM_CHUNK
    seg_seq_rows = []
    filled = 0
    for n in seq_lens:
        rows = n // SSM_CHUNK
        assert rows % SUBLANES == 0 and seg_rows % rows == 0
        if filled == 0:
            seg_seq_rows.append(rows)
        assert seg_seq_rows[-1] == rows
        filled = (filled + rows) % seg_rows
    assert filled == 0
    seg_seq_rows = jnp.asarray(np.array(seg_seq_rows, np.int32))

    rep = n_heads // N_KV_HEADS
    slot_head = np.array([g * rep + j for j in range(rep) for g in range(N_KV_HEADS)])
    col_perm = (slot_head[:, None] * HEAD_DIM + np.arange(HEAD_DIM)[None, :]).reshape(-1)
    in_perm = np.concatenate([np.arange(d_ssm), d_ssm + col_perm, np.arange(d_ssm + d_attn, w_in.shape[-1])])

    mod_all = _ada_modulation(c, w_ada, b_ada)
    gq_t = jnp.tile(g_q.astype(F32), (1, LANES // HEAD_DIM))
    gk_t = jnp.tile(g_k.astype(F32), (1, LANES // HEAD_DIM))
    head_mean = jnp.asarray(np.kron(np.eye(LANES // HEAD_DIM), np.full((HEAD_DIM, HEAD_DIM), 1.0 / HEAD_DIM)), BF16)
    w_router = jnp.concatenate([w_fine, w_coarse,
                                jnp.zeros((depth, d, LANES - n_experts - n_groups), F32)], -1).astype(BF16)
    b_router = jnp.concatenate([b_fine, b_coarse,
                                jnp.zeros((depth, LANES - n_experts - n_groups), F32)], -1)

    pending = None
    for i in range(depth):
        mod = mod_all[i]
        x, u_tm, q, kt, v, gates = _mix_in(
            x, pending, mod, g_norm1[i][None], w_in[i][:, in_perm].astype(BF16), cos_t, sin_t,
            gq_t[i][None], gk_t[i][None], head_mean, blk_seq, blk_pos, (d_ssm, d_attn, d_kv))

        w_in_rows, lag_blocks, w_out_cols, ssm_a = _ssm_weights(
            ssm_a_re[i], ssm_a_im[i], ssm_log_dt[i], ssm_b_re[i], ssm_b_im[i],
            ssm_c_re[i], ssm_c_im[i], ssm_d[i])
        states = _ssm_states(u_tm, w_in_rows, ssm_a, seg_rows, seg_seq_rows, ssm_h)
        y_tm = _ssm_output(u_tm, states, lag_blocks, w_out_cols, ssm_h)

        o_first = _attention(q, kt, v, 0, b1, l1)
        o_second = _attention(q, kt, v, b1 * l1, b2, l2)

        w_ba = w_branch_attn[i][col_perm].astype(BF16)
        x, h2, route, counts = _mix_out(
            x, y_tm, o_first, o_second, gates, mod, w_glu[i].astype(BF16), b_glu[i][None],
            w_branch_ssm[i].astype(BF16), w_ba, w_out[i].astype(BF16), g_norm2[i][None],
            w_router[i], b_router[i][None], blk_seq, n_groups, per_group)

        dest, src_tok, tile_expert, n_live = _dispatch_plan(route, counts, n_experts)
        xs = h2.at[src_tok].get(mode='promise_in_bounds')
        ys = _experts(xs, tile_expert, n_live, w_expert_gate[i], w_expert_up[i], w_expert_down[i])
        y1 = ys.at[dest[:, 0]].get(mode='promise_in_bounds')
        y2 = ys.at[dest[:, 1]].get(mode='promise_in_bounds')
        pending = (y1, y2, route, mod)

    x = _combine(x, *pending, blk_seq)
    x = _from_block_order(x)
    y_prompt = x[:b1 * l1].reshape(b1, l1, d)
    y_sample = x[b1 * l1:].reshape(b2, l2, d)
    return (y_prompt, y_sample)
```
